```python
import math
import jax
import jax.numpy as jnp
from jax import lax
import numpy as np

D_MODEL = 1024
BATCH = 2
SEQ = 8192
DEPTH = 2

GRID_W = 64
CTX_LEN = 256
BRANCH_W = D_MODEL // 2
N_BRANCH = 3
GLA_HEADS = 4
GLA_DV = BRANCH_W // GLA_HEADS
GLA_DK = GLA_DV // 2
GLA_RANK = 16
GLA_TAU = 16.0
GLA_CHUNK = 16
MLA_HEADS = 8
MLA_DV = BRANCH_W // MLA_HEADS
MLA_NOPE = MLA_DV
MLA_ROPE = MLA_NOPE // 2
MLA_Q_RANK = 3 * D_MODEL // 8
MLA_KV_RANK = D_MODEL // 4
ROPE_BASE = 10000.0
Q_BLOCK = 128
HY_W = BRANCH_W
HY_ORDER = 2
HY_DIRS = 2
HY_BANDS = 16
HY_EMB = 2 * HY_BANDS + 1
HY_FFN = 64
HY_DECAY_TARGET = 1e-2
HY_FAST_PCT = 0.3
HY_SLOW_PCT = 1.5
N_EXPERTS = 32
TOP_K = 4
D_FF = D_MODEL
SWIGLU_LIMIT = 7.0
SWIGLU_ALPHA = 1.702
MOE_BLOCK = 128
LN_EPS = 1e-5
RMS_EPS = 1e-6
DEEPNORM_ALPHA = (2 * DEPTH) ** 0.25
DEEPNORM_BETA = (8 * DEPTH) ** -0.25
IN_SPLITS = (
    GLA_HEADS * GLA_DK,
    GLA_HEADS * GLA_DV,
    GLA_RANK,
    GLA_RANK,
    MLA_KV_RANK,
    MLA_ROPE,
    GLA_HEADS * GLA_DK,
    GLA_HEADS * GLA_DV,
    MLA_Q_RANK,
    3 * HY_W,
    N_BRANCH * D_MODEL,
)
N_KEY_GROUPS = 6
KEY_COLS = sum(IN_SPLITS[:N_KEY_GROUPS])
IN_OFFSETS = tuple(int(o) for o in np.cumsum(IN_SPLITS)[:-1])
IN_TOTAL = sum(IN_SPLITS)
F32 = jnp.float32

kernel_name = 'hybrid_gla_mla_hyena_moe_diffusion_block'


def layer_norm(x, gain=None, bias=None):
    xf = x.astype(F32)
    xc = xf - xf.mean(-1, keepdims=True)
    y = xc * lax.rsqrt((xc * xc).mean(-1, keepdims=True) + LN_EPS)
    if gain is not None:
        y = y * gain.astype(F32) + bias.astype(F32)
    return y.astype(x.dtype)


def rms_norm(x, gain):
    xf = x.astype(F32)
    y = xf * lax.rsqrt(jnp.mean(xf * xf, -1, keepdims=True) + RMS_EPS) * gain.astype(F32)
    return y.astype(x.dtype)


def modulate(x, shift, scale):
    return layer_norm(x) * (1 + scale) + shift


def to_heads(t, n):
    B, L, _ = t.shape
    return t.reshape(B, L, n, -1).transpose(0, 2, 1, 3)


def from_heads(t):
    B, H, L, d = t.shape
    return t.transpose(0, 2, 1, 3).reshape(B, L, H * d)


def _rev(t):
    return None if t is None else jnp.flip(t, axis=2)


def axial_rope_angles(L):
    rows = L // GRID_W
    row = jnp.repeat(jnp.arange(rows, dtype=F32), GRID_W)
    col = jnp.tile(jnp.arange(GRID_W, dtype=F32), rows)
    n_freq = MLA_ROPE // 4
    inv = ROPE_BASE ** (-jnp.arange(n_freq, dtype=F32) / n_freq)
    ang = jnp.concatenate([row[:, None] * inv, col[:, None] * inv], -1)
    return jnp.cos(ang), jnp.sin(ang)


def apply_rope(x, cos, sin):
    half = x.shape[-1] // 2
    x1, x2 = x[..., :half].astype(F32), x[..., half:].astype(F32)
    return jnp.concatenate([x1 * cos - x2 * sin, x2 * cos + x1 * sin], -1).astype(x.dtype)


def attend_blocks(q, k, v):
    B, H, Lq, dq = q.shape
    nb = Lq // Q_BLOCK
    scale = dq ** -0.5
    qb = q.reshape(B, H, nb, Q_BLOCK, dq).transpose(2, 0, 1, 3, 4)

    def one_block(qi):
        s = jnp.einsum('bhqd,bhkd->bhqk', qi, k).astype(F32) * scale
        p = jax.nn.softmax(s, axis=-1)
        return jnp.einsum('bhqk,bhkd->bhqd', p.astype(v.dtype), v)

    o = lax.map(one_block, qb)
    return o.transpose(1, 2, 0, 3, 4).reshape(B, H, Lq, v.shape[-1])


def _chunk(t):
    B, H, L, d = t.shape
    return t.reshape(B, H, L // GLA_CHUNK, GLA_CHUNK, d)


def gla_states(k, v, b, s0):
    b_last = b[:, :, :, -1:, :]
    delta = jnp.einsum('bhncd,bhnce->bhnde', k * jnp.exp(b_last - b), v)
    dec = jnp.exp(b_last[:, :, :, 0, :])

    def step(s, inp):
        d_n, delta_n = inp
        return d_n[..., None] * s + delta_n, s

    s_final, s_prev = lax.scan(step, s0, (jnp.moveaxis(dec, 2, 0), jnp.moveaxis(delta, 2, 0)))
    return jnp.moveaxis(s_prev, 0, 2), s_final


def gla_outputs(q, k, v, b, s_prev):
    C = q.shape[3]
    lower = jnp.tril(jnp.ones((C, C), bool))
    diff = b[:, :, :, :, None, :] - b[:, :, :, None, :, :]
    decay = jnp.exp(jnp.where(lower[:, :, None], diff, -jnp.inf))
    scores = jnp.einsum('bhnid,bhnjd,bhnijd->bhnij', q, k, decay)
    o = (jnp.einsum('bhnij,bhnje->bhnie', scores, v)
         + jnp.einsum('bhnid,bhnde->bhnie', q * jnp.exp(b), s_prev))
    B, H, N, _, dv = o.shape
    return o.reshape(B, H, N * C, dv)


def gla_direction(q, k, v, g, qc, kc, vc, gc):
    B, H, _, dk = k.shape
    s0 = jnp.zeros((B, H, dk, v.shape[-1]), F32)
    kc_, vc_ = _chunk(kc), _chunk(vc)
    bc = jnp.cumsum(_chunk(gc), axis=3)
    sc_prev, sc_final = gla_states(kc_, vc_, bc, s0)
    oc = None if qc is None else gla_outputs(_chunk(qc), kc_, vc_, bc, sc_prev)
    k_, v_ = _chunk(k), _chunk(v)
    b = jnp.cumsum(_chunk(g), axis=3)
    s_prev, _ = gla_states(k_, v_, b, sc_final)
    return gla_outputs(_chunk(q), k_, v_, b, s_prev), oc


def gla_keys(pk, pv, paf, pab, wa2_f, ba_f, wa2_b, ba_b):
    k = to_heads(pk, GLA_HEADS)
    v = to_heads(pv, GLA_HEADS)
    g_f = to_heads(jax.nn.log_sigmoid((paf @ wa2_f + ba_f).astype(F32)) / GLA_TAU, GLA_HEADS)
    g_b = to_heads(jax.nn.log_sigmoid((pab @ wa2_b + ba_b).astype(F32)) / GLA_TAU, GLA_HEADS)
    return k, v, g_f, g_b


def gla_query(pq):
    return to_heads(pq, GLA_HEADS) * GLA_DK ** -0.5


def gla_finish(o, r, norm_w):
    return from_heads(rms_norm(o, norm_w)).astype(r.dtype) * jax.nn.silu(r)


def mla_keys(pkva, pkr, kv_norm, w_ukv, rope):
    kv = to_heads(rms_norm(pkva, kv_norm) @ w_ukv, MLA_HEADS)
    k_nope, v = kv[..., :MLA_NOPE], kv[..., MLA_NOPE:]
    k_rope = pkr[:, None]
    if rope is not None:
        k_rope = apply_rope(k_rope, *rope)
    k_rope = jnp.broadcast_to(k_rope, k_nope.shape[:-1] + (MLA_ROPE,))
    return jnp.concatenate([k_nope, k_rope], -1), v


def mla_query(pqa, q_norm, w_uq, rope):
    q = to_heads(rms_norm(pqa, q_norm) @ w_uq, MLA_HEADS)
    if rope is not None:
        q = jnp.concatenate([q[..., :MLA_NOPE], apply_rope(q[..., MLA_NOPE:], *rope)], -1)
    return q


def short_conv3(x, w, b):
    xp = jnp.pad(x, ((0, 0), (1, 1), (0, 0)))
    return xp[:, :-2] * w[0] + xp[:, 1:-1] * w[1] + xp[:, 2:] * w[2] + b


def hyena_filters(L, w1, b1, w2, b2, w3, freq):
    t = jnp.linspace(0.0, 1.0, L, dtype=F32)[:, None]
    w = 2 * math.pi * jnp.arange(L, dtype=F32)[:, None] / L
    f = jnp.linspace(1e-4, HY_BANDS - 1, HY_BANDS, dtype=F32)
    z = jnp.concatenate([t, jnp.cos(f * w), -jnp.sin(f * w)], -1)
    fr = freq.astype(F32)
    h = jnp.sin(fr * (z @ w1.astype(F32) + b1.astype(F32)))
    h = jnp.sin(fr * (h @ w2.astype(F32) + b2.astype(F32)))
    h = (h @ w3.astype(F32)).reshape(L, HY_ORDER, HY_DIRS, HY_W)
    deltas = jnp.abs(jnp.linspace(math.log(HY_DECAY_TARGET) / HY_SLOW_PCT,
                                  math.log(HY_DECAY_TARGET) / HY_FAST_PCT, HY_W, dtype=F32))
    h = h * jnp.exp(-t * deltas)[:, None, None, :]
    return h / jnp.sum(jnp.abs(h), axis=(0, 2), keepdims=True)


def two_sided_fftconv(z, h_fwd, h_bwd, bias):
    B, L, C = z.shape
    k = jnp.concatenate([h_fwd, jnp.zeros((1, C), F32), jnp.flip(h_bwd[1:], 0)], 0)
    zf = jnp.fft.rfft(z.astype(F32), n=2 * L, axis=1)
    kf = jnp.fft.rfft(k, n=2 * L, axis=0)
    y = jnp.fft.irfft(zf * kf[None], n=2 * L, axis=1)[:, :L]
    return (y + z.astype(F32) * bias.astype(F32)).astype(z.dtype)


def hyena_branch(p, conv_w, conv_b, w1, b1, w2, b2, w3, freq, hbias):
    v, x1, x2 = jnp.split(short_conv3(p, conv_w, conv_b), 3, axis=-1)
    filt = hyena_filters(p.shape[1], w1, b1, w2, b2, w3, freq)
    z = x1 * two_sided_fftconv(v, filt[:, 0, 0], filt[:, 0, 1], hbias[0])
    return x2 * two_sided_fftconv(z, filt[:, 1, 0], filt[:, 1, 1], hbias[1])


def merge_branches(o_gla, o_mla, o_hy, gates, w_br_gla, w_br_mla, w_br_hy, w_out):
    g1, g2, g3 = jnp.split(jax.nn.sigmoid(gates.astype(F32)).astype(o_gla.dtype), 3, axis=-1)
    y = g1 * (o_gla @ w_br_gla) + g2 * (o_mla @ w_br_mla) + g3 * (o_hy @ w_br_hy)
    return y @ w_out


def token_mixer(h, hc, rope, with_ctx_out, w_in, gla_wa2_f, gla_ba_f, gla_wa2_b, gla_ba_b, gla_norm,
                mla_q_norm, mla_w_uq, mla_kv_norm, mla_w_ukv,
                hy_conv_w, hy_conv_b, hy_w1, hy_b1, hy_w2, hy_b2, hy_w3, hy_freq, hy_bias,
                w_br_gla, w_br_mla, w_br_hy, w_out):
    gk, gv, gaf, gab, mkva, mkr, gq, gr, mqa, hy, gates = jnp.split(h @ w_in, IN_OFFSETS, axis=-1)
    if with_ctx_out:
        ck, cv, caf, cab, ckva, ckr, cq, cr, cqa, chy, cgates = jnp.split(hc @ w_in, IN_OFFSETS, axis=-1)
    else:
        ck, cv, caf, cab, ckva, ckr = jnp.split(hc @ w_in[:, :KEY_COLS], IN_OFFSETS[:N_KEY_GROUPS - 1], axis=-1)
    gate_w = (gla_wa2_f, gla_ba_f, gla_wa2_b, gla_ba_b)
    k, v, g_f, g_b = gla_keys(gk, gv, gaf, gab, *gate_w)
    kc, vc, gc_f, gc_b = gla_keys(ck, cv, caf, cab, *gate_w)
    q = gla_query(gq)
    qc = gla_query(cq) if with_ctx_out else None
    o_f, oc_f = gla_direction(q, k, v, g_f, qc, kc, vc, gc_f)
    o_b, oc_b = gla_direction(_rev(q), _rev(k), _rev(v), _rev(g_b), _rev(qc), _rev(kc), _rev(vc), _rev(gc_b))
    o_gla = gla_finish(o_f + _rev(o_b), gr, gla_norm)
    mk, mv = mla_keys(mkva, mkr, mla_kv_norm, mla_w_ukv, rope)
    mkc, mvc = mla_keys(ckva, ckr, mla_kv_norm, mla_w_ukv, None)
    mq = mla_query(mqa, mla_q_norm, mla_w_uq, rope)
    o_mla = from_heads(attend_blocks(mq, jnp.concatenate([mkc, mk], 2), jnp.concatenate([mvc, mv], 2)))
    hy_w = (hy_conv_w, hy_conv_b, hy_w1, hy_b1, hy_w2, hy_b2, hy_w3, hy_freq, hy_bias)
    o_hy = hyena_branch(hy, *hy_w)
    br_w = (w_br_gla, w_br_mla, w_br_hy, w_out)
    y = merge_branches(o_gla, o_mla, o_hy, gates, *br_w)
    if not with_ctx_out:
        return y, None
    oc_gla = gla_finish(oc_f + _rev(oc_b), cr, gla_norm)
    oc_mla = from_heads(attend_blocks(mla_query(cqa, mla_q_norm, mla_w_uq, None), mkc, mvc))
    oc_hy = hyena_branch(chy, *hy_w)
    yc = merge_branches(oc_gla, oc_mla, oc_hy, cgates, *br_w)
    return y, yc


def moe_ffn(h, router_w, router_b, w1, b1, w2, b2):
    N, D = h.shape
    logits = (h @ router_w + router_b).astype(F32)
    top_val, top_idx = lax.top_k(logits, TOP_K)
    gates = jax.nn.softmax(top_val, axis=-1)
    NK = N * TOP_K
    flat_e = top_idx.reshape(NK)
    order = jnp.argsort(flat_e)
    sorted_e = flat_e[order]
    sorted_tok = (order // TOP_K).astype(jnp.int32)
    counts = jnp.bincount(flat_e, length=N_EXPERTS)
    padded = (counts + MOE_BLOCK - 1) // MOE_BLOCK * MOE_BLOCK
    pad_end = jnp.cumsum(padded)
    pad_start = pad_end - padded
    start = jnp.cumsum(counts) - counts
    dest = pad_start[sorted_e] + (jnp.arange(NK) - start[sorted_e])
    n_blocks = -(-NK // MOE_BLOCK) + N_EXPERTS
    buf_tok = jnp.zeros((n_blocks * MOE_BLOCK,), jnp.int32).at[dest].set(sorted_tok)
    block_e = jnp.minimum(jnp.searchsorted(pad_end, jnp.arange(n_blocks) * MOE_BLOCK, side='right'), N_EXPERTS - 1)
    xb = h[buf_tok].reshape(n_blocks, MOE_BLOCK, D)

    def expert_block(args):
        xi, e = args
        glu, lin = jnp.split(xi @ w1[e] + b1[e], 2, axis=-1)
        glu = jnp.minimum(glu, SWIGLU_LIMIT)
        lin = jnp.clip(lin, -SWIGLU_LIMIT, SWIGLU_LIMIT)
        act = glu * jax.nn.sigmoid(SWIGLU_ALPHA * glu) * (lin + 1)
        return act @ w2[e] + b2[e]

    yb = lax.map(expert_block, (xb, block_e)).reshape(-1, D)
    y_assign = yb[dest] * gates.reshape(NK)[order][:, None].astype(yb.dtype)
    return jax.ops.segment_sum(y_assign, sorted_tok, num_segments=N)


def setup_inputs(seed: int = 0) -> dict:
    key = jax.random.key(seed)
    ks = iter(jax.random.split(key, 48))

    def nrm(shape, scale):
        return jax.random.normal(next(ks), shape, F32) * scale

    def gain(shape):
        return 1.0 + nrm(shape, 0.05)

    L_, D = DEPTH, D_MODEL
    return {
        'x': nrm((BATCH, SEQ, D), 1.0),
        'c': nrm((BATCH, D), 1.0),
        'ctx': nrm((BATCH, CTX_LEN, D), 1.0),
        'c_ctx': nrm((D,), 1.0),
        'ada_w': nrm((L_, D, 6 * D), 0.5 * D ** -0.5),
        'ada_b': nrm((L_, 6 * D), 0.02),
        'w_in': nrm((L_, D, IN_TOTAL), D ** -0.5),
        'gla_wa2_f': nrm((L_, GLA_RANK, GLA_HEADS * GLA_DK), GLA_RANK ** -0.5),
        'gla_ba_f': nrm((L_, GLA_HEADS * GLA_DK), 0.1),
        'gla_wa2_b': nrm((L_, GLA_RANK, GLA_HEADS * GLA_DK), GLA_RANK ** -0.5),
        'gla_ba_b': nrm((L_, GLA_HEADS * GLA_DK), 0.1),
        'gla_norm': gain((L_, GLA_DV)),
        'mla_q_norm': gain((L_, MLA_Q_RANK)),
        'mla_w_uq': nrm((L_, MLA_Q_RANK, MLA_HEADS * (MLA_NOPE + MLA_ROPE)), MLA_Q_RANK ** -0.5),
        'mla_kv_norm': gain((L_, MLA_KV_RANK)),
        'mla_w_ukv': nrm((L_, MLA_KV_RANK, MLA_HEADS * (MLA_NOPE + MLA_DV)), MLA_KV_RANK ** -0.5),
        'hy_conv_w': nrm((L_, 3, 3 * HY_W), 3 ** -0.5),
        'hy_conv_b': nrm((L_, 3 * HY_W), 0.02),
        'hy_w1': nrm((L_, HY_EMB, HY_FFN), HY_EMB ** -0.5),
        'hy_b1': nrm((L_, HY_FFN), 0.1),
        'hy_w2': nrm((L_, HY_FFN, HY_FFN), HY_FFN ** -0.5),
        'hy_b2': nrm((L_, HY_FFN), 0.1),
        'hy_w3': nrm((L_, HY_FFN, HY_ORDER * HY_DIRS * HY_W), HY_FFN ** -0.5),
        'hy_freq': gain((L_, HY_FFN)),
        'hy_bias': nrm((L_, HY_ORDER, HY_W), 1.0),
        'w_br_gla': nrm((L_, BRANCH_W, D), BRANCH_W ** -0.5 * DEEPNORM_BETA),
        'w_br_mla': nrm((L_, BRANCH_W, D), BRANCH_W ** -0.5 * DEEPNORM_BETA),
        'w_br_hy': nrm((L_, BRANCH_W, D), BRANCH_W ** -0.5 * DEEPNORM_BETA),
        'w_out': nrm((L_, D, D), D ** -0.5 * DEEPNORM_BETA),
        'ln1_g': gain((L_, D)),
        'ln1_b': nrm((L_, D), 0.02),
        'ln2_g': gain((L_, D)),
        'ln2_b': nrm((L_, D), 0.02),
        'router_w': nrm((L_, D, N_EXPERTS), D ** -0.5),
        'router_b': nrm((L_, N_EXPERTS), 0.01),
        'moe_w1': nrm((L_, N_EXPERTS, D, 2 * D_FF), D ** -0.5),
        'moe_b1': nrm((L_, N_EXPERTS, 2 * D_FF), 0.02),
        'moe_w2': nrm((L_, N_EXPERTS, D_FF, D), D_FF ** -0.5 * DEEPNORM_BETA),
        'moe_b2': nrm((L_, N_EXPERTS, D), 0.02),
    }


def reference(x, c, ctx, c_ctx, ada_w, ada_b, w_in, gla_wa2_f, gla_ba_f, gla_wa2_b, gla_ba_b, gla_norm,
              mla_q_norm, mla_w_uq, mla_kv_norm, mla_w_ukv, hy_conv_w, hy_conv_b, hy_w1, hy_b1, hy_w2, hy_b2,
              hy_w3, hy_freq, hy_bias, w_br_gla, w_br_mla, w_br_hy, w_out, ln1_g, ln1_b, ln2_g, ln2_b,
              router_w, router_b, moe_w1, moe_b1, moe_w2, moe_b2):
    B, L, D = x.shape
    CL = ctx.shape[1]
    rope = axial_rope_angles(L)
    xc = ctx
    c_act = jax.nn.silu(c)
    cc_act = jax.nn.silu(c_ctx)
    for l in range(DEPTH):
        last = l == DEPTH - 1
        sh1, sc1, g1, sh2, sc2, g2 = jnp.split((c_act @ ada_w[l] + ada_b[l])[:, None, :], 6, axis=-1)
        csh1, csc1, cg1, csh2, csc2, cg2 = jnp.split(cc_act @ ada_w[l] + ada_b[l], 6, axis=-1)
        y, yc = token_mixer(modulate(x, sh1, sc1), modulate(xc, csh1, csc1), rope, not last,
                            w_in[l], gla_wa2_f[l], gla_ba_f[l], gla_wa2_b[l], gla_ba_b[l], gla_norm[l],
                            mla_q_norm[l], mla_w_uq[l], mla_kv_norm[l], mla_w_ukv[l],
                            hy_conv_w[l], hy_conv_b[l], hy_w1[l], hy_b1[l], hy_w2[l], hy_b2[l], hy_w3[l],
                            hy_freq[l], hy_bias[l], w_br_gla[l], w_br_mla[l], w_br_hy[l], w_out[l])
        x = layer_norm(DEEPNORM_ALPHA * x + g1 * y, ln1_g[l], ln1_b[l])
        h = modulate(x, sh2, sc2).reshape(B * L, D)
        moe_w = (router_w[l], router_b[l], moe_w1[l], moe_b1[l], moe_w2[l], moe_b2[l])
        if last:
            f = moe_ffn(h, *moe_w).reshape(B, L, D)
            x = layer_norm(DEEPNORM_ALPHA * x + g2 * f, ln2_g[l], ln2_b[l])
        else:
            xc = layer_norm(DEEPNORM_ALPHA * xc + cg1 * yc, ln1_g[l], ln1_b[l])
            hc = modulate(xc, csh2, csc2).reshape(B * CL, D)
            f = moe_ffn(jnp.concatenate([h, hc], 0), *moe_w)
            x = layer_norm(DEEPNORM_ALPHA * x + g2 * f[:B * L].reshape(B, L, D), ln2_g[l], ln2_b[l])
            xc = layer_norm(DEEPNORM_ALPHA * xc + cg2 * f[B * L:].reshape(B, CL, D), ln2_g[l], ln2_b[l])
    return x
```

```python
import functools
import math
import jax
import jax.numpy as jnp
from jax import lax
import numpy as np
from jax.experimental import pallas as pl
from jax.experimental.pallas import tpu as pltpu

D_MODEL = 1024
DEPTH = 2
GRID_W = 64
BRANCH_W = D_MODEL // 2
N_BRANCH = 3
GLA_HEADS = 4
GLA_DV = BRANCH_W // GLA_HEADS
GLA_DK = GLA_DV // 2
GLA_RANK = 16
GLA_TAU = 16.0
GLA_CHUNK = 16
MLA_HEADS = 8
MLA_DV = BRANCH_W // MLA_HEADS
MLA_NOPE = MLA_DV
MLA_ROPE = MLA_NOPE // 2
MLA_Q_RANK = 3 * D_MODEL // 8
MLA_KV_RANK = D_MODEL // 4
ROPE_BASE = 10000.0
Q_BLOCK = 128
HY_W = BRANCH_W
HY_ORDER = 2
HY_DIRS = 2
HY_BANDS = 16
HY_EMB = 2 * HY_BANDS + 1
HY_FFN = 64
HY_DECAY_TARGET = 1e-2
HY_FAST_PCT = 0.3
HY_SLOW_PCT = 1.5
N_EXPERTS = 32
TOP_K = 4
D_FF = D_MODEL
SWIGLU_LIMIT = 7.0
SWIGLU_ALPHA = 1.702
MOE_BLOCK = 128
LN_EPS = 1e-5
RMS_EPS = 1e-6
DEEPNORM_ALPHA = (2 * DEPTH) ** 0.25
IN_SPLITS = (
    GLA_HEADS * GLA_DK, GLA_HEADS * GLA_DV, GLA_RANK, GLA_RANK, MLA_KV_RANK, MLA_ROPE,
    GLA_HEADS * GLA_DK, GLA_HEADS * GLA_DV, MLA_Q_RANK, 3 * HY_W, N_BRANCH * D_MODEL,
)
N_KEY_GROUPS = 6
KEY_COLS = sum(IN_SPLITS[:N_KEY_GROUPS])
IN_OFFSETS = tuple(int(o) for o in np.cumsum(IN_SPLITS)[:-1])
IN_TOTAL = sum(IN_SPLITS)
F32 = jnp.float32
BF16 = jnp.bfloat16


def _mm_kernel(a_ref, w_ref, o_ref):
    o_ref[...] = jnp.dot(a_ref[...], w_ref[...], preferred_element_type=F32)


def _pick(n, cands):
    for c in cands:
        if n % c == 0:
            return c
    return n


def pmm(a, w):
    lead = a.shape[:-1]
    K = a.shape[-1]
    N = w.shape[-1]
    a2 = a.reshape(-1, K).astype(BF16)
    M = a2.shape[0]
    N0 = N
    if N % 128:
        N = -(-N // 256) * 256
        w = jnp.pad(w, ((0, 0), (0, N - N0)))
    tm = _pick(M, (512, 256, 128, 64, 32, 16, 8))
    tn = _pick(N, (512, 256, 128))
    out = pl.pallas_call(
        _mm_kernel,
        grid=(M // tm, N // tn),
        in_specs=[pl.BlockSpec((tm, K), lambda i, j: (i, 0)),
                  pl.BlockSpec((K, tn), lambda i, j: (0, j))],
        out_specs=pl.BlockSpec((tm, tn), lambda i, j: (i, j)),
        out_shape=jax.ShapeDtypeStruct((M, N), F32),
    )(a2, w.astype(BF16))
    return out[:, :N0].reshape(lead + (N0,))


def layer_norm(x, gain=None, bias=None):
    xf = x.astype(F32)
    xc = xf - xf.mean(-1, keepdims=True)
    y = xc * lax.rsqrt((xc * xc).mean(-1, keepdims=True) + LN_EPS)
    if gain is not None:
        y = y * gain.astype(F32) + bias.astype(F32)
    return y.astype(x.dtype)


def rms_norm(x, gain):
    xf = x.astype(F32)
    y = xf * lax.rsqrt(jnp.mean(xf * xf, -1, keepdims=True) + RMS_EPS) * gain.astype(F32)
    return y.astype(x.dtype)


def modulate(x, shift, scale):
    return layer_norm(x) * (1 + scale) + shift


def to_heads(t, n):
    B, L, _ = t.shape
    return t.reshape(B, L, n, -1).transpose(0, 2, 1, 3)


def from_heads(t):
    B, H, L, d = t.shape
    return t.transpose(0, 2, 1, 3).reshape(B, L, H * d)


def _rev(t):
    return None if t is None else jnp.flip(t, axis=2)


def axial_rope_angles(L):
    rows = L // GRID_W
    row = jnp.repeat(jnp.arange(rows, dtype=F32), GRID_W)
    col = jnp.tile(jnp.arange(GRID_W, dtype=F32), rows)
    n_freq = MLA_ROPE // 4
    inv = ROPE_BASE ** (-jnp.arange(n_freq, dtype=F32) / n_freq)
    ang = jnp.concatenate([row[:, None] * inv, col[:, None] * inv], -1)
    return jnp.cos(ang), jnp.sin(ang)


def apply_rope(x, cos, sin):
    half = x.shape[-1] // 2
    x1, x2 = x[..., :half].astype(F32), x[..., half:].astype(F32)
    return jnp.concatenate([x1 * cos - x2 * sin, x2 * cos + x1 * sin], -1).astype(x.dtype)


def attend_blocks(q, k, v):
    B, H, Lq, dq = q.shape
    nb = Lq // Q_BLOCK
    scale = dq ** -0.5
    qb = q.reshape(B, H, nb, Q_BLOCK, dq).transpose(2, 0, 1, 3, 4)

    def one_block(qi):
        s = jnp.einsum('bhqd,bhkd->bhqk', qi, k).astype(F32) * scale
        p = jax.nn.softmax(s, axis=-1)
        return jnp.einsum('bhqk,bhkd->bhqd', p.astype(v.dtype), v)

    o = lax.map(one_block, qb)
    return o.transpose(1, 2, 0, 3, 4).reshape(B, H, Lq, v.shape[-1])


def _chunk(t):
    B, H, L, d = t.shape
    return t.reshape(B, H, L // GLA_CHUNK, GLA_CHUNK, d)


def gla_states(k, v, b, s0):
    b_last = b[:, :, :, -1:, :]
    delta = jnp.einsum('bhncd,bhnce->bhnde', k * jnp.exp(b_last - b), v)
    dec = jnp.exp(b_last[:, :, :, 0, :])

    def step(s, inp):
        d_n, delta_n = inp
        return d_n[..., None] * s + delta_n, s

    s_final, s_prev = lax.scan(step, s0, (jnp.moveaxis(dec, 2, 0), jnp.moveaxis(delta, 2, 0)))
    return jnp.moveaxis(s_prev, 0, 2), s_final


def gla_outputs(q, k, v, b, s_prev):
    C = q.shape[3]
    lower = jnp.tril(jnp.ones((C, C), bool))
    diff = b[:, :, :, :, None, :] - b[:, :, :, None, :, :]
    decay = jnp.exp(jnp.where(lower[:, :, None], diff, -jnp.inf))
    scores = jnp.einsum('bhnid,bhnjd,bhnijd->bhnij', q, k, decay)
    o = (jnp.einsum('bhnij,bhnje->bhnie', scores, v)
         + jnp.einsum('bhnid,bhnde->bhnie', q * jnp.exp(b), s_prev))
    B, H, N, _, dv = o.shape
    return o.reshape(B, H, N * C, dv)


def gla_direction(q, k, v, g, qc, kc, vc, gc):
    B, H, _, dk = k.shape
    s0 = jnp.zeros((B, H, dk, v.shape[-1]), F32)
    kc_, vc_ = _chunk(kc), _chunk(vc)
    bc = jnp.cumsum(_chunk(gc), axis=3)
    sc_prev, sc_final = gla_states(kc_, vc_, bc, s0)
    oc = None if qc is None else gla_outputs(_chunk(qc), kc_, vc_, bc, sc_prev)
    k_, v_ = _chunk(k), _chunk(v)
    b = jnp.cumsum(_chunk(g), axis=3)
    s_prev, _ = gla_states(k_, v_, b, sc_final)
    return gla_outputs(_chunk(q), k_, v_, b, s_prev), oc


def gla_keys(pk, pv, paf, pab, wa2_f, ba_f, wa2_b, ba_b):
    k = to_heads(pk, GLA_HEADS)
    v = to_heads(pv, GLA_HEADS)
    g_f = to_heads(jax.nn.log_sigmoid((paf @ wa2_f + ba_f).astype(F32)) / GLA_TAU, GLA_HEADS)
    g_b = to_heads(jax.nn.log_sigmoid((pab @ wa2_b + ba_b).astype(F32)) / GLA_TAU, GLA_HEADS)
    return k, v, g_f, g_b


def gla_query(pq):
    return to_heads(pq, GLA_HEADS) * GLA_DK ** -0.5


def gla_finish(o, r, norm_w):
    return from_heads(rms_norm(o, norm_w)).astype(r.dtype) * jax.nn.silu(r)


def mla_keys(pkva, pkr, kv_norm, w_ukv, rope):
    kv = to_heads(pmm(rms_norm(pkva, kv_norm), w_ukv), MLA_HEADS)
    k_nope, v = kv[..., :MLA_NOPE], kv[..., MLA_NOPE:]
    k_rope = pkr[:, None]
    if rope is not None:
        k_rope = apply_rope(k_rope, *rope)
    k_rope = jnp.broadcast_to(k_rope, k_nope.shape[:-1] + (MLA_ROPE,))
    return jnp.concatenate([k_nope, k_rope], -1), v


def mla_query(pqa, q_norm, w_uq, rope):
    q = to_heads(pmm(rms_norm(pqa, q_norm), w_uq), MLA_HEADS)
    if rope is not None:
        q = jnp.concatenate([q[..., :MLA_NOPE], apply_rope(q[..., MLA_NOPE:], *rope)], -1)
    return q


def short_conv3(x, w, b):
    xp = jnp.pad(x, ((0, 0), (1, 1), (0, 0)))
    return xp[:, :-2] * w[0] + xp[:, 1:-1] * w[1] + xp[:, 2:] * w[2] + b


def hyena_filters(L, w1, b1, w2, b2, w3, freq):
    t = jnp.linspace(0.0, 1.0, L, dtype=F32)[:, None]
    w = 2 * math.pi * jnp.arange(L, dtype=F32)[:, None] / L
    f = jnp.linspace(1e-4, HY_BANDS - 1, HY_BANDS, dtype=F32)
    z = jnp.concatenate([t, jnp.cos(f * w), -jnp.sin(f * w)], -1)
    fr = freq.astype(F32)
    h = jnp.sin(fr * (z @ w1.astype(F32) + b1.astype(F32)))
    h = jnp.sin(fr * (h @ w2.astype(F32) + b2.astype(F32)))
    h = (h @ w3.astype(F32)).reshape(L, HY_ORDER, HY_DIRS, HY_W)
    deltas = jnp.abs(jnp.linspace(math.log(HY_DECAY_TARGET) / HY_SLOW_PCT,
                                  math.log(HY_DECAY_TARGET) / HY_FAST_PCT, HY_W, dtype=F32))
    h = h * jnp.exp(-t * deltas)[:, None, None, :]
    return h / jnp.sum(jnp.abs(h), axis=(0, 2), keepdims=True)


def two_sided_fftconv(z, h_fwd, h_bwd, bias):
    B, L, C = z.shape
    k = jnp.concatenate([h_fwd, jnp.zeros((1, C), F32), jnp.flip(h_bwd[1:], 0)], 0)
    zf = jnp.fft.rfft(z.astype(F32), n=2 * L, axis=1)
    kf = jnp.fft.rfft(k, n=2 * L, axis=0)
    y = jnp.fft.irfft(zf * kf[None], n=2 * L, axis=1)[:, :L]
    return (y + z.astype(F32) * bias.astype(F32)).astype(z.dtype)


def hyena_branch(p, conv_w, conv_b, w1, b1, w2, b2, w3, freq, hbias):
    v, x1, x2 = jnp.split(short_conv3(p, conv_w, conv_b), 3, axis=-1)
    filt = hyena_filters(p.shape[1], w1, b1, w2, b2, w3, freq)
    z = x1 * two_sided_fftconv(v, filt[:, 0, 0], filt[:, 0, 1], hbias[0])
    return x2 * two_sided_fftconv(z, filt[:, 1, 0], filt[:, 1, 1], hbias[1])


def merge_branches(o_gla, o_mla, o_hy, gates, w_br_gla, w_br_mla, w_br_hy, w_out):
    g1, g2, g3 = jnp.split(jax.nn.sigmoid(gates.astype(F32)).astype(o_gla.dtype), 3, axis=-1)
    y = g1 * pmm(o_gla, w_br_gla) + g2 * pmm(o_mla, w_br_mla) + g3 * pmm(o_hy, w_br_hy)
    return pmm(y, w_out)


def token_mixer(h, hc, rope, with_ctx_out, w_in, gla_wa2_f, gla_ba_f, gla_wa2_b, gla_ba_b, gla_norm,
                mla_q_norm, mla_w_uq, mla_kv_norm, mla_w_ukv,
                hy_conv_w, hy_conv_b, hy_w1, hy_b1, hy_w2, hy_b2, hy_w3, hy_freq, hy_bias,
                w_br_gla, w_br_mla, w_br_hy, w_out):
    gk, gv, gaf, gab, mkva, mkr, gq, gr, mqa, hy, gates = jnp.split(pmm(h, w_in), IN_OFFSETS, axis=-1)
    if with_ctx_out:
        ck, cv, caf, cab, ckva, ckr, cq, cr, cqa, chy, cgates = jnp.split(pmm(hc, w_in), IN_OFFSETS, axis=-1)
    else:
        ck, cv, caf, cab, ckva, ckr = jnp.split(pmm(hc, w_in[:, :KEY_COLS]), IN_OFFSETS[:N_KEY_GROUPS - 1], axis=-1)
    gate_w = (gla_wa2_f, gla_ba_f, gla_wa2_b, gla_ba_b)
    k, v, g_f, g_b = gla_keys(gk, gv, gaf, gab, *gate_w)
    kc, vc, gc_f, gc_b = gla_keys(ck, cv, caf, cab, *gate_w)
    q = gla_query(gq)
    qc = gla_query(cq) if with_ctx_out else None
    o_f, oc_f = gla_direction(q, k, v, g_f, qc, kc, vc, gc_f)
    o_b, oc_b = gla_direction(_rev(q), _rev(k), _rev(v), _rev(g_b), _rev(qc), _rev(kc), _rev(vc), _rev(gc_b))
    o_gla = gla_finish(o_f + _rev(o_b), gr, gla_norm)
    mk, mv = mla_keys(mkva, mkr, mla_kv_norm, mla_w_ukv, rope)
    mkc, mvc = mla_keys(ckva, ckr, mla_kv_norm, mla_w_ukv, None)
    mq = mla_query(mqa, mla_q_norm, mla_w_uq, rope)
    o_mla = from_heads(attend_blocks(mq, jnp.concatenate([mkc, mk], 2), jnp.concatenate([mvc, mv], 2)))
    hy_w = (hy_conv_w, hy_conv_b, hy_w1, hy_b1, hy_w2, hy_b2, hy_w3, hy_freq, hy_bias)
    o_hy = hyena_branch(hy, *hy_w)
    br_w = (w_br_gla, w_br_mla, w_br_hy, w_out)
    y = merge_branches(o_gla, o_mla, o_hy, gates, *br_w)
    if not with_ctx_out:
        return y, None
    oc_gla = gla_finish(oc_f + _rev(oc_b), cr, gla_norm)
    oc_mla = from_heads(attend_blocks(mla_query(cqa, mla_q_norm, mla_w_uq, None), mkc, mvc))
    oc_hy = hyena_branch(chy, *hy_w)
    yc = merge_branches(oc_gla, oc_mla, oc_hy, cgates, *br_w)
    return y, yc


def moe_ffn(h, router_w, router_b, w1, b1, w2, b2):
    N, D = h.shape
    logits = (h @ router_w + router_b).astype(F32)
    top_val, top_idx = lax.top_k(logits, TOP_K)
    gates = jax.nn.softmax(top_val, axis=-1)
    NK = N * TOP_K
    flat_e = top_idx.reshape(NK)
    order = jnp.argsort(flat_e)
    sorted_e = flat_e[order]
    sorted_tok = (order // TOP_K).astype(jnp.int32)
    counts = jnp.bincount(flat_e, length=N_EXPERTS)
    padded = (counts + MOE_BLOCK - 1) // MOE_BLOCK * MOE_BLOCK
    pad_end = jnp.cumsum(padded)
    pad_start = pad_end - padded
    start = jnp.cumsum(counts) - counts
    dest = pad_start[sorted_e] + (jnp.arange(NK) - start[sorted_e])
    n_blocks = -(-NK // MOE_BLOCK) + N_EXPERTS
    buf_tok = jnp.zeros((n_blocks * MOE_BLOCK,), jnp.int32).at[dest].set(sorted_tok)
    block_e = jnp.minimum(jnp.searchsorted(pad_end, jnp.arange(n_blocks) * MOE_BLOCK, side='right'), N_EXPERTS - 1)
    xb = h[buf_tok].reshape(n_blocks, MOE_BLOCK, D)

    def expert_block(args):
        xi, e = args
        glu, lin = jnp.split(xi @ w1[e] + b1[e], 2, axis=-1)
        glu = jnp.minimum(glu, SWIGLU_LIMIT)
        lin = jnp.clip(lin, -SWIGLU_LIMIT, SWIGLU_LIMIT)
        act = glu * jax.nn.sigmoid(SWIGLU_ALPHA * glu) * (lin + 1)
        return act @ w2[e] + b2[e]

    yb = lax.map(expert_block, (xb, block_e)).reshape(-1, D)
    y_assign = yb[dest] * gates.reshape(NK)[order][:, None].astype(yb.dtype)
    return jax.ops.segment_sum(y_assign, sorted_tok, num_segments=N)


def kernel(x, c, ctx, c_ctx, ada_w, ada_b, w_in, gla_wa2_f, gla_ba_f, gla_wa2_b, gla_ba_b, gla_norm,
           mla_q_norm, mla_w_uq, mla_kv_norm, mla_w_ukv, hy_conv_w, hy_conv_b, hy_w1, hy_b1, hy_w2, hy_b2,
           hy_w3, hy_freq, hy_bias, w_br_gla, w_br_mla, w_br_hy, w_out, ln1_g, ln1_b, ln2_g, ln2_b,
           router_w, router_b, moe_w1, moe_b1, moe_w2, moe_b2):
    B, L, D = x.shape
    CL = ctx.shape[1]
    rope = axial_rope_angles(L)
    xc = ctx
    c_act = jax.nn.silu(c)
    cc_act = jax.nn.silu(c_ctx)
    for l in range(DEPTH):
        last = l == DEPTH - 1
        sh1, sc1, g1, sh2, sc2, g2 = jnp.split((c_act @ ada_w[l] + ada_b[l])[:, None, :], 6, axis=-1)
        csh1, csc1, cg1, csh2, csc2, cg2 = jnp.split(cc_act @ ada_w[l] + ada_b[l], 6, axis=-1)
        y, yc = token_mixer(modulate(x, sh1, sc1), modulate(xc, csh1, csc1), rope, not last,
                            w_in[l], gla_wa2_f[l], gla_ba_f[l], gla_wa2_b[l], gla_ba_b[l], gla_norm[l],
                            mla_q_norm[l], mla_w_uq[l], mla_kv_norm[l], mla_w_ukv[l],
                            hy_conv_w[l], hy_conv_b[l], hy_w1[l], hy_b1[l], hy_w2[l], hy_b2[l], hy_w3[l],
                            hy_freq[l], hy_bias[l], w_br_gla[l], w_br_mla[l], w_br_hy[l], w_out[l])
        x = layer_norm(DEEPNORM_ALPHA * x + g1 * y, ln1_g[l], ln1_b[l])
        h = modulate(x, sh2, sc2).reshape(B * L, D)
        moe_w = (router_w[l], router_b[l], moe_w1[l], moe_b1[l], moe_w2[l], moe_b2[l])
        if last:
            f = moe_ffn(h, *moe_w).reshape(B, L, D)
            x = layer_norm(DEEPNORM_ALPHA * x + g2 * f, ln2_g[l], ln2_b[l])
        else:
            xc = layer_norm(DEEPNORM_ALPHA * xc + cg1 * yc, ln1_g[l], ln1_b[l])
            hc = modulate(xc, csh2, csc2).reshape(B * CL, D)
            f = moe_ffn(jnp.concatenate([h, hc], 0), *moe_w)
            x = layer_norm(DEEPNORM_ALPHA * x + g2 * f[:B * L].reshape(B, L, D), ln2_g[l], ln2_b[l])
            xc = layer_norm(DEEPNORM_ALPHA * xc + cg2 * f[B * L:].reshape(B, CL, D), ln2_g[l], ln2_b[l])
    return x
```

```python
import functools
import math
import jax
import jax.numpy as jnp
from jax import lax
import numpy as np
from jax.experimental import pallas as pl
from jax.experimental.pallas import tpu as pltpu

D_MODEL = 1024
DEPTH = 2
GRID_W = 64
BRANCH_W = D_MODEL // 2
N_BRANCH = 3
GLA_HEADS = 4
GLA_DV = BRANCH_W // GLA_HEADS
GLA_DK = GLA_DV // 2
GLA_RANK = 16
GLA_TAU = 16.0
GLA_CHUNK = 16
MLA_HEADS = 8
MLA_DV = BRANCH_W // MLA_HEADS
MLA_NOPE = MLA_DV
MLA_ROPE = MLA_NOPE // 2
MLA_Q_RANK = 3 * D_MODEL // 8
MLA_KV_RANK = D_MODEL // 4
ROPE_BASE = 10000.0
Q_BLOCK = 128
HY_W = BRANCH_W
HY_ORDER = 2
HY_DIRS = 2
HY_BANDS = 16
HY_EMB = 2 * HY_BANDS + 1
HY_FFN = 64
HY_DECAY_TARGET = 1e-2
HY_FAST_PCT = 0.3
HY_SLOW_PCT = 1.5
N_EXPERTS = 32
TOP_K = 4
D_FF = D_MODEL
SWIGLU_LIMIT = 7.0
SWIGLU_ALPHA = 1.702
MOE_BLOCK = 128
LN_EPS = 1e-5
RMS_EPS = 1e-6
DEEPNORM_ALPHA = (2 * DEPTH) ** 0.25
IN_SPLITS = (
    GLA_HEADS * GLA_DK, GLA_HEADS * GLA_DV, GLA_RANK, GLA_RANK, MLA_KV_RANK, MLA_ROPE,
    GLA_HEADS * GLA_DK, GLA_HEADS * GLA_DV, MLA_Q_RANK, 3 * HY_W, N_BRANCH * D_MODEL,
)
N_KEY_GROUPS = 6
KEY_COLS = sum(IN_SPLITS[:N_KEY_GROUPS])
IN_OFFSETS = tuple(int(o) for o in np.cumsum(IN_SPLITS)[:-1])
IN_TOTAL = sum(IN_SPLITS)
F32 = jnp.float32
BF16 = jnp.bfloat16


def _mm_kernel(a_ref, w_ref, o_ref):
    o_ref[...] = jnp.dot(a_ref[...], w_ref[...], preferred_element_type=F32)


def _pick(n, cands):
    for c in cands:
        if n % c == 0:
            return c
    return n


def pmm(a, w):
    lead = a.shape[:-1]
    K = a.shape[-1]
    N = w.shape[-1]
    a2 = a.reshape(-1, K).astype(BF16)
    M = a2.shape[0]
    N0 = N
    if N % 128:
        N = -(-N // 256) * 256
        w = jnp.pad(w, ((0, 0), (0, N - N0)))
    tm = _pick(M, (512, 256, 128, 64, 32, 16, 8))
    tn = _pick(N, (512, 256, 128))
    out = pl.pallas_call(
        _mm_kernel,
        grid=(M // tm, N // tn),
        in_specs=[pl.BlockSpec((tm, K), lambda i, j: (i, 0)),
                  pl.BlockSpec((K, tn), lambda i, j: (0, j))],
        out_specs=pl.BlockSpec((tm, tn), lambda i, j: (i, j)),
        out_shape=jax.ShapeDtypeStruct((M, N), F32),
    )(a2, w.astype(BF16))
    return out[:, :N0].reshape(lead + (N0,))


HEAD_PAD = 128
ATT_TQ = 256
ATT_TK = 768
ONES_LANE_EVEN = MLA_DV
ONES_LANE_ODD = 0
LOG2E = 1.4426950408889634


def _attn_kernel(q_ref, k_ref, v_ref, o_ref, s_buf, p_buf, m_buf, a_buf, mrun, acc, *, n_ctx, tk):
    tq = q_ref.shape[1]
    T = k_ref.shape[1]
    in_ctx = pl.program_id(2) * tq < n_ctx
    lane = lax.broadcasted_iota(jnp.int32, (tq, HEAD_PAD), 1)
    hs = lambda h: slice(h * HEAD_PAD, (h + 1) * HEAD_PAD)

    def stage_a(c, slot, size):
        start = pl.multiple_of(c * size, size)
        for h in range(2):
            k = k_ref[0, pl.ds(start, size), hs(h)]
            s = lax.dot_general(q_ref[0, :, hs(h)], k, (((1,), (1,)), ((), ())), preferred_element_type=F32)
            s_buf[slot, h, :, :size] = s
            m_prev = mrun[h]
            m_new = jnp.maximum(m_prev, jnp.max(s, axis=-1, keepdims=True))
            mrun[h] = m_new
            m_buf[slot, h] = m_new
            a_buf[slot, h] = jnp.exp2(m_prev - m_new)

    def stage_b(slot, size):
        for h in range(2):
            p_buf[slot, h, :, :size] = jnp.exp2(s_buf[slot, h, :, :size] - m_buf[slot, h]).astype(BF16)

    def stage_c(c, slot, size):
        start = pl.multiple_of(c * size, size)
        for h in range(2):
            v = v_ref[0, pl.ds(start, size), hs(h)]
            acc[h] = a_buf[slot, h] * acc[h] + jnp.dot(p_buf[slot, h, :, :size], v, preferred_element_type=F32)

    def run(n, size):
        for h in range(2):
            mrun[h] = jnp.full((tq, 1), -jnp.inf, F32)
            acc[h] = jnp.zeros((tq, HEAD_PAD), F32)
        if n == 1:
            stage_a(0, 0, size)
            stage_b(0, size)
            stage_c(0, 0, size)
        else:
            stage_a(0, 0, size)
            stage_b(0, size)
            stage_a(1, 1, size)

            def step(c, par):
                stage_c(c, par, size)
                stage_b(1 - par, size)
                stage_a(c + 2, par, size)

            def body(j, carry):
                step(2 * j, 0)
                step(2 * j + 1, 1)
                return carry

            lax.fori_loop(0, (n - 2) // 2, body, 0)
            if (n - 2) % 2:
                step(n - 3, (n - 3) % 2)
            stage_c(n - 2, (n - 2) % 2, size)
            stage_b((n - 1) % 2, size)
            stage_c(n - 1, (n - 1) % 2, size)
        a0, a1 = acc[0], acc[1]
        l0 = a0[:, ONES_LANE_EVEN:ONES_LANE_EVEN + 1]
        l1 = a1[:, ONES_LANE_ODD:ONES_LANE_ODD + 1]
        o_ref[0] = jnp.where(lane < MLA_DV, a0 / l0, a1 / l1).astype(o_ref.dtype)

    @pl.when(in_ctx)
    def _():
        run(1, n_ctx)

    @pl.when(jnp.logical_not(in_ctx))
    def _():
        run(T // tk, tk)


def mla_attention(q, k, v, n_ctx):
    B, T, _ = q.shape
    assert T % ATT_TQ == 0 and T % ATT_TK == 0 and n_ctx % ATT_TQ == 0 and n_ctx <= ATT_TK
    kern = functools.partial(_attn_kernel, n_ctx=n_ctx, tk=ATT_TK)
    return pl.pallas_call(
        kern,
        grid=(B, MLA_HEADS // 2, T // ATT_TQ),
        in_specs=[pl.BlockSpec((1, ATT_TQ, 2 * HEAD_PAD), lambda b, h, i: (b, i, h)),
                  pl.BlockSpec((1, T, 2 * HEAD_PAD), lambda b, h, i: (b, 0, h)),
                  pl.BlockSpec((1, T, 2 * HEAD_PAD), lambda b, h, i: (b, 0, h))],
        out_specs=pl.BlockSpec((1, ATT_TQ, 2 * MLA_DV), lambda b, h, i: (b, i, h)),
        out_shape=jax.ShapeDtypeStruct((B, T, MLA_HEADS * MLA_DV), BF16),
        scratch_shapes=[pltpu.VMEM((2, 2, ATT_TQ, ATT_TK), F32), pltpu.VMEM((2, 2, ATT_TQ, ATT_TK), BF16),
                        pltpu.VMEM((2, 2, ATT_TQ, 1), F32), pltpu.VMEM((2, 2, ATT_TQ, 1), F32),
                        pltpu.VMEM((2, ATT_TQ, 1), F32), pltpu.VMEM((2, ATT_TQ, HEAD_PAD), F32)],
        compiler_params=pltpu.CompilerParams(
            dimension_semantics=("arbitrary", "arbitrary", "arbitrary"),
            vmem_limit_bytes=56 * 1024 * 1024),
        name="mla_attention",
    )(q, k, v)


def interleave_v(v):
    B, T, H, dv = v.shape
    v = v.reshape(B, T, H // 2, 2, dv)
    z = jnp.zeros_like(v[:, :, :, 0])
    one_e = z.at[..., ONES_LANE_EVEN - dv].set(1.0)
    one_o = z.at[..., ONES_LANE_ODD].set(1.0)
    v_even = jnp.concatenate([v[:, :, :, 0], one_e], -1)
    v_odd = jnp.concatenate([one_o, v[:, :, :, 1]], -1)
    return jnp.stack([v_even, v_odd], 3).reshape(B, T, H * HEAD_PAD).astype(BF16)


MOE_TM = 256


def _moe_kernel(be_ref, first_ref, nb_ref, x_ref, g_ref, w1_ref, b1_ref, w2_ref, b2_ref, o_ref,
                w1_bf, w2_bf):
    i = pl.program_id(0)

    @pl.when(jnp.logical_and(i < nb_ref[0], first_ref[i] == 1))
    def _():
        w1_bf[...] = w1_ref[0].astype(BF16)
        w2_bf[...] = w2_ref[0].astype(BF16)

    @pl.when(i < nb_ref[0])
    def _():
        hcat = jnp.dot(x_ref[...], w1_bf[...], preferred_element_type=F32) + b1_ref[0]
        glu = jnp.minimum(hcat[:, :D_FF], SWIGLU_LIMIT)
        lin = jnp.clip(hcat[:, D_FF:], -SWIGLU_LIMIT, SWIGLU_LIMIT)
        act = glu * jax.nn.sigmoid(SWIGLU_ALPHA * glu) * (lin + 1)
        y = jnp.dot(act.astype(BF16), w2_bf[...], preferred_element_type=F32) + b2_ref[0]
        o_ref[...] = (y * g_ref[...]).astype(o_ref.dtype)

    @pl.when(i >= nb_ref[0])
    def _():
        o_ref[...] = jnp.zeros_like(o_ref)


def moe_experts(xb, gate_b, block_e, first, n_used, w1, b1, w2, b2):
    R, D = xb.shape
    n_blocks = R // MOE_TM
    grid_spec = pltpu.PrefetchScalarGridSpec(
        num_scalar_prefetch=3,
        grid=(n_blocks,),
        in_specs=[pl.BlockSpec((MOE_TM, D), lambda i, be, fi, nb: (i, 0)),
                  pl.BlockSpec((MOE_TM, 1), lambda i, be, fi, nb: (i, 0)),
                  pl.BlockSpec((1, D, 2 * D_FF), lambda i, be, fi, nb: (be[i], 0, 0)),
                  pl.BlockSpec((1, 1, 2 * D_FF), lambda i, be, fi, nb: (be[i], 0, 0)),
                  pl.BlockSpec((1, D_FF, D), lambda i, be, fi, nb: (be[i], 0, 0)),
                  pl.BlockSpec((1, 1, D), lambda i, be, fi, nb: (be[i], 0, 0))],
        out_specs=pl.BlockSpec((MOE_TM, D), lambda i, be, fi, nb: (i, 0)),
        scratch_shapes=[pltpu.VMEM((D, 2 * D_FF), BF16), pltpu.VMEM((D_FF, D), BF16)],
    )
    return pl.pallas_call(
        _moe_kernel,
        grid_spec=grid_spec,
        out_shape=jax.ShapeDtypeStruct((R, D), F32),
        compiler_params=pltpu.CompilerParams(
            dimension_semantics=("arbitrary",), vmem_limit_bytes=56 * 1024 * 1024),
        name="moe_experts",
    )(block_e, first, n_used, xb, gate_b, w1, b1[:, None, :], w2, b2[:, None, :])


def moe_ffn(h, router_w, router_b, w1, b1, w2, b2):
    N, D = h.shape
    hb = h.astype(BF16)
    logits = jnp.dot(hb, router_w.astype(BF16), preferred_element_type=F32) + router_b
    top_val, top_idx = lax.top_k(logits, TOP_K)
    gates = jax.nn.softmax(top_val, axis=-1)
    NK = N * TOP_K
    flat_e = top_idx.reshape(NK)
    onehot = (flat_e[:, None] == jnp.arange(N_EXPERTS, dtype=flat_e.dtype)[None, :]).astype(jnp.int32)
    csum = jnp.cumsum(onehot, axis=0)
    rank = jnp.take_along_axis(csum, flat_e[:, None], axis=1)[:, 0] - 1
    counts = csum[-1]
    padded = (counts + MOE_TM - 1) // MOE_TM * MOE_TM
    pad_end = jnp.cumsum(padded)
    pad_start = pad_end - padded
    dest = (pad_start[flat_e] + rank).astype(jnp.int32)
    n_blocks = NK // MOE_TM + N_EXPERTS
    R = n_blocks * MOE_TM
    tok = (jnp.arange(NK, dtype=jnp.int32) // TOP_K)
    buf_tok = jnp.zeros((R,), jnp.int32).at[dest].set(tok)
    buf_gate = jnp.zeros((R,), F32).at[dest].set(gates.reshape(NK))
    blk_start = jnp.arange(n_blocks, dtype=jnp.int32) * MOE_TM
    block_e = jnp.minimum(jnp.searchsorted(pad_end, blk_start, side='right'), N_EXPERTS - 1).astype(jnp.int32)
    first = jnp.concatenate([jnp.ones((1,), jnp.int32), (block_e[1:] != block_e[:-1]).astype(jnp.int32)])
    n_used = (pad_end[-1:] // MOE_TM).astype(jnp.int32)
    yb = moe_experts(hb[buf_tok], buf_gate[:, None], block_e, first, n_used, w1, b1, w2, b2)
    return yb[dest.reshape(N, TOP_K)].sum(axis=1)


def layer_norm(x, gain=None, bias=None):
    xf = x.astype(F32)
    xc = xf - xf.mean(-1, keepdims=True)
    y = xc * lax.rsqrt((xc * xc).mean(-1, keepdims=True) + LN_EPS)
    if gain is not None:
        y = y * gain.astype(F32) + bias.astype(F32)
    return y.astype(x.dtype)


def rms_norm(x, gain):
    xf = x.astype(F32)
    y = xf * lax.rsqrt(jnp.mean(xf * xf, -1, keepdims=True) + RMS_EPS) * gain.astype(F32)
    return y.astype(x.dtype)


def modulate(x, shift, scale):
    return layer_norm(x) * (1 + scale) + shift


def to_heads(t, n):
    B, L, _ = t.shape
    return t.reshape(B, L, n, -1).transpose(0, 2, 1, 3)


def from_heads(t):
    B, H, L, d = t.shape
    return t.transpose(0, 2, 1, 3).reshape(B, L, H * d)


def _rev(t):
    return None if t is None else jnp.flip(t, axis=2)


def axial_rope_angles(L):
    rows = L // GRID_W
    row = jnp.repeat(jnp.arange(rows, dtype=F32), GRID_W)
    col = jnp.tile(jnp.arange(GRID_W, dtype=F32), rows)
    n_freq = MLA_ROPE // 4
    inv = ROPE_BASE ** (-jnp.arange(n_freq, dtype=F32) / n_freq)
    ang = jnp.concatenate([row[:, None] * inv, col[:, None] * inv], -1)
    return jnp.cos(ang), jnp.sin(ang)


def apply_rope(x, cos, sin):
    half = x.shape[-1] // 2
    x1, x2 = x[..., :half].astype(F32), x[..., half:].astype(F32)
    return jnp.concatenate([x1 * cos - x2 * sin, x2 * cos + x1 * sin], -1).astype(x.dtype)


def _chunk(t):
    B, H, L, d = t.shape
    return t.reshape(B, H, L // GLA_CHUNK, GLA_CHUNK, d)


def gla_states(k, v, b, s0):
    b_last = b[:, :, :, -1:, :]
    delta = jnp.einsum('bhncd,bhnce->bhnde', k * jnp.exp(b_last - b), v)
    dec = jnp.exp(b_last[:, :, :, 0, :])

    def step(s, inp):
        d_n, delta_n = inp
        return d_n[..., None] * s + delta_n, s

    s_final, s_prev = lax.scan(step, s0, (jnp.moveaxis(dec, 2, 0), jnp.moveaxis(delta, 2, 0)))
    return jnp.moveaxis(s_prev, 0, 2), s_final


def gla_outputs(q, k, v, b, s_prev):
    C = q.shape[3]
    lower = jnp.tril(jnp.ones((C, C), bool))
    diff = b[:, :, :, :, None, :] - b[:, :, :, None, :, :]
    decay = jnp.exp(jnp.where(lower[:, :, None], diff, -jnp.inf))
    scores = jnp.einsum('bhnid,bhnjd,bhnijd->bhnij', q, k, decay)
    o = (jnp.einsum('bhnij,bhnje->bhnie', scores, v)
         + jnp.einsum('bhnid,bhnde->bhnie', q * jnp.exp(b), s_prev))
    B, H, N, _, dv = o.shape
    return o.reshape(B, H, N * C, dv)


def gla_direction(q, k, v, g, qc, kc, vc, gc):
    B, H, _, dk = k.shape
    s0 = jnp.zeros((B, H, dk, v.shape[-1]), F32)
    kc_, vc_ = _chunk(kc), _chunk(vc)
    bc = jnp.cumsum(_chunk(gc), axis=3)
    sc_prev, sc_final = gla_states(kc_, vc_, bc, s0)
    oc = None if qc is None else gla_outputs(_chunk(qc), kc_, vc_, bc, sc_prev)
    k_, v_ = _chunk(k), _chunk(v)
    b = jnp.cumsum(_chunk(g), axis=3)
    s_prev, _ = gla_states(k_, v_, b, sc_final)
    return gla_outputs(_chunk(q), k_, v_, b, s_prev), oc


def gla_keys(pk, pv, paf, pab, wa2_f, ba_f, wa2_b, ba_b):
    k = to_heads(pk, GLA_HEADS)
    v = to_heads(pv, GLA_HEADS)
    g_f = to_heads(jax.nn.log_sigmoid((paf @ wa2_f + ba_f).astype(F32)) / GLA_TAU, GLA_HEADS)
    g_b = to_heads(jax.nn.log_sigmoid((pab @ wa2_b + ba_b).astype(F32)) / GLA_TAU, GLA_HEADS)
    return k, v, g_f, g_b


def gla_query(pq):
    return to_heads(pq, GLA_HEADS) * GLA_DK ** -0.5


def gla_finish(o, r, norm_w):
    return from_heads(rms_norm(o, norm_w)).astype(r.dtype) * jax.nn.silu(r)


def mla_branch(pqa, pkva, pkr, q_norm, w_uq, kv_norm, w_ukv, rope, n_ctx):
    B, T, _ = pqa.shape
    cos, sin = rope
    q = pmm(rms_norm(pqa, q_norm), w_uq).reshape(B, T, MLA_HEADS, MLA_NOPE + MLA_ROPE)
    q = jnp.concatenate([q[..., :MLA_NOPE], apply_rope(q[..., MLA_NOPE:], cos[:, None], sin[:, None])], -1)
    q = q * ((MLA_NOPE + MLA_ROPE) ** -0.5 * LOG2E)
    kv = pmm(rms_norm(pkva, kv_norm), w_ukv).reshape(B, T, MLA_HEADS, MLA_NOPE + MLA_DV)
    k_rope = jnp.broadcast_to(apply_rope(pkr, cos, sin)[:, :, None, :], (B, T, MLA_HEADS, MLA_ROPE))
    k = jnp.concatenate([kv[..., :MLA_NOPE], k_rope], -1)
    pad = HEAD_PAD - MLA_NOPE - MLA_ROPE
    qp = jnp.pad(q, ((0, 0), (0, 0), (0, 0), (0, pad))).reshape(B, T, MLA_HEADS * HEAD_PAD).astype(BF16)
    kp = jnp.pad(k, ((0, 0), (0, 0), (0, 0), (0, pad))).reshape(B, T, MLA_HEADS * HEAD_PAD).astype(BF16)
    return mla_attention(qp, kp, interleave_v(kv[..., MLA_NOPE:]), n_ctx)


def short_conv3(x, w, b):
    xp = jnp.pad(x, ((0, 0), (1, 1), (0, 0)))
    return xp[:, :-2] * w[0] + xp[:, 1:-1] * w[1] + xp[:, 2:] * w[2] + b


def hyena_filters(L, w1, b1, w2, b2, w3, freq):
    t = jnp.linspace(0.0, 1.0, L, dtype=F32)[:, None]
    w = 2 * math.pi * jnp.arange(L, dtype=F32)[:, None] / L
    f = jnp.linspace(1e-4, HY_BANDS - 1, HY_BANDS, dtype=F32)
    z = jnp.concatenate([t, jnp.cos(f * w), -jnp.sin(f * w)], -1)
    fr = freq.astype(F32)
    h = jnp.sin(fr * (z @ w1.astype(F32) + b1.astype(F32)))
    h = jnp.sin(fr * (h @ w2.astype(F32) + b2.astype(F32)))
    h = (h @ w3.astype(F32)).reshape(L, HY_ORDER, HY_DIRS, HY_W)
    deltas = jnp.abs(jnp.linspace(math.log(HY_DECAY_TARGET) / HY_SLOW_PCT,
                                  math.log(HY_DECAY_TARGET) / HY_FAST_PCT, HY_W, dtype=F32))
    h = h * jnp.exp(-t * deltas)[:, None, None, :]
    return h / jnp.sum(jnp.abs(h), axis=(0, 2), keepdims=True)


def two_sided_fftconv(z, h_fwd, h_bwd, bias):
    B, L, C = z.shape
    k = jnp.concatenate([h_fwd, jnp.zeros((1, C), F32), jnp.flip(h_bwd[1:], 0)], 0)
    zf = jnp.fft.rfft(z.astype(F32), n=2 * L, axis=1)
    kf = jnp.fft.rfft(k, n=2 * L, axis=0)
    y = jnp.fft.irfft(zf * kf[None], n=2 * L, axis=1)[:, :L]
    return (y + z.astype(F32) * bias.astype(F32)).astype(z.dtype)


def hyena_branch(p, conv_w, conv_b, w1, b1, w2, b2, w3, freq, hbias):
    v, x1, x2 = jnp.split(short_conv3(p, conv_w, conv_b), 3, axis=-1)
    filt = hyena_filters(p.shape[1], w1, b1, w2, b2, w3, freq)
    z = x1 * two_sided_fftconv(v, filt[:, 0, 0], filt[:, 0, 1], hbias[0])
    return x2 * two_sided_fftconv(z, filt[:, 1, 0], filt[:, 1, 1], hbias[1])


def merge_branches(o_gla, o_mla, o_hy, gates, w_br_gla, w_br_mla, w_br_hy, w_out):
    g1, g2, g3 = jnp.split(jax.nn.sigmoid(gates.astype(F32)).astype(o_gla.dtype), 3, axis=-1)
    y = g1 * pmm(o_gla, w_br_gla) + g2 * pmm(o_mla, w_br_mla) + g3 * pmm(o_hy, w_br_hy)
    return pmm(y, w_out)


def token_mixer(h, rope, n_ctx, w_in, gla_wa2_f, gla_ba_f, gla_wa2_b, gla_ba_b, gla_norm,
                mla_q_norm, mla_w_uq, mla_kv_norm, mla_w_ukv,
                hy_conv_w, hy_conv_b, hy_w1, hy_b1, hy_w2, hy_b2, hy_w3, hy_freq, hy_bias,
                w_br_gla, w_br_mla, w_br_hy, w_out):
    pk, pv, paf, pab, mkva, mkr, pq, pr, mqa, hy, gates = jnp.split(pmm(h, w_in), IN_OFFSETS, axis=-1)
    gate_w = (gla_wa2_f, gla_ba_f, gla_wa2_b, gla_ba_b)
    ka, va, gfa, gba = gla_keys(pk, pv, paf, pab, *gate_w)
    qa = gla_query(pq)
    sl_c = lambda t: t[:, :, :n_ctx]
    sl_x = lambda t: t[:, :, n_ctx:]
    o_f, oc_f = gla_direction(sl_x(qa), sl_x(ka), sl_x(va), sl_x(gfa), sl_c(qa), sl_c(ka), sl_c(va), sl_c(gfa))
    o_b, oc_b = gla_direction(_rev(sl_x(qa)), _rev(sl_x(ka)), _rev(sl_x(va)), _rev(sl_x(gba)),
                              _rev(sl_c(qa)), _rev(sl_c(ka)), _rev(sl_c(va)), _rev(sl_c(gba)))
    o = jnp.concatenate([oc_f + _rev(oc_b), o_f + _rev(o_b)], axis=2)
    o_gla = gla_finish(o, pr, gla_norm)
    o_mla = mla_branch(mqa, mkva, mkr, mla_q_norm, mla_w_uq, mla_kv_norm, mla_w_ukv, rope, n_ctx)
    hy_w = (hy_conv_w, hy_conv_b, hy_w1, hy_b1, hy_w2, hy_b2, hy_w3, hy_freq, hy_bias)
    o_hy = jnp.concatenate([hyena_branch(hy[:, :n_ctx], *hy_w), hyena_branch(hy[:, n_ctx:], *hy_w)], axis=1)
    return merge_branches(o_gla, o_mla, o_hy, gates, w_br_gla, w_br_mla, w_br_hy, w_out)


def kernel(x, c, ctx, c_ctx, ada_w, ada_b, w_in, gla_wa2_f, gla_ba_f, gla_wa2_b, gla_ba_b, gla_norm,
           mla_q_norm, mla_w_uq, mla_kv_norm, mla_w_ukv, hy_conv_w, hy_conv_b, hy_w1, hy_b1, hy_w2, hy_b2,
           hy_w3, hy_freq, hy_bias, w_br_gla, w_br_mla, w_br_hy, w_out, ln1_g, ln1_b, ln2_g, ln2_b,
           router_w, router_b, moe_w1, moe_b1, moe_w2, moe_b2):
    B, L, D = x.shape
    CL = ctx.shape[1]
    T = CL + L
    cos, sin = axial_rope_angles(L)
    rope = (jnp.concatenate([jnp.ones((CL, cos.shape[1]), F32), cos], 0),
            jnp.concatenate([jnp.zeros((CL, sin.shape[1]), F32), sin], 0))
    xa = jnp.concatenate([ctx, x], axis=1)
    c_act = jax.nn.silu(c)
    cc_act = jax.nn.silu(c_ctx)
    for l in range(DEPTH):
        last = l == DEPTH - 1
        ada_x = (c_act @ ada_w[l] + ada_b[l])[:, None, :]
        ada_c = jnp.broadcast_to((cc_act @ ada_w[l] + ada_b[l])[None, None, :], ada_x.shape)
        is_ctx = (jnp.arange(T) < CL)[None, :, None]
        sh1, sc1, g1, sh2, sc2, g2 = [jnp.where(is_ctx, a, b) for a, b in
                                      zip(jnp.split(ada_c, 6, axis=-1), jnp.split(ada_x, 6, axis=-1))]
        y = token_mixer(modulate(xa, sh1, sc1), rope, CL,
                        w_in[l], gla_wa2_f[l], gla_ba_f[l], gla_wa2_b[l], gla_ba_b[l], gla_norm[l],
                        mla_q_norm[l], mla_w_uq[l], mla_kv_norm[l], mla_w_ukv[l],
                        hy_conv_w[l], hy_conv_b[l], hy_w1[l], hy_b1[l], hy_w2[l], hy_b2[l], hy_w3[l],
                        hy_freq[l], hy_bias[l], w_br_gla[l], w_br_mla[l], w_br_hy[l], w_out[l])
        xa = layer_norm(DEEPNORM_ALPHA * xa + g1 * y, ln1_g[l], ln1_b[l])
        h = modulate(xa, sh2, sc2)
        moe_w = (router_w[l], router_b[l], moe_w1[l], moe_b1[l], moe_w2[l], moe_b2[l])
        if last:
            f = moe_ffn(h[:, CL:].reshape(B * L, D), *moe_w).reshape(B, L, D)
            return layer_norm(DEEPNORM_ALPHA * xa[:, CL:] + g2[:, CL:] * f, ln2_g[l], ln2_b[l])
        f = moe_ffn(h.reshape(B * T, D), *moe_w).reshape(B, T, D)
        xa = layer_norm(DEEPNORM_ALPHA * xa + g2 * f, ln2_g[l], ln2_b[l])
```

```python
import functools
import math
import jax
import jax.numpy as jnp
from jax import lax
import numpy as np
from jax.experimental import pallas as pl
from jax.experimental.pallas import tpu as pltpu

D_MODEL = 1024
DEPTH = 2
GRID_W = 64
BRANCH_W = D_MODEL // 2
N_BRANCH = 3
GLA_HEADS = 4
GLA_DV = BRANCH_W // GLA_HEADS
GLA_DK = GLA_DV // 2
GLA_RANK = 16
GLA_TAU = 16.0
GLA_CHUNK = 16
MLA_HEADS = 8
MLA_DV = BRANCH_W // MLA_HEADS
MLA_NOPE = MLA_DV
MLA_ROPE = MLA_NOPE // 2
MLA_Q_RANK = 3 * D_MODEL // 8
MLA_KV_RANK = D_MODEL // 4
ROPE_BASE = 10000.0
Q_BLOCK = 128
HY_W = BRANCH_W
HY_ORDER = 2
HY_DIRS = 2
HY_BANDS = 16
HY_EMB = 2 * HY_BANDS + 1
HY_FFN = 64
HY_DECAY_TARGET = 1e-2
HY_FAST_PCT = 0.3
HY_SLOW_PCT = 1.5
N_EXPERTS = 32
TOP_K = 4
D_FF = D_MODEL
SWIGLU_LIMIT = 7.0
SWIGLU_ALPHA = 1.702
MOE_BLOCK = 128
LN_EPS = 1e-5
RMS_EPS = 1e-6
DEEPNORM_ALPHA = (2 * DEPTH) ** 0.25
IN_SPLITS = (
    GLA_HEADS * GLA_DK, GLA_HEADS * GLA_DV, GLA_RANK, GLA_RANK, MLA_KV_RANK, MLA_ROPE,
    GLA_HEADS * GLA_DK, GLA_HEADS * GLA_DV, MLA_Q_RANK, 3 * HY_W, N_BRANCH * D_MODEL,
)
N_KEY_GROUPS = 6
KEY_COLS = sum(IN_SPLITS[:N_KEY_GROUPS])
IN_OFFSETS = tuple(int(o) for o in np.cumsum(IN_SPLITS)[:-1])
IN_TOTAL = sum(IN_SPLITS)
F32 = jnp.float32
BF16 = jnp.bfloat16


def _mm_kernel(a_ref, w_ref, o_ref):
    o_ref[...] = jnp.dot(a_ref[...], w_ref[...], preferred_element_type=F32)


def _pick(n, cands):
    for c in cands:
        if n % c == 0:
            return c
    return n


def pmm(a, w):
    lead = a.shape[:-1]
    K = a.shape[-1]
    N = w.shape[-1]
    a2 = a.reshape(-1, K).astype(BF16)
    M = a2.shape[0]
    N0 = N
    if N % 128:
        N = -(-N // 256) * 256
        w = jnp.pad(w, ((0, 0), (0, N - N0)))
    tm = _pick(M, (512, 256, 128, 64, 32, 16, 8))
    tn = _pick(N, (512, 256, 128))
    out = pl.pallas_call(
        _mm_kernel,
        grid=(M // tm, N // tn),
        in_specs=[pl.BlockSpec((tm, K), lambda i, j: (i, 0)),
                  pl.BlockSpec((K, tn), lambda i, j: (0, j))],
        out_specs=pl.BlockSpec((tm, tn), lambda i, j: (i, j)),
        out_shape=jax.ShapeDtypeStruct((M, N), F32),
    )(a2, w.astype(BF16))
    return out[:, :N0].reshape(lead + (N0,))


HEAD_PAD = 128
ATT_TQ = 256
ATT_TK = 768
ONES_LANE_EVEN = MLA_DV
ONES_LANE_ODD = 0
LOG2E = 1.4426950408889634


def _attn_kernel(q_ref, k_ref, v_ref, o_ref, s_buf, p_buf, m_buf, a_buf, mrun, acc, *, n_ctx, tk):
    tq = q_ref.shape[1]
    T = k_ref.shape[1]
    in_ctx = pl.program_id(2) * tq < n_ctx
    lane = lax.broadcasted_iota(jnp.int32, (tq, HEAD_PAD), 1)
    hs = lambda h: slice(h * HEAD_PAD, (h + 1) * HEAD_PAD)

    def stage_a(c, slot, size):
        start = pl.multiple_of(c * size, size)
        for h in range(2):
            k = k_ref[0, pl.ds(start, size), hs(h)]
            s = lax.dot_general(q_ref[0, :, hs(h)], k, (((1,), (1,)), ((), ())), preferred_element_type=F32)
            s_buf[slot, h, :, :size] = s
            m_prev = mrun[h]
            m_new = jnp.maximum(m_prev, jnp.max(s, axis=-1, keepdims=True))
            mrun[h] = m_new
            m_buf[slot, h] = m_new
            a_buf[slot, h] = jnp.exp2(m_prev - m_new)

    def stage_b(slot, size):
        for h in range(2):
            p_buf[slot, h, :, :size] = jnp.exp2(s_buf[slot, h, :, :size] - m_buf[slot, h]).astype(BF16)

    def stage_c(c, slot, size):
        start = pl.multiple_of(c * size, size)
        for h in range(2):
            v = v_ref[0, pl.ds(start, size), hs(h)]
            acc[h] = a_buf[slot, h] * acc[h] + jnp.dot(p_buf[slot, h, :, :size], v, preferred_element_type=F32)

    def run(n, size):
        for h in range(2):
            mrun[h] = jnp.full((tq, 1), -jnp.inf, F32)
            acc[h] = jnp.zeros((tq, HEAD_PAD), F32)
        if n == 1:
            stage_a(0, 0, size)
            stage_b(0, size)
            stage_c(0, 0, size)
        else:
            stage_a(0, 0, size)
            stage_b(0, size)
            stage_a(1, 1, size)

            def step(c, par):
                stage_c(c, par, size)
                stage_b(1 - par, size)
                stage_a(c + 2, par, size)

            def body(j, carry):
                step(2 * j, 0)
                step(2 * j + 1, 1)
                return carry

            lax.fori_loop(0, (n - 2) // 2, body, 0)
            if (n - 2) % 2:
                step(n - 3, (n - 3) % 2)
            stage_c(n - 2, (n - 2) % 2, size)
            stage_b((n - 1) % 2, size)
            stage_c(n - 1, (n - 1) % 2, size)
        a0, a1 = acc[0], acc[1]
        l0 = a0[:, ONES_LANE_EVEN:ONES_LANE_EVEN + 1]
        l1 = a1[:, ONES_LANE_ODD:ONES_LANE_ODD + 1]
        o_ref[0] = jnp.where(lane < MLA_DV, a0 / l0, a1 / l1).astype(o_ref.dtype)

    @pl.when(in_ctx)
    def _():
        run(1, n_ctx)

    @pl.when(jnp.logical_not(in_ctx))
    def _():
        run(T // tk, tk)


def mla_attention(q, k, v, n_ctx):
    B, T, _ = q.shape
    assert T % ATT_TQ == 0 and T % ATT_TK == 0 and n_ctx % ATT_TQ == 0 and n_ctx <= ATT_TK
    kern = functools.partial(_attn_kernel, n_ctx=n_ctx, tk=ATT_TK)
    return pl.pallas_call(
        kern,
        grid=(B, MLA_HEADS // 2, T // ATT_TQ),
        in_specs=[pl.BlockSpec((1, ATT_TQ, 2 * HEAD_PAD), lambda b, h, i: (b, i, h)),
                  pl.BlockSpec((1, T, 2 * HEAD_PAD), lambda b, h, i: (b, 0, h)),
                  pl.BlockSpec((1, T, 2 * HEAD_PAD), lambda b, h, i: (b, 0, h))],
        out_specs=pl.BlockSpec((1, ATT_TQ, 2 * MLA_DV), lambda b, h, i: (b, i, h)),
        out_shape=jax.ShapeDtypeStruct((B, T, MLA_HEADS * MLA_DV), BF16),
        scratch_shapes=[pltpu.VMEM((2, 2, ATT_TQ, ATT_TK), F32), pltpu.VMEM((2, 2, ATT_TQ, ATT_TK), BF16),
                        pltpu.VMEM((2, 2, ATT_TQ, 1), F32), pltpu.VMEM((2, 2, ATT_TQ, 1), F32),
                        pltpu.VMEM((2, ATT_TQ, 1), F32), pltpu.VMEM((2, ATT_TQ, HEAD_PAD), F32)],
        compiler_params=pltpu.CompilerParams(
            dimension_semantics=("arbitrary", "arbitrary", "arbitrary"),
            vmem_limit_bytes=56 * 1024 * 1024),
        name="mla_attention",
    )(q, k, v)


def interleave_v(v):
    B, T, H, dv = v.shape
    v = v.reshape(B, T, H // 2, 2, dv)
    z = jnp.zeros_like(v[:, :, :, 0])
    one_e = z.at[..., ONES_LANE_EVEN - dv].set(1.0)
    one_o = z.at[..., ONES_LANE_ODD].set(1.0)
    v_even = jnp.concatenate([v[:, :, :, 0], one_e], -1)
    v_odd = jnp.concatenate([one_o, v[:, :, :, 1]], -1)
    return jnp.stack([v_even, v_odd], 3).reshape(B, T, H * HEAD_PAD).astype(BF16)


MOE_TM = 256


def _moe_kernel(be_ref, first_ref, nb_ref, x_ref, g_ref, w1_ref, b1_ref, w2_ref, b2_ref, o_ref,
                w1_bf, w2_bf):
    i = pl.program_id(0)

    @pl.when(jnp.logical_and(i < nb_ref[0], first_ref[i] == 1))
    def _():
        w1_bf[...] = w1_ref[0].astype(BF16)
        w2_bf[...] = w2_ref[0].astype(BF16)

    @pl.when(i < nb_ref[0])
    def _():
        hcat = jnp.dot(x_ref[...], w1_bf[...], preferred_element_type=F32) + b1_ref[0]
        glu = jnp.minimum(hcat[:, :D_FF], SWIGLU_LIMIT)
        lin = jnp.clip(hcat[:, D_FF:], -SWIGLU_LIMIT, SWIGLU_LIMIT)
        act = glu * jax.nn.sigmoid(SWIGLU_ALPHA * glu) * (lin + 1)
        y = jnp.dot(act.astype(BF16), w2_bf[...], preferred_element_type=F32) + b2_ref[0]
        o_ref[...] = (y * g_ref[...]).astype(o_ref.dtype)

    @pl.when(i >= nb_ref[0])
    def _():
        o_ref[...] = jnp.zeros_like(o_ref)


def moe_experts(xb, gate_b, block_e, first, n_used, w1, b1, w2, b2):
    R, D = xb.shape
    n_blocks = R // MOE_TM
    grid_spec = pltpu.PrefetchScalarGridSpec(
        num_scalar_prefetch=3,
        grid=(n_blocks,),
        in_specs=[pl.BlockSpec((MOE_TM, D), lambda i, be, fi, nb: (i, 0)),
                  pl.BlockSpec((MOE_TM, 1), lambda i, be, fi, nb: (i, 0)),
                  pl.BlockSpec((1, D, 2 * D_FF), lambda i, be, fi, nb: (be[i], 0, 0)),
                  pl.BlockSpec((1, 1, 2 * D_FF), lambda i, be, fi, nb: (be[i], 0, 0)),
                  pl.BlockSpec((1, D_FF, D), lambda i, be, fi, nb: (be[i], 0, 0)),
                  pl.BlockSpec((1, 1, D), lambda i, be, fi, nb: (be[i], 0, 0))],
        out_specs=pl.BlockSpec((MOE_TM, D), lambda i, be, fi, nb: (i, 0)),
        scratch_shapes=[pltpu.VMEM((D, 2 * D_FF), BF16), pltpu.VMEM((D_FF, D), BF16)],
    )
    return pl.pallas_call(
        _moe_kernel,
        grid_spec=grid_spec,
        out_shape=jax.ShapeDtypeStruct((R, D), F32),
        compiler_params=pltpu.CompilerParams(
            dimension_semantics=("arbitrary",), vmem_limit_bytes=56 * 1024 * 1024),
        name="moe_experts",
    )(block_e, first, n_used, xb, gate_b, w1, b1[:, None, :], w2, b2[:, None, :])


def moe_ffn(h, router_w, router_b, w1, b1, w2, b2):
    N, D = h.shape
    hb = h.astype(BF16)
    logits = jnp.dot(hb, router_w.astype(BF16), preferred_element_type=F32) + router_b
    top_val, top_idx = lax.top_k(logits, TOP_K)
    gates = jax.nn.softmax(top_val, axis=-1)
    NK = N * TOP_K
    flat_e = top_idx.reshape(NK)
    onehot = (flat_e[:, None] == jnp.arange(N_EXPERTS, dtype=flat_e.dtype)[None, :]).astype(jnp.int32)
    csum = jnp.cumsum(onehot, axis=0)
    rank = jnp.take_along_axis(csum, flat_e[:, None], axis=1)[:, 0] - 1
    counts = csum[-1]
    padded = (counts + MOE_TM - 1) // MOE_TM * MOE_TM
    pad_end = jnp.cumsum(padded)
    pad_start = pad_end - padded
    dest = (pad_start[flat_e] + rank).astype(jnp.int32)
    n_blocks = NK // MOE_TM + N_EXPERTS
    R = n_blocks * MOE_TM
    tok = (jnp.arange(NK, dtype=jnp.int32) // TOP_K)
    buf_tok = jnp.zeros((R,), jnp.int32).at[dest].set(tok)
    buf_gate = jnp.zeros((R,), F32).at[dest].set(gates.reshape(NK))
    blk_start = jnp.arange(n_blocks, dtype=jnp.int32) * MOE_TM
    block_e = jnp.minimum(jnp.searchsorted(pad_end, blk_start, side='right'), N_EXPERTS - 1).astype(jnp.int32)
    first = jnp.concatenate([jnp.ones((1,), jnp.int32), (block_e[1:] != block_e[:-1]).astype(jnp.int32)])
    n_used = (pad_end[-1:] // MOE_TM).astype(jnp.int32)
    yb = moe_experts(hb[buf_tok], buf_gate[:, None], block_e, first, n_used, w1, b1, w2, b2)
    return yb[dest.reshape(N, TOP_K)].sum(axis=1)


def layer_norm(x, gain=None, bias=None):
    xf = x.astype(F32)
    xc = xf - xf.mean(-1, keepdims=True)
    y = xc * lax.rsqrt((xc * xc).mean(-1, keepdims=True) + LN_EPS)
    if gain is not None:
        y = y * gain.astype(F32) + bias.astype(F32)
    return y.astype(x.dtype)


def rms_norm(x, gain):
    xf = x.astype(F32)
    y = xf * lax.rsqrt(jnp.mean(xf * xf, -1, keepdims=True) + RMS_EPS) * gain.astype(F32)
    return y.astype(x.dtype)


def modulate(x, shift, scale):
    return layer_norm(x) * (1 + scale) + shift


def to_heads(t, n):
    B, L, _ = t.shape
    return t.reshape(B, L, n, -1).transpose(0, 2, 1, 3)


def from_heads(t):
    B, H, L, d = t.shape
    return t.transpose(0, 2, 1, 3).reshape(B, L, H * d)


def _rev(t):
    return None if t is None else jnp.flip(t, axis=2)


def axial_rope_angles(L):
    rows = L // GRID_W
    row = jnp.repeat(jnp.arange(rows, dtype=F32), GRID_W)
    col = jnp.tile(jnp.arange(GRID_W, dtype=F32), rows)
    n_freq = MLA_ROPE // 4
    inv = ROPE_BASE ** (-jnp.arange(n_freq, dtype=F32) / n_freq)
    ang = jnp.concatenate([row[:, None] * inv, col[:, None] * inv], -1)
    return jnp.cos(ang), jnp.sin(ang)


def apply_rope(x, cos, sin):
    half = x.shape[-1] // 2
    x1, x2 = x[..., :half].astype(F32), x[..., half:].astype(F32)
    return jnp.concatenate([x1 * cos - x2 * sin, x2 * cos + x1 * sin], -1).astype(x.dtype)


GLA_TILE = 256
GLA_BASE = 16
GLA_LEVELS = (16, 32, 64, 128)
N_SEL = 2 * (len(GLA_LEVELS) + 1)


def gla_constants():
    n = GLA_TILE
    i = np.arange(n)[:, None]
    t = np.arange(n)[None, :]
    sizes = GLA_LEVELS + (n,)
    P = [((t <= i) & (t // s == i // s)) for s in sizes]
    S = [((t > i) & (t // s == i // s)) for s in sizes]
    sel_f = np.concatenate(P + S, 0)
    sel_b = np.concatenate([p.T for p in P] + [s.T for s in S], 0)
    lv = [((i // (2 * s) == t // (2 * s)) & ((i // s) % 2 == 1) & ((t // s) % 2 == 0)) for s in GLA_LEVELS]
    dg = (i // GLA_BASE == t // GLA_BASE) & (t <= i)
    msk_f = np.stack(lv + [dg], 0)
    msk_b = np.stack([m.T for m in lv] + [dg.T], 0)
    return (jnp.asarray(np.stack([sel_f, sel_b], 0), BF16), jnp.asarray(np.stack([msk_f, msk_b], 0), F32))


def _gla_kernel(qf_ref, kf_ref, vf_ref, af_ref, qb_ref, kb_ref, vb_ref, ab_ref,
                wf_ref, bf_ref, wb_ref, bb_ref, sel_ref, msk_ref, of_ref, ob_ref, state):
    TL = GLA_TILE
    NL = len(GLA_LEVELS)

    @pl.when(pl.program_id(1) == 0)
    def _():
        state[...] = jnp.zeros_like(state)

    lane = lax.broadcasted_iota(jnp.int32, (1, 2 * GLA_DK), 1)
    nt = lambda a, b: lax.dot_general(a, b, (((1,), (1,)), ((), ())), preferred_element_type=F32)
    dirs = ((qf_ref, kf_ref, vf_ref, af_ref, wf_ref, bf_ref, of_ref, TL - 1),
            (qb_ref, kb_ref, vb_ref, ab_ref, wb_ref, bb_ref, ob_ref, 0))
    for d, (q_ref, k_ref, v_ref, a_ref, w_ref, b_ref, o_ref, last) in enumerate(dirs):
        x = jnp.dot(a_ref[0].astype(BF16), w_ref[...].astype(BF16), preferred_element_type=F32) + b_ref[...]
        g = jax.nn.log_sigmoid(x) * (1.0 / GLA_TAU)
        g1 = g.astype(BF16)
        g2 = (g - g1.astype(F32)).astype(BF16)
        sel = sel_ref[d]
        ps = jnp.dot(sel, g1, preferred_element_type=F32) + jnp.dot(sel, g2, preferred_element_type=F32)
        seg = lambda j: ps[j * TL:(j + 1) * TL]
        q = q_ref[0].astype(F32) * (GLA_DK ** -0.5)
        k = k_ref[0].astype(F32)
        e_pre = [jnp.exp(seg(j)) for j in range(NL + 1)]
        e_suf = [jnp.exp(seg(NL + 1 + j)) for j in range(NL + 1)]
        qh = [(q * e).astype(BF16) for e in e_pre]
        kh = [(k * e).astype(BF16) for e in e_suf]
        kd = (k * jnp.exp(-seg(0))).astype(BF16)
        dec = e_pre[NL][last:last + 1]
        for h in range(GLA_HEADS):
            pr = slice((h // 2) * 2 * GLA_DK, (h // 2 + 1) * 2 * GLA_DK)
            mine = (lane // GLA_DK) == (h % 2)
            qm = [jnp.where(mine, t[:, pr], 0) for t in qh[:NL]]
            sc = msk_ref[d, NL] * nt(qm[0], kd[:, pr])
            for j in range(NL):
                sc = sc + msk_ref[d, j] * nt(qm[j], kh[j][:, pr])
            v = v_ref[0, :, h * GLA_DV:(h + 1) * GLA_DV].astype(BF16)
            st = state[d, h]
            o = jnp.dot(sc.astype(BF16), v, preferred_element_type=F32) + nt(qh[NL][:, pr], st.astype(BF16))
            o_ref[0, :, h * GLA_DV:(h + 1) * GLA_DV] = o
            upd = lax.dot_general(v, kh[NL][:, pr], (((0,), (0,)), ((), ())), preferred_element_type=F32)
            state[d, h] = jnp.where(mine, dec[:, pr] * st + upd, 0)


def gla_scan(q, k, v, paf, pab, wa2_f, ba_f, wa2_b, ba_b, n_ctx):
    B, T, _ = q.shape
    TL = GLA_TILE
    assert T % TL == 0 and n_ctx % TL == 0
    NT, NC = T // TL, n_ctx // TL
    sel, msk = gla_constants()
    fwd = lambda b, n: (b, n, 0)
    bwd = lambda b, n: (b, jnp.where(n < NC, NC - 1 - n, NT - 1 - (n - NC)), 0)
    tok = lambda w, im: pl.BlockSpec((1, TL, w), im)
    full = lambda a: pl.BlockSpec(a.shape, lambda b, n: (0,) * a.ndim)
    HK, HV = GLA_HEADS * GLA_DK, GLA_HEADS * GLA_DV
    ba_f2, ba_b2 = ba_f.reshape(1, HK), ba_b.reshape(1, HK)
    return pl.pallas_call(
        _gla_kernel,
        grid=(B, NT),
        in_specs=[tok(HK, fwd), tok(HK, fwd), tok(HV, fwd), tok(GLA_RANK, fwd),
                  tok(HK, bwd), tok(HK, bwd), tok(HV, bwd), tok(GLA_RANK, bwd),
                  full(wa2_f), full(ba_f2), full(wa2_b), full(ba_b2), full(sel), full(msk)],
        out_specs=[tok(HV, fwd), tok(HV, bwd)],
        out_shape=[jax.ShapeDtypeStruct((B, T, HV), F32)] * 2,
        scratch_shapes=[pltpu.VMEM((2, GLA_HEADS, GLA_DV, 2 * GLA_DK), F32)],
        compiler_params=pltpu.CompilerParams(dimension_semantics=("arbitrary", "arbitrary"),
                                             vmem_limit_bytes=48 * 1024 * 1024),
        name="gla_scan",
    )(q, k, v, paf, q, k, v, pab, wa2_f, ba_f2, wa2_b, ba_b2, sel, msk)


def gla_finish(o, r, norm_w):
    B, T, _ = o.shape
    on = rms_norm(o.reshape(B, T, GLA_HEADS, GLA_DV), norm_w).reshape(B, T, GLA_HEADS * GLA_DV)
    return on * jax.nn.silu(r)


def mla_branch(pqa, pkva, pkr, q_norm, w_uq, kv_norm, w_ukv, rope, n_ctx):
    B, T, _ = pqa.shape
    cos, sin = rope
    q = pmm(rms_norm(pqa, q_norm), w_uq).reshape(B, T, MLA_HEADS, MLA_NOPE + MLA_ROPE)
    q = jnp.concatenate([q[..., :MLA_NOPE], apply_rope(q[..., MLA_NOPE:], cos[:, None], sin[:, None])], -1)
    q = q * ((MLA_NOPE + MLA_ROPE) ** -0.5 * LOG2E)
    kv = pmm(rms_norm(pkva, kv_norm), w_ukv).reshape(B, T, MLA_HEADS, MLA_NOPE + MLA_DV)
    k_rope = jnp.broadcast_to(apply_rope(pkr, cos, sin)[:, :, None, :], (B, T, MLA_HEADS, MLA_ROPE))
    k = jnp.concatenate([kv[..., :MLA_NOPE], k_rope], -1)
    pad = HEAD_PAD - MLA_NOPE - MLA_ROPE
    qp = jnp.pad(q, ((0, 0), (0, 0), (0, 0), (0, pad))).reshape(B, T, MLA_HEADS * HEAD_PAD).astype(BF16)
    kp = jnp.pad(k, ((0, 0), (0, 0), (0, 0), (0, pad))).reshape(B, T, MLA_HEADS * HEAD_PAD).astype(BF16)
    return mla_attention(qp, kp, interleave_v(kv[..., MLA_NOPE:]), n_ctx)


def short_conv3(x, w, b):
    xp = jnp.pad(x, ((0, 0), (1, 1), (0, 0)))
    return xp[:, :-2] * w[0] + xp[:, 1:-1] * w[1] + xp[:, 2:] * w[2] + b


def hyena_filters(L, w1, b1, w2, b2, w3, freq):
    t = jnp.linspace(0.0, 1.0, L, dtype=F32)[:, None]
    w = 2 * math.pi * jnp.arange(L, dtype=F32)[:, None] / L
    f = jnp.linspace(1e-4, HY_BANDS - 1, HY_BANDS, dtype=F32)
    z = jnp.concatenate([t, jnp.cos(f * w), -jnp.sin(f * w)], -1)
    fr = freq.astype(F32)
    h = jnp.sin(fr * (z @ w1.astype(F32) + b1.astype(F32)))
    h = jnp.sin(fr * (h @ w2.astype(F32) + b2.astype(F32)))
    h = (h @ w3.astype(F32)).reshape(L, HY_ORDER, HY_DIRS, HY_W)
    deltas = jnp.abs(jnp.linspace(math.log(HY_DECAY_TARGET) / HY_SLOW_PCT,
                                  math.log(HY_DECAY_TARGET) / HY_FAST_PCT, HY_W, dtype=F32))
    h = h * jnp.exp(-t * deltas)[:, None, None, :]
    return h / jnp.sum(jnp.abs(h), axis=(0, 2), keepdims=True)


FFT_R = 128
FFT_N = FFT_R * FFT_R
FFT_COLS = 2048


def _dft_parts(n):
    ang = 2.0 * np.pi * np.outer(np.arange(n), np.arange(n)) / n
    return np.cos(ang), np.sin(ang)


def hyena_fft_constants():
    C, S = _dft_parts(FFT_R)
    h = FFT_R // 2
    m_first = np.block([[C[:, :h], S[:, :h]], [-S[:, :h], C[:, :h]]])
    m_first_real = np.concatenate([C, -S], 0)
    m_mid = np.block([[C, S], [-S, C]])
    m_mid_inv = np.block([[C, -S], [S, C]])
    m_last = np.block([[C[:h], -S[:h]], [S[:h], C[:h]]])
    ang = 2.0 * np.pi * np.outer(np.arange(FFT_R), np.arange(FFT_R)) / FFT_N
    tw = np.stack([np.cos(ang), np.sin(ang)], -1)
    bf = lambda a: jnp.asarray(a, BF16)
    return dict(first=bf(m_first), first_real=bf(m_first_real), mid=bf(m_mid), mid_inv=bf(m_mid_inv),
                last=bf(m_last), tw=jnp.asarray(tw, F32))


def _fft_outer_kernel(m_ref, x_ref, o_ref):
    o_ref[...] = jnp.dot(m_ref[...], x_ref[...].astype(BF16), preferred_element_type=F32).astype(o_ref.dtype)


def fft_first(m, x2d):
    R, W = x2d.shape
    return pl.pallas_call(
        _fft_outer_kernel, grid=(W // FFT_COLS,),
        in_specs=[pl.BlockSpec(m.shape, lambda j: (0, 0)), pl.BlockSpec((R, FFT_COLS), lambda j: (0, j))],
        out_specs=pl.BlockSpec((m.shape[0], FFT_COLS), lambda j: (0, j)),
        out_shape=jax.ShapeDtypeStruct((m.shape[0], W), BF16),
        compiler_params=pltpu.CompilerParams(dimension_semantics=("arbitrary",)),
        name="hyena_fft_first")(m, x2d)


def _fft_mid_kernel(*refs, conv):
    if conv:
        a_ref, tw_ref, mf_ref, mi_ref, h_ref, o_ref = refs
    else:
        a_ref, tw_ref, mf_ref, o_ref = refs
    R = FFT_R
    tc = tw_ref[0, :, 0:1]
    ts = tw_ref[0, :, 1:2]
    ar = a_ref[0, 0].astype(F32)
    ai = a_ref[1, 0].astype(F32)
    x = jnp.concatenate([ar * tc + ai * ts, ai * tc - ar * ts], 0).astype(BF16)
    X = jnp.dot(mf_ref[...], x, preferred_element_type=F32)
    if not conv:
        o_ref[0, 0] = X[:R]
        o_ref[1, 0] = X[R:]
        return
    xr, xi = X[:R], X[R:]
    hr, hi = h_ref[0, 0], h_ref[1, 0]
    y = jnp.concatenate([xr * hr - xi * hi, xr * hi + xi * hr], 0).astype(BF16)
    Bm = jnp.dot(mi_ref[...], y, preferred_element_type=F32)
    br, bi = Bm[:R], Bm[R:]
    o_ref[0, 0] = (br * tc - bi * ts).astype(o_ref.dtype)
    o_ref[1, 0] = (bi * tc + br * ts).astype(o_ref.dtype)


def fft_mid(a, consts, h=None):
    C = a.shape[-1]
    slab = pl.BlockSpec((2, 1, FFT_R, C), lambda k: (0, k, 0, 0))
    twspec = pl.BlockSpec((1, FFT_R, 2), lambda k: (k, 0, 0))
    mspec = pl.BlockSpec((2 * FFT_R, 2 * FFT_R), lambda k: (0, 0))
    conv = h is not None
    ins = [a, consts['tw'], consts['mid']] + ([consts['mid_inv'], h] if conv else [])
    specs = [slab, twspec, mspec] + ([mspec, slab] if conv else [])
    return pl.pallas_call(
        functools.partial(_fft_mid_kernel, conv=conv), grid=(FFT_R,),
        in_specs=specs, out_specs=slab,
        out_shape=jax.ShapeDtypeStruct((2, FFT_R, FFT_R, C), BF16 if conv else F32),
        compiler_params=pltpu.CompilerParams(dimension_semantics=("arbitrary",)),
        name="hyena_fft_mid")(*ins)


def _fft_last_kernel(m_ref, b_ref, z_ref, bias_ref, gate_ref, o_ref):
    y = jnp.dot(m_ref[...], b_ref[...], preferred_element_type=F32)
    o_ref[...] = gate_ref[...] * (y + bias_ref[...] * z_ref[...])


def fft_last(m, b2d, z2d, bias_row, gate2d):
    R, W = z2d.shape
    col = lambda r: pl.BlockSpec((r, FFT_COLS), lambda j: (0, j))
    return pl.pallas_call(
        _fft_last_kernel, grid=(W // FFT_COLS,),
        in_specs=[pl.BlockSpec(m.shape, lambda j: (0, 0)), col(b2d.shape[0]), col(R),
                  pl.BlockSpec((1, FFT_COLS), lambda j: (0, 0)), col(R)],
        out_specs=col(R),
        out_shape=jax.ShapeDtypeStruct((R, W), F32),
        compiler_params=pltpu.CompilerParams(dimension_semantics=("arbitrary",)),
        name="hyena_fft_last")(m, b2d, z2d, bias_row, gate2d)


def two_sided_kernel(h_fwd, h_bwd, n):
    C = h_fwd.shape[1]
    return jnp.concatenate([h_fwd, jnp.zeros((1, C), F32), jnp.flip(h_bwd[1:], 0)], 0) * (1.0 / n)


def filter_spectrum(h_fwd, h_bwd, consts):
    C = h_fwd.shape[1]
    k = two_sided_kernel(h_fwd, h_bwd, FFT_N)
    a = fft_first(consts['first_real'], k.reshape(FFT_R, FFT_R * C))
    return fft_mid(a.reshape(2, FFT_R, FFT_R, C), consts)


def long_conv(z, gate, spec, bias, consts):
    B, L, C = z.shape
    assert B == 2 and 2 * L == FFT_N and FFT_COLS % C == 0
    z2d = z.reshape(FFT_R, FFT_R * C)
    a = fft_first(consts['first'], z2d)
    b = fft_mid(a.reshape(2, FFT_R, FFT_R, C), consts, spec)
    bias_row = jnp.tile(bias.reshape(1, C), (1, FFT_COLS // C))
    y = fft_last(consts['last'], b.reshape(2 * FFT_R, FFT_R * C), z2d, bias_row, gate.reshape(FFT_R, FFT_R * C))
    return y.reshape(B, L, C)


def dense_dft_constants(L):
    C, S = _dft_parts(2 * L)
    m_fwd = np.block([[C[:, :L], S[:, :L]], [-S[:, :L], C[:, :L]]])
    m_fwd_real = np.concatenate([C, -S], 0)
    m_inv = np.block([[C[:L], -S[:L]], [S[:L], C[:L]]])
    bf = lambda a: jnp.asarray(a, BF16)
    return bf(m_fwd), bf(m_fwd_real), bf(m_inv)


def _short_conv_kernel(mf_ref, mfr_ref, mi_ref, z_ref, k_ref, bias_ref, gate_ref, o_ref):
    n = k_ref.shape[0]
    z = z_ref[...]
    X = jnp.dot(mf_ref[...], z.astype(BF16), preferred_element_type=F32)
    Hs = jnp.dot(mfr_ref[...], k_ref[...].astype(BF16), preferred_element_type=F32)
    xr, xi, hr, hi = X[:n], X[n:], Hs[:n], Hs[n:]
    y = jnp.concatenate([xr * hr - xi * hi, xr * hi + xi * hr], 0).astype(BF16)
    o_ref[...] = gate_ref[...] * (jnp.dot(mi_ref[...], y, preferred_element_type=F32) + bias_ref[...] * z)


def short_conv(z, gate, h_fwd, h_bwd, bias):
    B, L, C = z.shape
    assert B == 2
    mf, mfr, mi = dense_dft_constants(L)
    y = pl.pallas_call(
        _short_conv_kernel,
        out_shape=jax.ShapeDtypeStruct((2 * L, C), F32),
        name="hyena_short_conv")(mf, mfr, mi, z.reshape(2 * L, C), two_sided_kernel(h_fwd, h_bwd, 2 * L),
                                 bias.reshape(1, C), gate.reshape(2 * L, C))
    return y.reshape(B, L, C)


def hyena_branch(p, with_ctx, n_ctx, conv_w, conv_b, w1, b1, w2, b2, w3, freq, hbias):
    consts = hyena_fft_constants()
    hy_w = (w1, b1, w2, b2, w3, freq)
    v, x1, x2 = jnp.split(short_conv3(p[:, n_ctx:], conv_w, conv_b), 3, axis=-1)
    filt = hyena_filters(p.shape[1] - n_ctx, *hy_w)
    z = long_conv(v, x1, filter_spectrum(filt[:, 0, 0], filt[:, 0, 1], consts), hbias[0], consts)
    o = long_conv(z, x2, filter_spectrum(filt[:, 1, 0], filt[:, 1, 1], consts), hbias[1], consts)
    if with_ctx:
        vc, xc1, xc2 = jnp.split(short_conv3(p[:, :n_ctx], conv_w, conv_b), 3, axis=-1)
        fc = hyena_filters(n_ctx, *hy_w)
        zc = short_conv(vc, xc1, fc[:, 0, 0], fc[:, 0, 1], hbias[0])
        oc = short_conv(zc, xc2, fc[:, 1, 0], fc[:, 1, 1], hbias[1])
    else:
        oc = jnp.zeros((o.shape[0], n_ctx, o.shape[2]), o.dtype)
    return jnp.concatenate([oc, o], axis=1)


def merge_branches(o_gla, o_mla, o_hy, gates, w_br_gla, w_br_mla, w_br_hy, w_out):
    g1, g2, g3 = jnp.split(jax.nn.sigmoid(gates.astype(F32)).astype(o_gla.dtype), 3, axis=-1)
    y = g1 * pmm(o_gla, w_br_gla) + g2 * pmm(o_mla, w_br_mla) + g3 * pmm(o_hy, w_br_hy)
    return pmm(y, w_out)


def token_mixer(h, rope, n_ctx, with_ctx, w_in, gla_wa2_f, gla_ba_f, gla_wa2_b, gla_ba_b, gla_norm,
                mla_q_norm, mla_w_uq, mla_kv_norm, mla_w_ukv,
                hy_conv_w, hy_conv_b, hy_w1, hy_b1, hy_w2, hy_b2, hy_w3, hy_freq, hy_bias,
                w_br_gla, w_br_mla, w_br_hy, w_out):
    pk, pv, paf, pab, mkva, mkr, pq, pr, mqa, hy, gates = jnp.split(pmm(h, w_in), IN_OFFSETS, axis=-1)
    o_f, o_b = gla_scan(pq, pk, pv, paf, pab, gla_wa2_f, gla_ba_f, gla_wa2_b, gla_ba_b, n_ctx)
    o_gla = gla_finish(o_f + o_b, pr, gla_norm)
    o_mla = mla_branch(mqa, mkva, mkr, mla_q_norm, mla_w_uq, mla_kv_norm, mla_w_ukv, rope, n_ctx)
    hy_w = (hy_conv_w, hy_conv_b, hy_w1, hy_b1, hy_w2, hy_b2, hy_w3, hy_freq, hy_bias)
    o_hy = hyena_branch(hy, with_ctx, n_ctx, *hy_w)
    return merge_branches(o_gla, o_mla, o_hy, gates, w_br_gla, w_br_mla, w_br_hy, w_out)


def kernel(x, c, ctx, c_ctx, ada_w, ada_b, w_in, gla_wa2_f, gla_ba_f, gla_wa2_b, gla_ba_b, gla_norm,
           mla_q_norm, mla_w_uq, mla_kv_norm, mla_w_ukv, hy_conv_w, hy_conv_b, hy_w1, hy_b1, hy_w2, hy_b2,
           hy_w3, hy_freq, hy_bias, w_br_gla, w_br_mla, w_br_hy, w_out, ln1_g, ln1_b, ln2_g, ln2_b,
           router_w, router_b, moe_w1, moe_b1, moe_w2, moe_b2):
    B, L, D = x.shape
    CL = ctx.shape[1]
    T = CL + L
    cos, sin = axial_rope_angles(L)
    rope = (jnp.concatenate([jnp.ones((CL, cos.shape[1]), F32), cos], 0),
            jnp.concatenate([jnp.zeros((CL, sin.shape[1]), F32), sin], 0))
    xa = jnp.concatenate([ctx, x], axis=1)
    c_act = jax.nn.silu(c)
    cc_act = jax.nn.silu(c_ctx)
    for l in range(DEPTH):
        last = l == DEPTH - 1
        ada_x = (c_act @ ada_w[l] + ada_b[l])[:, None, :]
        ada_c = jnp.broadcast_to((cc_act @ ada_w[l] + ada_b[l])[None, None, :], ada_x.shape)
        is_ctx = (jnp.arange(T) < CL)[None, :, None]
        sh1, sc1, g1, sh2, sc2, g2 = [jnp.where(is_ctx, a, b) for a, b in
                                      zip(jnp.split(ada_c, 6, axis=-1), jnp.split(ada_x, 6, axis=-1))]
        y = token_mixer(modulate(xa, sh1, sc1), rope, CL, not last,
                        w_in[l], gla_wa2_f[l], gla_ba_f[l], gla_wa2_b[l], gla_ba_b[l], gla_norm[l],
                        mla_q_norm[l], mla_w_uq[l], mla_kv_norm[l], mla_w_ukv[l],
                        hy_conv_w[l], hy_conv_b[l], hy_w1[l], hy_b1[l], hy_w2[l], hy_b2[l], hy_w3[l],
                        hy_freq[l], hy_bias[l], w_br_gla[l], w_br_mla[l], w_br_hy[l], w_out[l])
        xa = layer_norm(DEEPNORM_ALPHA * xa + g1 * y, ln1_g[l], ln1_b[l])
        h = modulate(xa, sh2, sc2)
        moe_w = (router_w[l], router_b[l], moe_w1[l], moe_b1[l], moe_w2[l], moe_b2[l])
        if last:
            f = moe_ffn(h[:, CL:].reshape(B * L, D), *moe_w).reshape(B, L, D)
            return layer_norm(DEEPNORM_ALPHA * xa[:, CL:] + g2[:, CL:] * f, ln2_g[l], ln2_b[l])
        f = moe_ffn(h.reshape(B * T, D), *moe_w).reshape(B, T, D)
        xa = layer_norm(DEEPNORM_ALPHA * xa + g2 * f, ln2_g[l], ln2_b[l])
```

```python
import functools
import math
import jax
import jax.numpy as jnp
from jax import lax
import numpy as np
from jax.experimental import pallas as pl
from jax.experimental.pallas import tpu as pltpu

D_MODEL = 1024
DEPTH = 2
GRID_W = 64
BRANCH_W = D_MODEL // 2
N_BRANCH = 3
GLA_HEADS = 4
GLA_DV = BRANCH_W // GLA_HEADS
GLA_DK = GLA_DV // 2
GLA_RANK = 16
GLA_TAU = 16.0
GLA_CHUNK = 16
MLA_HEADS = 8
MLA_DV = BRANCH_W // MLA_HEADS
MLA_NOPE = MLA_DV
MLA_ROPE = MLA_NOPE // 2
MLA_Q_RANK = 3 * D_MODEL // 8
MLA_KV_RANK = D_MODEL // 4
ROPE_BASE = 10000.0
Q_BLOCK = 128
HY_W = BRANCH_W
HY_ORDER = 2
HY_DIRS = 2
HY_BANDS = 16
HY_EMB = 2 * HY_BANDS + 1
HY_FFN = 64
HY_DECAY_TARGET = 1e-2
HY_FAST_PCT = 0.3
HY_SLOW_PCT = 1.5
N_EXPERTS = 32
TOP_K = 4
D_FF = D_MODEL
SWIGLU_LIMIT = 7.0
SWIGLU_ALPHA = 1.702
MOE_BLOCK = 128
LN_EPS = 1e-5
RMS_EPS = 1e-6
DEEPNORM_ALPHA = (2 * DEPTH) ** 0.25
IN_SPLITS = (
    GLA_HEADS * GLA_DK, GLA_HEADS * GLA_DV, GLA_RANK, GLA_RANK, MLA_KV_RANK, MLA_ROPE,
    GLA_HEADS * GLA_DK, GLA_HEADS * GLA_DV, MLA_Q_RANK, 3 * HY_W, N_BRANCH * D_MODEL,
)
N_KEY_GROUPS = 6
KEY_COLS = sum(IN_SPLITS[:N_KEY_GROUPS])
IN_OFFSETS = tuple(int(o) for o in np.cumsum(IN_SPLITS)[:-1])
IN_TOTAL = sum(IN_SPLITS)
F32 = jnp.float32
BF16 = jnp.bfloat16


def _mm_kernel(a_ref, w_ref, o_ref):
    o_ref[...] = jnp.dot(a_ref[...], w_ref[...], preferred_element_type=F32)


def _pick(n, cands):
    for c in cands:
        if n % c == 0:
            return c
    return n


def pmm(a, w):
    lead = a.shape[:-1]
    K = a.shape[-1]
    N = w.shape[-1]
    a2 = a.reshape(-1, K).astype(BF16)
    M = a2.shape[0]
    N0 = N
    if N % 128:
        N = -(-N // 256) * 256
        w = jnp.pad(w, ((0, 0), (0, N - N0)))
    tm = _pick(M, (512, 256, 128, 64, 32, 16, 8))
    tn = _pick(N, (512, 256, 128))
    out = pl.pallas_call(
        _mm_kernel,
        grid=(M // tm, N // tn),
        in_specs=[pl.BlockSpec((tm, K), lambda i, j: (i, 0)),
                  pl.BlockSpec((K, tn), lambda i, j: (0, j))],
        out_specs=pl.BlockSpec((tm, tn), lambda i, j: (i, j)),
        out_shape=jax.ShapeDtypeStruct((M, N), F32),
    )(a2, w.astype(BF16))
    return out[:, :N0].reshape(lead + (N0,))


ROW_TILE = 256
HEAD_PAD = 128
N_GATE = N_BRANCH * D_MODEL
N_HY = 3 * HY_W
COL = dict(gate=0, hy=N_GATE, qa=N_GATE + N_HY)
COL['misc'] = COL['qa'] + MLA_Q_RANK
COL['v'] = COL['misc'] + HEAD_PAD
COL['r'] = COL['v'] + GLA_HEADS * GLA_DV
COL['k'] = COL['r'] + GLA_HEADS * GLA_DV
COL['q'] = COL['k'] + GLA_HEADS * GLA_DK
COL['kva'] = COL['q'] + GLA_HEADS * GLA_DK
P_COLS = COL['kva'] + MLA_KV_RANK
WIDTH = dict(gate=N_GATE, hy=N_HY, qa=MLA_Q_RANK, misc=HEAD_PAD, v=GLA_HEADS * GLA_DV, r=GLA_HEADS * GLA_DV,
             k=GLA_HEADS * GLA_DK, q=GLA_HEADS * GLA_DK, kva=MLA_KV_RANK)
MISC_AF, MISC_AB, MISC_KR = 0, GLA_RANK, MLA_NOPE
P_CHUNK = 768
assert all(COL[g] % WIDTH[g] == 0 for g in COL) and P_COLS % P_CHUNK == 0


def permute_w_in(w_in):
    o = (0,) + IN_OFFSETS + (IN_TOTAL,)
    k, v, af, ab, kva, kr, q, r, qa, hy, gate = [w_in[:, o[i]:o[i + 1]] for i in range(len(IN_SPLITS))]
    z = lambda n: jnp.zeros((w_in.shape[0], n), w_in.dtype)
    misc = jnp.concatenate([af, ab, z(MISC_KR - 2 * GLA_RANK), kr, z(HEAD_PAD - MISC_KR - MLA_ROPE)], 1)
    return jnp.concatenate([gate, hy, qa, misc, v, r, k, q, kva], 1).astype(BF16)


def _layer_norm(x):
    xc = x - jnp.mean(x, axis=-1, keepdims=True)
    return xc * lax.rsqrt(jnp.mean(xc * xc, axis=-1, keepdims=True) + LN_EPS)


def _rms(x, gain):
    return x * lax.rsqrt(jnp.mean(x * x, axis=-1, keepdims=True) + RMS_EPS) * gain


def _proj_kernel(x_ref, mod_ref, w_ref, o_ref):
    h = _layer_norm(x_ref[0]) * (1.0 + mod_ref[0, 0, 1:2]) + mod_ref[0, 0, 0:1]
    hb = h.astype(BF16)
    for c in range(0, P_COLS, P_CHUNK):
        o_ref[0, :, c:c + P_CHUNK] = jnp.dot(hb, w_ref[:, c:c + P_CHUNK], preferred_element_type=F32).astype(BF16)


def _tile_kind(n_ctx_tiles):
    return lambda b, t: (b, jnp.where(t < n_ctx_tiles, 1, 0), 0, 0)


def ln_mod_proj(xa, mod, w_p, n_ctx):
    B, T, D = xa.shape
    return pl.pallas_call(
        _proj_kernel, grid=(B, T // ROW_TILE),
        in_specs=[pl.BlockSpec((1, ROW_TILE, D), lambda b, t: (b, t, 0)),
                  pl.BlockSpec((1, 1, 2, D), _tile_kind(n_ctx // ROW_TILE)),
                  pl.BlockSpec((D, P_COLS), lambda b, t: (0, 0), pipeline_mode=pl.Buffered(1))],
        out_specs=pl.BlockSpec((1, ROW_TILE, P_COLS), lambda b, t: (b, t, 0)),
        out_shape=jax.ShapeDtypeStruct((B, T, P_COLS), BF16),
        compiler_params=pltpu.CompilerParams(dimension_semantics=("arbitrary", "arbitrary"),
                                             vmem_limit_bytes=48 * 1024 * 1024),
        name="ln_mod_proj")(xa, mod, w_p)


def pcol(group, row_map=lambda b, t: t):
    w = WIDTH[group]
    return pl.BlockSpec((1, ROW_TILE, w), lambda b, t, _i=COL[group] // w: (b, row_map(b, t), _i))


ONES_LANE_EVEN = MLA_DV
ONES_LANE_ODD = 0
LOG2E = 1.4426950408889634


def rope_tables(L, n_ctx):
    rows = L // GRID_W
    row = jnp.repeat(jnp.arange(rows, dtype=F32), GRID_W)
    col = jnp.tile(jnp.arange(GRID_W, dtype=F32), rows)
    n_freq = MLA_ROPE // 4
    inv = ROPE_BASE ** (-jnp.arange(n_freq, dtype=F32) / n_freq)
    ang = jnp.concatenate([row[:, None] * inv, col[:, None] * inv], -1)
    ang = jnp.concatenate([jnp.zeros((n_ctx, ang.shape[1]), F32), ang], 0)
    cos, sin = jnp.cos(ang), jnp.sin(ang)
    T = ang.shape[0]
    half = MLA_ROPE // 2
    z = lambda n: jnp.zeros((T, n), F32)
    ct = jnp.concatenate([jnp.ones((T, MLA_NOPE), F32), cos, cos, z(HEAD_PAD - MLA_NOPE - MLA_ROPE)], 1)
    sa = jnp.concatenate([z(MLA_NOPE), -sin, z(HEAD_PAD - MLA_NOPE - half)], 1)
    sb = jnp.concatenate([z(MLA_NOPE + half), sin, z(HEAD_PAD - MLA_NOPE - MLA_ROPE)], 1)
    return ct, sa, sb


def mla_weights(w_uq, w_ukv):
    H = MLA_HEADS
    wq = w_uq.reshape(MLA_Q_RANK, H, MLA_NOPE + MLA_ROPE)
    wq = jnp.pad(wq, ((0, 0), (0, 0), (0, HEAD_PAD - MLA_NOPE - MLA_ROPE))).reshape(MLA_Q_RANK, H * HEAD_PAD)
    wkv = w_ukv.reshape(MLA_KV_RANK, H, MLA_NOPE + MLA_DV)
    wk = jnp.pad(wkv[:, :, :MLA_NOPE], ((0, 0), (0, 0), (0, HEAD_PAD - MLA_NOPE))).reshape(MLA_KV_RANK, H * HEAD_PAD)
    wv = wkv[:, :, MLA_NOPE:].reshape(MLA_KV_RANK, H // 2, 2, MLA_DV)
    z = jnp.zeros_like(wv[:, :, 0])
    wv = jnp.stack([jnp.concatenate([wv[:, :, 0], z], -1), jnp.concatenate([z, wv[:, :, 1]], -1)], 2)
    ones = np.zeros((H // 2, 2, HEAD_PAD), np.float32)
    ones[:, 0, ONES_LANE_EVEN] = 1.0
    ones[:, 1, ONES_LANE_ODD] = 1.0
    return (wq.astype(BF16), wk.astype(BF16), wv.reshape(MLA_KV_RANK, H * HEAD_PAD).astype(BF16),
            jnp.asarray(ones.reshape(1, H * HEAD_PAD)))


def _mla_prep_kernel(qa_ref, kva_ref, misc_ref, ct_ref, sa_ref, sb_ref, wq_ref, wk_ref, wv_ref, ones_ref,
                     qn_ref, kn_ref, q_ref, k_ref, v_ref):
    ct, sa, sb = ct_ref[...], sa_ref[...], sb_ref[...]
    half = MLA_ROPE // 2
    rope = lambda x: x * ct + pltpu.roll(x, HEAD_PAD - half, 1) * sa + pltpu.roll(x, half, 1) * sb
    qn = _rms(qa_ref[0].astype(F32), qn_ref[...]).astype(BF16)
    q = jnp.dot(qn, wq_ref[...], preferred_element_type=F32)
    kn = _rms(kva_ref[0].astype(F32), kn_ref[...]).astype(BF16)
    k = jnp.dot(kn, wk_ref[...], preferred_element_type=F32)
    v_ref[0] = (jnp.dot(kn, wv_ref[...], preferred_element_type=F32) + ones_ref[...]).astype(BF16)
    lane = lax.broadcasted_iota(jnp.int32, (1, HEAD_PAD), 1)
    is_kr = (lane >= MISC_KR) & (lane < MISC_KR + MLA_ROPE)
    kr = rope(jnp.where(is_kr, misc_ref[0].astype(F32), 0.0))
    qscale = (MLA_NOPE + MLA_ROPE) ** -0.5 * LOG2E
    for h in range(MLA_HEADS):
        hs = slice(h * HEAD_PAD, (h + 1) * HEAD_PAD)
        q_ref[0, :, hs] = (rope(q[:, hs]) * qscale).astype(BF16)
        k_ref[0, :, hs] = (k[:, hs] + kr).astype(BF16)


def mla_prep(P, tables, weights, q_norm, kv_norm):
    B, T, _ = P.shape
    ct, sa, sb = tables
    wq, wk, wv, ones = weights
    full = lambda a: pl.BlockSpec(a.shape, lambda b, t: (0,) * a.ndim)
    tab = pl.BlockSpec((ROW_TILE, HEAD_PAD), lambda b, t: (t, 0))
    qn, kn = q_norm.reshape(1, -1), kv_norm.reshape(1, -1)
    W = MLA_HEADS * HEAD_PAD
    out = pl.BlockSpec((1, ROW_TILE, W), lambda b, t: (b, t, 0))
    return pl.pallas_call(
        _mla_prep_kernel, grid=(B, T // ROW_TILE),
        in_specs=[pcol('qa'), pcol('kva'), pcol('misc'), tab, tab, tab, full(wq), full(wk), full(wv), full(ones),
                  full(qn), full(kn)],
        out_specs=[out, out, out],
        out_shape=[jax.ShapeDtypeStruct((B, T, W), BF16)] * 3,
        compiler_params=pltpu.CompilerParams(dimension_semantics=("arbitrary", "arbitrary")),
        name="mla_prep")(P, P, P, ct, sa, sb, wq, wk, wv, ones, qn, kn)


ATT_TQ = 256
ATT_TK = 768


def _attn_kernel(q_ref, k_ref, v_ref, o_ref, s_buf, p_buf, m_buf, a_buf, mrun, acc, *, n_ctx, tk):
    tq = q_ref.shape[1]
    T = k_ref.shape[1]
    in_ctx = pl.program_id(2) * tq < n_ctx
    lane = lax.broadcasted_iota(jnp.int32, (tq, HEAD_PAD), 1)
    hs = lambda h: slice(h * HEAD_PAD, (h + 1) * HEAD_PAD)

    def stage_a(c, slot, size):
        start = pl.multiple_of(c * size, size)
        for h in range(2):
            k = k_ref[0, pl.ds(start, size), hs(h)]
            s = lax.dot_general(q_ref[0, :, hs(h)], k, (((1,), (1,)), ((), ())), preferred_element_type=F32)
            s_buf[slot, h, :, :size] = s
            m_prev = mrun[h]
            m_new = jnp.maximum(m_prev, jnp.max(s, axis=-1, keepdims=True))
            mrun[h] = m_new
            m_buf[slot, h] = m_new
            a_buf[slot, h] = jnp.exp2(m_prev - m_new)

    def stage_b(slot, size):
        for h in range(2):
            p_buf[slot, h, :, :size] = jnp.exp2(s_buf[slot, h, :, :size] - m_buf[slot, h]).astype(BF16)

    def stage_c(c, slot, size):
        start = pl.multiple_of(c * size, size)
        for h in range(2):
            v = v_ref[0, pl.ds(start, size), hs(h)]
            acc[h] = a_buf[slot, h] * acc[h] + jnp.dot(p_buf[slot, h, :, :size], v, preferred_element_type=F32)

    def run(n, size):
        for h in range(2):
            mrun[h] = jnp.full((tq, 1), -jnp.inf, F32)
            acc[h] = jnp.zeros((tq, HEAD_PAD), F32)
        if n == 1:
            stage_a(0, 0, size)
            stage_b(0, size)
            stage_c(0, 0, size)
        else:
            stage_a(0, 0, size)
            stage_b(0, size)
            stage_a(1, 1, size)

            def step(c, par):
                stage_c(c, par, size)
                stage_b(1 - par, size)
                stage_a(c + 2, par, size)

            def body(j, carry):
                step(2 * j, 0)
                step(2 * j + 1, 1)
                return carry

            lax.fori_loop(0, (n - 2) // 2, body, 0)
            if (n - 2) % 2:
                step(n - 3, (n - 3) % 2)
            stage_c(n - 2, (n - 2) % 2, size)
            stage_b((n - 1) % 2, size)
            stage_c(n - 1, (n - 1) % 2, size)
        a0, a1 = acc[0], acc[1]
        l0 = a0[:, ONES_LANE_EVEN:ONES_LANE_EVEN + 1]
        l1 = a1[:, ONES_LANE_ODD:ONES_LANE_ODD + 1]
        o_ref[0] = jnp.where(lane < MLA_DV, a0 / l0, a1 / l1).astype(o_ref.dtype)

    @pl.when(in_ctx)
    def _():
        run(1, n_ctx)

    @pl.when(jnp.logical_not(in_ctx))
    def _():
        run(T // tk, tk)


def mla_attention(q, k, v, n_ctx):
    B, T, _ = q.shape
    assert T % ATT_TQ == 0 and T % ATT_TK == 0 and n_ctx % ATT_TQ == 0 and n_ctx <= ATT_TK
    kern = functools.partial(_attn_kernel, n_ctx=n_ctx, tk=ATT_TK)
    return pl.pallas_call(
        kern,
        grid=(B, MLA_HEADS // 2, T // ATT_TQ),
        in_specs=[pl.BlockSpec((1, ATT_TQ, 2 * HEAD_PAD), lambda b, h, i: (b, i, h)),
                  pl.BlockSpec((1, T, 2 * HEAD_PAD), lambda b, h, i: (b, 0, h)),
                  pl.BlockSpec((1, T, 2 * HEAD_PAD), lambda b, h, i: (b, 0, h))],
        out_specs=pl.BlockSpec((1, ATT_TQ, 2 * MLA_DV), lambda b, h, i: (b, i, h)),
        out_shape=jax.ShapeDtypeStruct((B, T, MLA_HEADS * MLA_DV), BF16),
        scratch_shapes=[pltpu.VMEM((2, 2, ATT_TQ, ATT_TK), F32), pltpu.VMEM((2, 2, ATT_TQ, ATT_TK), BF16),
                        pltpu.VMEM((2, 2, ATT_TQ, 1), F32), pltpu.VMEM((2, 2, ATT_TQ, 1), F32),
                        pltpu.VMEM((2, ATT_TQ, 1), F32), pltpu.VMEM((2, ATT_TQ, HEAD_PAD), F32)],
        compiler_params=pltpu.CompilerParams(
            dimension_semantics=("arbitrary", "arbitrary", "arbitrary"),
            vmem_limit_bytes=56 * 1024 * 1024),
        name="mla_attention",
    )(q, k, v)


MOE_TM = 256


def _moe_kernel(be_ref, first_ref, nb_ref, x_ref, g_ref, w1_ref, b1_ref, w2_ref, b2_ref, o_ref,
                w1_bf, w2_bf):
    i = pl.program_id(0)

    @pl.when(jnp.logical_and(i < nb_ref[0], first_ref[i] == 1))
    def _():
        w1_bf[...] = w1_ref[0].astype(BF16)
        w2_bf[...] = w2_ref[0].astype(BF16)

    @pl.when(i < nb_ref[0])
    def _():
        hcat = jnp.dot(x_ref[...], w1_bf[...], preferred_element_type=F32) + b1_ref[0]
        glu = jnp.minimum(hcat[:, :D_FF], SWIGLU_LIMIT)
        lin = jnp.clip(hcat[:, D_FF:], -SWIGLU_LIMIT, SWIGLU_LIMIT)
        act = glu * jax.nn.sigmoid(SWIGLU_ALPHA * glu) * (lin + 1)
        y = jnp.dot(act.astype(BF16), w2_bf[...], preferred_element_type=F32) + b2_ref[0]
        o_ref[...] = (y * g_ref[...]).astype(o_ref.dtype)

    @pl.when(i >= nb_ref[0])
    def _():
        o_ref[...] = jnp.zeros_like(o_ref)


def moe_experts(xb, gate_b, block_e, first, n_used, w1, b1, w2, b2):
    R, D = xb.shape
    n_blocks = R // MOE_TM
    grid_spec = pltpu.PrefetchScalarGridSpec(
        num_scalar_prefetch=3,
        grid=(n_blocks,),
        in_specs=[pl.BlockSpec((MOE_TM, D), lambda i, be, fi, nb: (i, 0)),
                  pl.BlockSpec((MOE_TM, 1), lambda i, be, fi, nb: (i, 0)),
                  pl.BlockSpec((1, D, 2 * D_FF), lambda i, be, fi, nb: (be[i], 0, 0)),
                  pl.BlockSpec((1, 1, 2 * D_FF), lambda i, be, fi, nb: (be[i], 0, 0)),
                  pl.BlockSpec((1, D_FF, D), lambda i, be, fi, nb: (be[i], 0, 0)),
                  pl.BlockSpec((1, 1, D), lambda i, be, fi, nb: (be[i], 0, 0))],
        out_specs=pl.BlockSpec((MOE_TM, D), lambda i, be, fi, nb: (i, 0)),
        scratch_shapes=[pltpu.VMEM((D, 2 * D_FF), BF16), pltpu.VMEM((D_FF, D), BF16)],
    )
    return pl.pallas_call(
        _moe_kernel,
        grid_spec=grid_spec,
        out_shape=jax.ShapeDtypeStruct((R, D), F32),
        compiler_params=pltpu.CompilerParams(
            dimension_semantics=("arbitrary",), vmem_limit_bytes=56 * 1024 * 1024),
        name="moe_experts",
    )(block_e, first, n_used, xb, gate_b, w1, b1[:, None, :], w2, b2[:, None, :])


def moe_ffn(hb, logits, w1, b1, w2, b2):
    N, D = hb.shape
    top_val, top_idx = lax.top_k(logits, TOP_K)
    gates = jax.nn.softmax(top_val, axis=-1)
    NK = N * TOP_K
    flat_e = top_idx.reshape(NK)
    onehot = (flat_e[:, None] == jnp.arange(N_EXPERTS, dtype=flat_e.dtype)[None, :]).astype(jnp.int32)
    csum = jnp.cumsum(onehot, axis=0)
    rank = jnp.take_along_axis(csum, flat_e[:, None], axis=1)[:, 0] - 1
    counts = csum[-1]
    padded = (counts + MOE_TM - 1) // MOE_TM * MOE_TM
    pad_end = jnp.cumsum(padded)
    pad_start = pad_end - padded
    dest = (pad_start[flat_e] + rank).astype(jnp.int32)
    n_blocks = NK // MOE_TM + N_EXPERTS
    R = n_blocks * MOE_TM
    tok = (jnp.arange(NK, dtype=jnp.int32) // TOP_K)
    buf_tok = jnp.zeros((R,), jnp.int32).at[dest].set(tok)
    buf_gate = jnp.zeros((R,), F32).at[dest].set(gates.reshape(NK))
    blk_start = jnp.arange(n_blocks, dtype=jnp.int32) * MOE_TM
    block_e = jnp.minimum(jnp.searchsorted(pad_end, blk_start, side='right'), N_EXPERTS - 1).astype(jnp.int32)
    first = jnp.concatenate([jnp.ones((1,), jnp.int32), (block_e[1:] != block_e[:-1]).astype(jnp.int32)])
    n_used = (pad_end[-1:] // MOE_TM).astype(jnp.int32)
    yb = moe_experts(hb[buf_tok], buf_gate[:, None], block_e, first, n_used, w1, b1, w2, b2)
    return yb[dest.reshape(N, TOP_K)].sum(axis=1)


def layer_norm(x, gain=None, bias=None):
    xf = x.astype(F32)
    xc = xf - xf.mean(-1, keepdims=True)
    y = xc * lax.rsqrt((xc * xc).mean(-1, keepdims=True) + LN_EPS)
    if gain is not None:
        y = y * gain.astype(F32) + bias.astype(F32)
    return y.astype(x.dtype)


GLA_TILE = 256
GLA_BASE = 16
GLA_LEVELS = (16, 32, 64, 128)
N_SEL = 2 * (len(GLA_LEVELS) + 1)


def gla_constants():
    n = GLA_TILE
    i = np.arange(n)[:, None]
    t = np.arange(n)[None, :]
    sizes = GLA_LEVELS + (n,)
    P = [((t <= i) & (t // s == i // s)) for s in sizes]
    S = [((t > i) & (t // s == i // s)) for s in sizes]
    sel_f = np.concatenate(P + S, 0)
    sel_b = np.concatenate([p.T for p in P] + [s.T for s in S], 0)
    lv = [((i // (2 * s) == t // (2 * s)) & ((i // s) % 2 == 1) & ((t // s) % 2 == 0)) for s in GLA_LEVELS]
    dg = (i // GLA_BASE == t // GLA_BASE) & (t <= i)
    msk_f = np.stack(lv + [dg], 0)
    msk_b = np.stack([m.T for m in lv] + [dg.T], 0)
    return (jnp.asarray(np.stack([sel_f, sel_b], 0), BF16), jnp.asarray(np.stack([msk_f, msk_b], 0), F32))


def _gla_kernel(qf_ref, kf_ref, vf_ref, af_ref, qb_ref, kb_ref, vb_ref, ab_ref,
                wf_ref, bf_ref, wb_ref, bb_ref, sel_ref, msk_ref, of_ref, ob_ref, state):
    TL = GLA_TILE
    NL = len(GLA_LEVELS)

    @pl.when(pl.program_id(1) == 0)
    def _():
        state[...] = jnp.zeros_like(state)

    lane = lax.broadcasted_iota(jnp.int32, (1, 2 * GLA_DK), 1)
    nt = lambda a, b: lax.dot_general(a, b, (((1,), (1,)), ((), ())), preferred_element_type=F32)
    dirs = ((qf_ref, kf_ref, vf_ref, af_ref, wf_ref, bf_ref, of_ref, TL - 1),
            (qb_ref, kb_ref, vb_ref, ab_ref, wb_ref, bb_ref, ob_ref, 0))
    for d, (q_ref, k_ref, v_ref, a_ref, w_ref, b_ref, o_ref, last) in enumerate(dirs):
        x = jnp.dot(a_ref[0], w_ref[...], preferred_element_type=F32) + b_ref[...]
        g = jax.nn.log_sigmoid(x) * (1.0 / GLA_TAU)
        g1 = g.astype(BF16)
        g2 = (g - g1.astype(F32)).astype(BF16)
        sel = sel_ref[d]
        ps = jnp.dot(sel, g1, preferred_element_type=F32) + jnp.dot(sel, g2, preferred_element_type=F32)
        seg = lambda j: ps[j * TL:(j + 1) * TL]
        q = q_ref[0].astype(F32) * (GLA_DK ** -0.5)
        k = k_ref[0].astype(F32)
        e_pre = [jnp.exp(seg(j)) for j in range(NL + 1)]
        e_suf = [jnp.exp(seg(NL + 1 + j)) for j in range(NL + 1)]
        qh = [(q * e).astype(BF16) for e in e_pre]
        kh = [(k * e).astype(BF16) for e in e_suf]
        kd = (k * jnp.exp(-seg(0))).astype(BF16)
        dec = e_pre[NL][last:last + 1]
        for h in range(GLA_HEADS):
            pr = slice((h // 2) * 2 * GLA_DK, (h // 2 + 1) * 2 * GLA_DK)
            mine = (lane // GLA_DK) == (h % 2)
            qm = [jnp.where(mine, t[:, pr], 0) for t in qh[:NL]]
            sc = msk_ref[d, NL] * nt(qm[0], kd[:, pr])
            for j in range(NL):
                sc = sc + msk_ref[d, j] * nt(qm[j], kh[j][:, pr])
            v = v_ref[0, :, h * GLA_DV:(h + 1) * GLA_DV].astype(BF16)
            st = state[d, h]
            o = jnp.dot(sc.astype(BF16), v, preferred_element_type=F32) + nt(qh[NL][:, pr], st.astype(BF16))
            o_ref[0, :, h * GLA_DV:(h + 1) * GLA_DV] = o
            upd = lax.dot_general(v, kh[NL][:, pr], (((0,), (0,)), ((), ())), preferred_element_type=F32)
            state[d, h] = jnp.where(mine, dec[:, pr] * st + upd, 0)


def gla_scan(P, wa2_f, ba_f, wa2_b, ba_b, n_ctx):
    B, T, _ = P.shape
    assert GLA_TILE == ROW_TILE and T % ROW_TILE == 0 and n_ctx % ROW_TILE == 0
    NT, NC = T // ROW_TILE, n_ctx // ROW_TILE
    sel, msk = gla_constants()
    fwd = lambda b, n: n
    bwd = lambda b, n: jnp.where(n < NC, NC - 1 - n, NT - 1 - (n - NC))
    full = lambda a: pl.BlockSpec(a.shape, lambda b, n: (0,) * a.ndim)
    HK, HV = GLA_HEADS * GLA_DK, GLA_HEADS * GLA_DV
    pad = lambda w, off: jnp.zeros((HEAD_PAD, HK), F32).at[off:off + GLA_RANK].set(w).astype(BF16)
    wf, wb = pad(wa2_f, MISC_AF), pad(wa2_b, MISC_AB)
    ba_f2, ba_b2 = ba_f.reshape(1, HK), ba_b.reshape(1, HK)
    out = lambda rm: pl.BlockSpec((1, ROW_TILE, HV), lambda b, n: (b, rm(b, n), 0))
    return pl.pallas_call(
        _gla_kernel,
        grid=(B, NT),
        in_specs=[pcol('q', fwd), pcol('k', fwd), pcol('v', fwd), pcol('misc', fwd),
                  pcol('q', bwd), pcol('k', bwd), pcol('v', bwd), pcol('misc', bwd),
                  full(wf), full(ba_f2), full(wb), full(ba_b2), full(sel), full(msk)],
        out_specs=[out(fwd), out(bwd)],
        out_shape=[jax.ShapeDtypeStruct((B, T, HV), F32)] * 2,
        scratch_shapes=[pltpu.VMEM((2, GLA_HEADS, GLA_DV, 2 * GLA_DK), F32)],
        compiler_params=pltpu.CompilerParams(dimension_semantics=("arbitrary", "arbitrary"),
                                             vmem_limit_bytes=48 * 1024 * 1024),
        name="gla_scan",
    )(P, P, P, P, P, P, P, P, wf, ba_f2, wb, ba_b2, sel, msk)


def short_conv3(x, w, b):
    xp = jnp.pad(x, ((0, 0), (1, 1), (0, 0)))
    return xp[:, :-2] * w[0] + xp[:, 1:-1] * w[1] + xp[:, 2:] * w[2] + b


def hyena_filters(L, w1, b1, w2, b2, w3, freq):
    t = jnp.linspace(0.0, 1.0, L, dtype=F32)[:, None]
    w = 2 * math.pi * jnp.arange(L, dtype=F32)[:, None] / L
    f = jnp.linspace(1e-4, HY_BANDS - 1, HY_BANDS, dtype=F32)
    z = jnp.concatenate([t, jnp.cos(f * w), -jnp.sin(f * w)], -1)
    fr = freq.astype(F32)
    h = jnp.sin(fr * (z @ w1.astype(F32) + b1.astype(F32)))
    h = jnp.sin(fr * (h @ w2.astype(F32) + b2.astype(F32)))
    h = (h @ w3.astype(F32)).reshape(L, HY_ORDER, HY_DIRS, HY_W)
    deltas = jnp.abs(jnp.linspace(math.log(HY_DECAY_TARGET) / HY_SLOW_PCT,
                                  math.log(HY_DECAY_TARGET) / HY_FAST_PCT, HY_W, dtype=F32))
    h = h * jnp.exp(-t * deltas)[:, None, None, :]
    return h / jnp.sum(jnp.abs(h), axis=(0, 2), keepdims=True)


FFT_R = 128
FFT_N = FFT_R * FFT_R
FFT_COLS = 2048


def _dft_parts(n):
    ang = 2.0 * np.pi * np.outer(np.arange(n), np.arange(n)) / n
    return np.cos(ang), np.sin(ang)


def hyena_fft_constants():
    C, S = _dft_parts(FFT_R)
    h = FFT_R // 2
    m_first = np.block([[C[:, :h], S[:, :h]], [-S[:, :h], C[:, :h]]])
    m_first_real = np.concatenate([C, -S], 0)
    m_mid = np.block([[C, S], [-S, C]])
    m_mid_inv = np.block([[C, -S], [S, C]])
    m_last = np.block([[C[:h], -S[:h]], [S[:h], C[:h]]])
    ang = 2.0 * np.pi * np.outer(np.arange(FFT_R), np.arange(FFT_R)) / FFT_N
    tw = np.stack([np.cos(ang), np.sin(ang)], -1)
    bf = lambda a: jnp.asarray(a, BF16)
    return dict(first=bf(m_first), first_real=bf(m_first_real), mid=bf(m_mid), mid_inv=bf(m_mid_inv),
                last=bf(m_last), tw=jnp.asarray(tw, F32))


def _fft_outer_kernel(m_ref, x_ref, o_ref):
    o_ref[...] = jnp.dot(m_ref[...], x_ref[...].astype(BF16), preferred_element_type=F32).astype(o_ref.dtype)


def fft_first(m, x2d):
    R, W = x2d.shape
    return pl.pallas_call(
        _fft_outer_kernel, grid=(W // FFT_COLS,),
        in_specs=[pl.BlockSpec(m.shape, lambda j: (0, 0)), pl.BlockSpec((R, FFT_COLS), lambda j: (0, j))],
        out_specs=pl.BlockSpec((m.shape[0], FFT_COLS), lambda j: (0, j)),
        out_shape=jax.ShapeDtypeStruct((m.shape[0], W), BF16),
        compiler_params=pltpu.CompilerParams(dimension_semantics=("arbitrary",)),
        name="hyena_fft_first")(m, x2d)


def _fft_mid_kernel(*refs, conv):
    if conv:
        a_ref, tw_ref, mf_ref, mi_ref, h_ref, o_ref = refs
    else:
        a_ref, tw_ref, mf_ref, o_ref = refs
    R = FFT_R
    tc = tw_ref[0, :, 0:1]
    ts = tw_ref[0, :, 1:2]
    ar = a_ref[0, 0].astype(F32)
    ai = a_ref[1, 0].astype(F32)
    x = jnp.concatenate([ar * tc + ai * ts, ai * tc - ar * ts], 0).astype(BF16)
    X = jnp.dot(mf_ref[...], x, preferred_element_type=F32)
    if not conv:
        o_ref[0, 0] = X[:R]
        o_ref[1, 0] = X[R:]
        return
    xr, xi = X[:R], X[R:]
    hr, hi = h_ref[0, 0], h_ref[1, 0]
    y = jnp.concatenate([xr * hr - xi * hi, xr * hi + xi * hr], 0).astype(BF16)
    Bm = jnp.dot(mi_ref[...], y, preferred_element_type=F32)
    br, bi = Bm[:R], Bm[R:]
    o_ref[0, 0] = (br * tc - bi * ts).astype(o_ref.dtype)
    o_ref[1, 0] = (bi * tc + br * ts).astype(o_ref.dtype)


def fft_mid(a, consts, h=None):
    C = a.shape[-1]
    slab = pl.BlockSpec((2, 1, FFT_R, C), lambda k: (0, k, 0, 0))
    twspec = pl.BlockSpec((1, FFT_R, 2), lambda k: (k, 0, 0))
    mspec = pl.BlockSpec((2 * FFT_R, 2 * FFT_R), lambda k: (0, 0))
    conv = h is not None
    ins = [a, consts['tw'], consts['mid']] + ([consts['mid_inv'], h] if conv else [])
    specs = [slab, twspec, mspec] + ([mspec, slab] if conv else [])
    return pl.pallas_call(
        functools.partial(_fft_mid_kernel, conv=conv), grid=(FFT_R,),
        in_specs=specs, out_specs=slab,
        out_shape=jax.ShapeDtypeStruct((2, FFT_R, FFT_R, C), BF16 if conv else F32),
        compiler_params=pltpu.CompilerParams(dimension_semantics=("arbitrary",)),
        name="hyena_fft_mid")(*ins)


def _fft_last_kernel(m_ref, b_ref, z_ref, bias_ref, gate_ref, o_ref):
    y = jnp.dot(m_ref[...], b_ref[...], preferred_element_type=F32)
    o_ref[...] = gate_ref[...] * (y + bias_ref[...] * z_ref[...])


def fft_last(m, b2d, z2d, bias_row, gate2d):
    R, W = z2d.shape
    col = lambda r: pl.BlockSpec((r, FFT_COLS), lambda j: (0, j))
    return pl.pallas_call(
        _fft_last_kernel, grid=(W // FFT_COLS,),
        in_specs=[pl.BlockSpec(m.shape, lambda j: (0, 0)), col(b2d.shape[0]), col(R),
                  pl.BlockSpec((1, FFT_COLS), lambda j: (0, 0)), col(R)],
        out_specs=col(R),
        out_shape=jax.ShapeDtypeStruct((R, W), F32),
        compiler_params=pltpu.CompilerParams(dimension_semantics=("arbitrary",)),
        name="hyena_fft_last")(m, b2d, z2d, bias_row, gate2d)


def two_sided_kernel(h_fwd, h_bwd, n):
    C = h_fwd.shape[1]
    return jnp.concatenate([h_fwd, jnp.zeros((1, C), F32), jnp.flip(h_bwd[1:], 0)], 0) * (1.0 / n)


def filter_spectrum(h_fwd, h_bwd, consts):
    C = h_fwd.shape[1]
    k = two_sided_kernel(h_fwd, h_bwd, FFT_N)
    a = fft_first(consts['first_real'], k.reshape(FFT_R, FFT_R * C))
    return fft_mid(a.reshape(2, FFT_R, FFT_R, C), consts)


def long_conv(z, gate, spec, bias, consts):
    B, L, C = z.shape
    assert B == 2 and 2 * L == FFT_N and FFT_COLS % C == 0
    z2d = z.reshape(FFT_R, FFT_R * C)
    a = fft_first(consts['first'], z2d)
    b = fft_mid(a.reshape(2, FFT_R, FFT_R, C), consts, spec)
    bias_row = jnp.tile(bias.reshape(1, C), (1, FFT_COLS // C))
    y = fft_last(consts['last'], b.reshape(2 * FFT_R, FFT_R * C), z2d, bias_row, gate.reshape(FFT_R, FFT_R * C))
    return y.reshape(B, L, C)


def dense_dft_constants(L):
    C, S = _dft_parts(2 * L)
    m_fwd = np.block([[C[:, :L], S[:, :L]], [-S[:, :L], C[:, :L]]])
    m_fwd_real = np.concatenate([C, -S], 0)
    m_inv = np.block([[C[:L], -S[:L]], [S[:L], C[:L]]])
    bf = lambda a: jnp.asarray(a, BF16)
    return bf(m_fwd), bf(m_fwd_real), bf(m_inv)


def _short_conv_kernel(mf_ref, mfr_ref, mi_ref, z_ref, k_ref, bias_ref, gate_ref, o_ref):
    n = k_ref.shape[0]
    z = z_ref[...]
    X = jnp.dot(mf_ref[...], z.astype(BF16), preferred_element_type=F32)
    Hs = jnp.dot(mfr_ref[...], k_ref[...].astype(BF16), preferred_element_type=F32)
    xr, xi, hr, hi = X[:n], X[n:], Hs[:n], Hs[n:]
    y = jnp.concatenate([xr * hr - xi * hi, xr * hi + xi * hr], 0).astype(BF16)
    o_ref[...] = gate_ref[...] * (jnp.dot(mi_ref[...], y, preferred_element_type=F32) + bias_ref[...] * z)


def short_conv(z, gate, h_fwd, h_bwd, bias):
    B, L, C = z.shape
    assert B == 2
    mf, mfr, mi = dense_dft_constants(L)
    y = pl.pallas_call(
        _short_conv_kernel,
        out_shape=jax.ShapeDtypeStruct((2 * L, C), F32),
        name="hyena_short_conv")(mf, mfr, mi, z.reshape(2 * L, C), two_sided_kernel(h_fwd, h_bwd, 2 * L),
                                 bias.reshape(1, C), gate.reshape(2 * L, C))
    return y.reshape(B, L, C)


def hyena_branch(p, with_ctx, n_ctx, conv_w, conv_b, w1, b1, w2, b2, w3, freq, hbias):
    consts = hyena_fft_constants()
    hy_w = (w1, b1, w2, b2, w3, freq)
    v, x1, x2 = jnp.split(short_conv3(p[:, n_ctx:], conv_w, conv_b), 3, axis=-1)
    filt = hyena_filters(p.shape[1] - n_ctx, *hy_w)
    z = long_conv(v, x1, filter_spectrum(filt[:, 0, 0], filt[:, 0, 1], consts), hbias[0], consts)
    o = long_conv(z, x2, filter_spectrum(filt[:, 1, 0], filt[:, 1, 1], consts), hbias[1], consts)
    if with_ctx:
        vc, xc1, xc2 = jnp.split(short_conv3(p[:, :n_ctx], conv_w, conv_b), 3, axis=-1)
        fc = hyena_filters(n_ctx, *hy_w)
        zc = short_conv(vc, xc1, fc[:, 0, 0], fc[:, 0, 1], hbias[0])
        oc = short_conv(zc, xc2, fc[:, 1, 0], fc[:, 1, 1], hbias[1])
    else:
        oc = jnp.zeros((o.shape[0], n_ctx, o.shape[2]), o.dtype)
    return o, oc


ROUTER_PAD = 128


def _merge_kernel(of_ref, ob_ref, r_ref, om_ref, ohl_ref, ohc_ref, g_ref, x_ref, mod_ref,
                  wg_ref, wm_ref, wh_ref, wo_ref, gn_ref, l1g_ref, l1b_ref, rw_ref, rb_ref,
                  x1_ref, h2_ref, lg_ref, *, n_ctx_tiles):
    o = of_ref[0] + ob_ref[0]
    parts = [_rms(o[:, h * GLA_DV:(h + 1) * GLA_DV], gn_ref[...]) for h in range(GLA_HEADS)]
    og = (jnp.concatenate(parts, axis=1) * jax.nn.silu(r_ref[0].astype(F32))).astype(BF16)
    oh = jnp.where(pl.program_id(1) < n_ctx_tiles, ohc_ref[0], ohl_ref[0]).astype(BF16)
    D = D_MODEL
    g = g_ref[0]
    y = (jax.nn.sigmoid(g[:, :D].astype(F32)) * jnp.dot(og, wg_ref[...], preferred_element_type=F32)
         + jax.nn.sigmoid(g[:, D:2 * D].astype(F32)) * jnp.dot(om_ref[0], wm_ref[...], preferred_element_type=F32)
         + jax.nn.sigmoid(g[:, 2 * D:].astype(F32)) * jnp.dot(oh, wh_ref[...], preferred_element_type=F32))
    ym = jnp.dot(y.astype(BF16), wo_ref[...], preferred_element_type=F32)
    gate1, sh2, sc2 = mod_ref[0, 0, 0:1], mod_ref[0, 0, 1:2], mod_ref[0, 0, 2:3]
    x1 = _layer_norm(DEEPNORM_ALPHA * x_ref[0] + gate1 * ym) * l1g_ref[...] + l1b_ref[...]
    x1_ref[0] = x1
    h2 = (_layer_norm(x1) * (1.0 + sc2) + sh2).astype(BF16)
    h2_ref[0] = h2
    lg_ref[0] = jnp.dot(h2, rw_ref[...], preferred_element_type=F32) + rb_ref[...]


def merge_block(o_f, o_b, P, o_mla, o_hy_lat, o_hy_ctx, xa, mod, w_br_gla, w_br_mla, w_br_hy, w_out, gla_norm,
                ln1_g, ln1_b, router_w, router_b, n_ctx):
    B, T, D = xa.shape
    NC = n_ctx // ROW_TILE
    bf = lambda a: a.astype(BF16)
    row = lambda a: a.reshape(1, -1)
    rw = jnp.pad(router_w, ((0, 0), (0, ROUTER_PAD - N_EXPERTS))).astype(BF16)
    rb = jnp.pad(router_b, (0, ROUTER_PAD - N_EXPERTS)).reshape(1, -1)
    full = lambda a: pl.BlockSpec(a.shape, lambda b, t: (0,) * a.ndim)
    tile = lambda w: pl.BlockSpec((1, ROW_TILE, w), lambda b, t: (b, t, 0))
    lat = pl.BlockSpec((1, ROW_TILE, BRANCH_W), lambda b, t: (b, jnp.maximum(t - NC, 0), 0))
    ctx = pl.BlockSpec((1, ROW_TILE, BRANCH_W), lambda b, t: (b, jnp.minimum(t, NC - 1), 0))
    ws = [bf(w_br_gla), bf(w_br_mla), bf(w_br_hy), bf(w_out), row(gla_norm), row(ln1_g), row(ln1_b), rw, rb]
    return pl.pallas_call(
        functools.partial(_merge_kernel, n_ctx_tiles=NC), grid=(B, T // ROW_TILE),
        in_specs=[tile(BRANCH_W), tile(BRANCH_W), pcol('r'), tile(BRANCH_W), lat, ctx, pcol('gate'), tile(D),
                  pl.BlockSpec((1, 1, 3, D), _tile_kind(NC))] + [full(w) for w in ws],
        out_specs=[tile(D), tile(D), tile(ROUTER_PAD)],
        out_shape=[jax.ShapeDtypeStruct((B, T, D), F32), jax.ShapeDtypeStruct((B, T, D), BF16),
                   jax.ShapeDtypeStruct((B, T, ROUTER_PAD), F32)],
        compiler_params=pltpu.CompilerParams(dimension_semantics=("arbitrary", "arbitrary"),
                                             vmem_limit_bytes=48 * 1024 * 1024),
        name="merge_block")(o_f, o_b, P, o_mla, o_hy_lat, o_hy_ctx, P, xa, mod, *ws)


def kernel(x, c, ctx, c_ctx, ada_w, ada_b, w_in, gla_wa2_f, gla_ba_f, gla_wa2_b, gla_ba_b, gla_norm,
           mla_q_norm, mla_w_uq, mla_kv_norm, mla_w_ukv, hy_conv_w, hy_conv_b, hy_w1, hy_b1, hy_w2, hy_b2,
           hy_w3, hy_freq, hy_bias, w_br_gla, w_br_mla, w_br_hy, w_out, ln1_g, ln1_b, ln2_g, ln2_b,
           router_w, router_b, moe_w1, moe_b1, moe_w2, moe_b2):
    B, L, D = x.shape
    CL = ctx.shape[1]
    T = CL + L
    tables = rope_tables(L, CL)
    xa = jnp.concatenate([ctx, x], axis=1)
    cond = jnp.concatenate([jax.nn.silu(c), jax.nn.silu(c_ctx)[None], jnp.zeros((8 - B - 1, D), F32)], 0)
    for l in range(DEPTH):
        last = l == DEPTH - 1
        ada = pmm(cond, ada_w[l])[:B + 1] + ada_b[l]
        ada = jnp.stack([ada[:B], jnp.broadcast_to(ada[B:], (B, 6 * D))], 1).reshape(B, 2, 6, D)
        P = ln_mod_proj(xa, ada[:, :, 0:2], permute_w_in(w_in[l]), CL)
        o_f, o_b = gla_scan(P, gla_wa2_f[l], gla_ba_f[l], gla_wa2_b[l], gla_ba_b[l], CL)
        q, k, v = mla_prep(P, tables, mla_weights(mla_w_uq[l], mla_w_ukv[l]), mla_q_norm[l], mla_kv_norm[l])
        o_mla = mla_attention(q, k, v, CL)
        hy = P[:, :, COL['hy']:COL['hy'] + N_HY].astype(F32)
        o_hy, o_hy_ctx = hyena_branch(hy, not last, CL, hy_conv_w[l], hy_conv_b[l], hy_w1[l], hy_b1[l], hy_w2[l],
                                      hy_b2[l], hy_w3[l], hy_freq[l], hy_bias[l])
        x1, h2, logits = merge_block(o_f, o_b, P, o_mla, o_hy, o_hy_ctx, xa, ada[:, :, 2:5],
                                     w_br_gla[l], w_br_mla[l], w_br_hy[l], w_out[l], gla_norm[l],
                                     ln1_g[l], ln1_b[l], router_w[l], router_b[l], CL)
        logits = logits[:, :, :N_EXPERTS]
        moe_w = (moe_w1[l], moe_b1[l], moe_w2[l], moe_b2[l])
        if last:
            f = moe_ffn(h2[:, CL:].reshape(B * L, D), logits[:, CL:].reshape(B * L, N_EXPERTS), *moe_w)
            g2 = ada[:, 0, 5][:, None, :]
            return layer_norm(DEEPNORM_ALPHA * x1[:, CL:] + g2 * f.reshape(B, L, D), ln2_g[l], ln2_b[l])
        f = moe_ffn(h2.reshape(B * T, D), logits.reshape(B * T, N_EXPERTS), *moe_w).reshape(B, T, D)
        is_ctx = (jnp.arange(T) < CL)[None, :, None]
        g2 = jnp.where(is_ctx, ada[:, 1, 5][:, None, :], ada[:, 0, 5][:, None, :])
        xa = layer_norm(DEEPNORM_ALPHA * x1 + g2 * f, ln2_g[l], ln2_b[l])
```

```python
import functools
import math
import jax
import jax.numpy as jnp
from jax import lax
import numpy as np
from jax.experimental import pallas as pl
from jax.experimental.pallas import tpu as pltpu

D_MODEL = 1024
DEPTH = 2
GRID_W = 64
BRANCH_W = D_MODEL // 2
N_BRANCH = 3
GLA_HEADS = 4
GLA_DV = BRANCH_W // GLA_HEADS
GLA_DK = GLA_DV // 2
GLA_RANK = 16
GLA_TAU = 16.0
GLA_CHUNK = 16
MLA_HEADS = 8
MLA_DV = BRANCH_W // MLA_HEADS
MLA_NOPE = MLA_DV
MLA_ROPE = MLA_NOPE // 2
MLA_Q_RANK = 3 * D_MODEL // 8
MLA_KV_RANK = D_MODEL // 4
ROPE_BASE = 10000.0
Q_BLOCK = 128
HY_W = BRANCH_W
HY_ORDER = 2
HY_DIRS = 2
HY_BANDS = 16
HY_EMB = 2 * HY_BANDS + 1
HY_FFN = 64
HY_DECAY_TARGET = 1e-2
HY_FAST_PCT = 0.3
HY_SLOW_PCT = 1.5
N_EXPERTS = 32
TOP_K = 4
D_FF = D_MODEL
SWIGLU_LIMIT = 7.0
SWIGLU_ALPHA = 1.702
MOE_BLOCK = 128
LN_EPS = 1e-5
RMS_EPS = 1e-6
DEEPNORM_ALPHA = (2 * DEPTH) ** 0.25
IN_SPLITS = (
    GLA_HEADS * GLA_DK, GLA_HEADS * GLA_DV, GLA_RANK, GLA_RANK, MLA_KV_RANK, MLA_ROPE,
    GLA_HEADS * GLA_DK, GLA_HEADS * GLA_DV, MLA_Q_RANK, 3 * HY_W, N_BRANCH * D_MODEL,
)
N_KEY_GROUPS = 6
KEY_COLS = sum(IN_SPLITS[:N_KEY_GROUPS])
IN_OFFSETS = tuple(int(o) for o in np.cumsum(IN_SPLITS)[:-1])
IN_TOTAL = sum(IN_SPLITS)
F32 = jnp.float32
BF16 = jnp.bfloat16
U32 = jnp.uint32


def _mm_kernel(a_ref, w_ref, o_ref):
    o_ref[...] = jnp.dot(a_ref[...], w_ref[...], preferred_element_type=F32)


def _pick(n, cands):
    for c in cands:
        if n % c == 0:
            return c
    return n


def pmm(a, w):
    lead = a.shape[:-1]
    K = a.shape[-1]
    N = w.shape[-1]
    a2 = a.reshape(-1, K).astype(BF16)
    M = a2.shape[0]
    N0 = N
    if N % 128:
        N = -(-N // 256) * 256
        w = jnp.pad(w, ((0, 0), (0, N - N0)))
    tm = _pick(M, (512, 256, 128, 64, 32, 16, 8))
    tn = _pick(N, (512, 256, 128))
    out = pl.pallas_call(
        _mm_kernel,
        grid=(M // tm, N // tn),
        in_specs=[pl.BlockSpec((tm, K), lambda i, j: (i, 0)),
                  pl.BlockSpec((K, tn), lambda i, j: (0, j))],
        out_specs=pl.BlockSpec((tm, tn), lambda i, j: (i, j)),
        out_shape=jax.ShapeDtypeStruct((M, N), F32),
    )(a2, w.astype(BF16))
    return out[:, :N0].reshape(lead + (N0,))


ROW_TILE = 256
HEAD_PAD = 128
N_GATE = N_BRANCH * D_MODEL
N_HY = 3 * HY_W
COL = dict(gate=0, hy=N_GATE, qa=N_GATE + N_HY)
COL['misc'] = COL['qa'] + MLA_Q_RANK
COL['v'] = COL['misc'] + HEAD_PAD
COL['r'] = COL['v'] + GLA_HEADS * GLA_DV
COL['k'] = COL['r'] + GLA_HEADS * GLA_DV
COL['q'] = COL['k'] + GLA_HEADS * GLA_DK
COL['kva'] = COL['q'] + GLA_HEADS * GLA_DK
P_COLS = COL['kva'] + MLA_KV_RANK
WIDTH = dict(gate=N_GATE, hy=N_HY, qa=MLA_Q_RANK, misc=HEAD_PAD, v=GLA_HEADS * GLA_DV, r=GLA_HEADS * GLA_DV,
             k=GLA_HEADS * GLA_DK, q=GLA_HEADS * GLA_DK, kva=MLA_KV_RANK)
MISC_AF, MISC_AB, MISC_KR = 0, GLA_RANK, MLA_NOPE
P_CHUNK = 768
assert all(COL[g] % WIDTH[g] == 0 for g in COL) and P_COLS % P_CHUNK == 0


def permute_w_in(w_in):
    o = (0,) + IN_OFFSETS + (IN_TOTAL,)
    k, v, af, ab, kva, kr, q, r, qa, hy, gate = [w_in[:, o[i]:o[i + 1]] for i in range(len(IN_SPLITS))]
    z = lambda n: jnp.zeros((w_in.shape[0], n), w_in.dtype)
    misc = jnp.concatenate([af, ab, z(MISC_KR - 2 * GLA_RANK), kr, z(HEAD_PAD - MISC_KR - MLA_ROPE)], 1)
    return jnp.concatenate([gate, hy, qa, misc, v, r, k, q, kva], 1).astype(BF16)


def _layer_norm(x):
    xc = x - jnp.mean(x, axis=-1, keepdims=True)
    return xc * lax.rsqrt(jnp.mean(xc * xc, axis=-1, keepdims=True) + LN_EPS)


def _rms(x, gain):
    return x * lax.rsqrt(jnp.mean(x * x, axis=-1, keepdims=True) + RMS_EPS) * gain


def _proj_kernel(x_ref, mod_ref, w_ref, o_ref):
    h = _layer_norm(x_ref[0]) * (1.0 + mod_ref[0, 0, 1:2]) + mod_ref[0, 0, 0:1]
    hb = h.astype(BF16)
    for c in range(0, P_COLS, P_CHUNK):
        o_ref[0, :, c:c + P_CHUNK] = jnp.dot(hb, w_ref[:, c:c + P_CHUNK], preferred_element_type=F32).astype(BF16)


def _tile_kind(n_ctx_tiles):
    return lambda b, t: (b, jnp.where(t < n_ctx_tiles, 1, 0), 0, 0)


def ln_mod_proj(xa, mod, w_p, n_ctx):
    B, T, D = xa.shape
    return pl.pallas_call(
        _proj_kernel, grid=(B, T // ROW_TILE),
        in_specs=[pl.BlockSpec((1, ROW_TILE, D), lambda b, t: (b, t, 0)),
                  pl.BlockSpec((1, 1, 2, D), _tile_kind(n_ctx // ROW_TILE)),
                  pl.BlockSpec((D, P_COLS), lambda b, t: (0, 0), pipeline_mode=pl.Buffered(1))],
        out_specs=pl.BlockSpec((1, ROW_TILE, P_COLS), lambda b, t: (b, t, 0)),
        out_shape=jax.ShapeDtypeStruct((B, T, P_COLS), BF16),
        compiler_params=pltpu.CompilerParams(dimension_semantics=("arbitrary", "arbitrary"),
                                             vmem_limit_bytes=48 * 1024 * 1024),
        name="ln_mod_proj")(xa, mod, w_p)


def pcol(group, row_map=lambda b, t: t):
    w = WIDTH[group]
    return pl.BlockSpec((1, ROW_TILE, w), lambda b, t, _i=COL[group] // w: (b, row_map(b, t), _i))


ONES_LANE_EVEN = MLA_DV
ONES_LANE_ODD = 0
LOG2E = 1.4426950408889634


def rope_tables(L, n_ctx):
    rows = L // GRID_W
    row = jnp.repeat(jnp.arange(rows, dtype=F32), GRID_W)
    col = jnp.tile(jnp.arange(GRID_W, dtype=F32), rows)
    n_freq = MLA_ROPE // 4
    inv = ROPE_BASE ** (-jnp.arange(n_freq, dtype=F32) / n_freq)
    ang = jnp.concatenate([row[:, None] * inv, col[:, None] * inv], -1)
    ang = jnp.concatenate([jnp.zeros((n_ctx, ang.shape[1]), F32), ang], 0)
    cos, sin = jnp.cos(ang), jnp.sin(ang)
    T = ang.shape[0]
    half = MLA_ROPE // 2
    z = lambda n: jnp.zeros((T, n), F32)
    ct = jnp.concatenate([jnp.ones((T, MLA_NOPE), F32), cos, cos, z(HEAD_PAD - MLA_NOPE - MLA_ROPE)], 1)
    sa = jnp.concatenate([z(MLA_NOPE), -sin, z(HEAD_PAD - MLA_NOPE - half)], 1)
    sb = jnp.concatenate([z(MLA_NOPE + half), sin, z(HEAD_PAD - MLA_NOPE - MLA_ROPE)], 1)
    return ct, sa, sb


def mla_weights(w_uq, w_ukv):
    H = MLA_HEADS
    wq = w_uq.reshape(MLA_Q_RANK, H, MLA_NOPE + MLA_ROPE)
    wq = jnp.pad(wq, ((0, 0), (0, 0), (0, HEAD_PAD - MLA_NOPE - MLA_ROPE))).reshape(MLA_Q_RANK, H * HEAD_PAD)
    wkv = w_ukv.reshape(MLA_KV_RANK, H, MLA_NOPE + MLA_DV)
    wk = jnp.pad(wkv[:, :, :MLA_NOPE], ((0, 0), (0, 0), (0, HEAD_PAD - MLA_NOPE))).reshape(MLA_KV_RANK, H * HEAD_PAD)
    wv = wkv[:, :, MLA_NOPE:].reshape(MLA_KV_RANK, H // 2, 2, MLA_DV)
    z = jnp.zeros_like(wv[:, :, 0])
    wv = jnp.stack([jnp.concatenate([wv[:, :, 0], z], -1), jnp.concatenate([z, wv[:, :, 1]], -1)], 2)
    ones = np.zeros((H // 2, 2, HEAD_PAD), np.float32)
    ones[:, 0, ONES_LANE_EVEN] = 1.0
    ones[:, 1, ONES_LANE_ODD] = 1.0
    return (wq.astype(BF16), wk.astype(BF16), wv.reshape(MLA_KV_RANK, H * HEAD_PAD).astype(BF16),
            jnp.asarray(ones.reshape(1, H * HEAD_PAD)))


def _mla_prep_kernel(qa_ref, kva_ref, misc_ref, ct_ref, sa_ref, sb_ref, wq_ref, wk_ref, wv_ref, ones_ref,
                     qn_ref, kn_ref, q_ref, k_ref, v_ref):
    ct, sa, sb = ct_ref[...], sa_ref[...], sb_ref[...]
    half = MLA_ROPE // 2
    rope = lambda x: x * ct + pltpu.roll(x, HEAD_PAD - half, 1) * sa + pltpu.roll(x, half, 1) * sb
    qn = _rms(qa_ref[0].astype(F32), qn_ref[...]).astype(BF16)
    q = jnp.dot(qn, wq_ref[...], preferred_element_type=F32)
    kn = _rms(kva_ref[0].astype(F32), kn_ref[...]).astype(BF16)
    k = jnp.dot(kn, wk_ref[...], preferred_element_type=F32)
    v_ref[0] = (jnp.dot(kn, wv_ref[...], preferred_element_type=F32) + ones_ref[...]).astype(BF16)
    lane = lax.broadcasted_iota(jnp.int32, (1, HEAD_PAD), 1)
    is_kr = (lane >= MISC_KR) & (lane < MISC_KR + MLA_ROPE)
    kr = rope(jnp.where(is_kr, misc_ref[0].astype(F32), 0.0))
    qscale = (MLA_NOPE + MLA_ROPE) ** -0.5 * LOG2E
    for h in range(MLA_HEADS):
        hs = slice(h * HEAD_PAD, (h + 1) * HEAD_PAD)
        q_ref[0, :, hs] = (rope(q[:, hs]) * qscale).astype(BF16)
        k_ref[0, :, hs] = (k[:, hs] + kr).astype(BF16)


def mla_prep(P, tables, weights, q_norm, kv_norm):
    B, T, _ = P.shape
    ct, sa, sb = tables
    wq, wk, wv, ones = weights
    full = lambda a: pl.BlockSpec(a.shape, lambda b, t: (0,) * a.ndim)
    tab = pl.BlockSpec((ROW_TILE, HEAD_PAD), lambda b, t: (t, 0))
    qn, kn = q_norm.reshape(1, -1), kv_norm.reshape(1, -1)
    W = MLA_HEADS * HEAD_PAD
    out = pl.BlockSpec((1, ROW_TILE, W), lambda b, t: (b, t, 0))
    return pl.pallas_call(
        _mla_prep_kernel, grid=(B, T // ROW_TILE),
        in_specs=[pcol('qa'), pcol('kva'), pcol('misc'), tab, tab, tab, full(wq), full(wk), full(wv), full(ones),
                  full(qn), full(kn)],
        out_specs=[out, out, out],
        out_shape=[jax.ShapeDtypeStruct((B, T, W), BF16)] * 3,
        compiler_params=pltpu.CompilerParams(dimension_semantics=("arbitrary", "arbitrary")),
        name="mla_prep")(P, P, P, ct, sa, sb, wq, wk, wv, ones, qn, kn)


ATT_TQ = 256
ATT_TK = 768


def _attn_kernel(q_ref, k_ref, v_ref, o_ref, s_buf, p_buf, m_buf, a_buf, mrun, acc, *, n_ctx, tk):
    tq = q_ref.shape[1]
    T = k_ref.shape[1]
    in_ctx = pl.program_id(2) * tq < n_ctx
    lane = lax.broadcasted_iota(jnp.int32, (tq, HEAD_PAD), 1)
    hs = lambda h: slice(h * HEAD_PAD, (h + 1) * HEAD_PAD)

    def stage_a(c, slot, size):
        start = pl.multiple_of(c * size, size)
        for h in range(2):
            k = k_ref[0, pl.ds(start, size), hs(h)]
            s = lax.dot_general(q_ref[0, :, hs(h)], k, (((1,), (1,)), ((), ())), preferred_element_type=F32)
            s_buf[slot, h, :, :size] = s
            m_prev = mrun[h]
            m_new = jnp.maximum(m_prev, jnp.max(s, axis=-1, keepdims=True))
            mrun[h] = m_new
            m_buf[slot, h] = m_new
            a_buf[slot, h] = jnp.exp2(m_prev - m_new)

    def stage_b(slot, size):
        for h in range(2):
            p_buf[slot, h, :, :size] = jnp.exp2(s_buf[slot, h, :, :size] - m_buf[slot, h]).astype(BF16)

    def stage_c(c, slot, size):
        start = pl.multiple_of(c * size, size)
        for h in range(2):
            v = v_ref[0, pl.ds(start, size), hs(h)]
            acc[h] = a_buf[slot, h] * acc[h] + jnp.dot(p_buf[slot, h, :, :size], v, preferred_element_type=F32)

    def run(n, size):
        for h in range(2):
            mrun[h] = jnp.full((tq, 1), -jnp.inf, F32)
            acc[h] = jnp.zeros((tq, HEAD_PAD), F32)
        if n == 1:
            stage_a(0, 0, size)
            stage_b(0, size)
            stage_c(0, 0, size)
        else:
            stage_a(0, 0, size)
            stage_b(0, size)
            stage_a(1, 1, size)

            def step(c, par):
                stage_c(c, par, size)
                stage_b(1 - par, size)
                stage_a(c + 2, par, size)

            def body(j, carry):
                step(2 * j, 0)
                step(2 * j + 1, 1)
                return carry

            lax.fori_loop(0, (n - 2) // 2, body, 0)
            if (n - 2) % 2:
                step(n - 3, (n - 3) % 2)
            stage_c(n - 2, (n - 2) % 2, size)
            stage_b((n - 1) % 2, size)
            stage_c(n - 1, (n - 1) % 2, size)
        a0, a1 = acc[0], acc[1]
        l0 = a0[:, ONES_LANE_EVEN:ONES_LANE_EVEN + 1]
        l1 = a1[:, ONES_LANE_ODD:ONES_LANE_ODD + 1]
        o_ref[0] = jnp.where(lane < MLA_DV, a0 / l0, a1 / l1).astype(o_ref.dtype)

    @pl.when(in_ctx)
    def _():
        run(1, n_ctx)

    @pl.when(jnp.logical_not(in_ctx))
    def _():
        run(T // tk, tk)


def mla_attention(q, k, v, n_ctx):
    B, T, _ = q.shape
    assert T % ATT_TQ == 0 and T % ATT_TK == 0 and n_ctx % ATT_TQ == 0 and n_ctx <= ATT_TK
    kern = functools.partial(_attn_kernel, n_ctx=n_ctx, tk=ATT_TK)
    return pl.pallas_call(
        kern,
        grid=(B, MLA_HEADS // 2, T // ATT_TQ),
        in_specs=[pl.BlockSpec((1, ATT_TQ, 2 * HEAD_PAD), lambda b, h, i: (b, i, h)),
                  pl.BlockSpec((1, T, 2 * HEAD_PAD), lambda b, h, i: (b, 0, h)),
                  pl.BlockSpec((1, T, 2 * HEAD_PAD), lambda b, h, i: (b, 0, h))],
        out_specs=pl.BlockSpec((1, ATT_TQ, 2 * MLA_DV), lambda b, h, i: (b, i, h)),
        out_shape=jax.ShapeDtypeStruct((B, T, MLA_HEADS * MLA_DV), BF16),
        scratch_shapes=[pltpu.VMEM((2, 2, ATT_TQ, ATT_TK), F32), pltpu.VMEM((2, 2, ATT_TQ, ATT_TK), BF16),
                        pltpu.VMEM((2, 2, ATT_TQ, 1), F32), pltpu.VMEM((2, 2, ATT_TQ, 1), F32),
                        pltpu.VMEM((2, ATT_TQ, 1), F32), pltpu.VMEM((2, ATT_TQ, HEAD_PAD), F32)],
        compiler_params=pltpu.CompilerParams(
            dimension_semantics=("arbitrary", "arbitrary", "arbitrary"),
            vmem_limit_bytes=56 * 1024 * 1024),
        name="mla_attention",
    )(q, k, v)


MOE_TM = 256
HALF_D = D_MODEL // 2
HI_MASK = 0xFFFF0000
N_ROW_DMA = ROW_TILE * TOP_K


def pack_bf16_pairs(x):
    u = pltpu.bitcast(x.astype(BF16).astype(F32), U32)
    return (u[:, :HALF_D] >> 16) | (u[:, HALF_D:] & jnp.uint32(HI_MASK))


def unpack_bf16_pairs(w):
    lo = pltpu.bitcast(w << 16, F32)
    hi = pltpu.bitcast(w & jnp.uint32(HI_MASK), F32)
    return jnp.concatenate([lo, hi], axis=1).astype(BF16)


def moe_route(logits):
    N = logits.shape[0]
    top_val, top_idx = lax.top_k(logits, TOP_K)
    gates = jax.nn.softmax(top_val, axis=-1)
    NK = N * TOP_K
    flat_e = top_idx.reshape(NK)
    onehot = (flat_e[:, None] == jnp.arange(N_EXPERTS, dtype=flat_e.dtype)[None, :]).astype(jnp.int32)
    csum = jnp.cumsum(onehot, axis=0)
    rank = jnp.take_along_axis(csum, flat_e[:, None], axis=1)[:, 0] - 1
    counts = csum[-1]
    padded = (counts + MOE_TM - 1) // MOE_TM * MOE_TM
    pad_end = jnp.cumsum(padded)
    dest = ((pad_end - padded)[flat_e] + rank).astype(jnp.int32)
    n_blocks = NK // MOE_TM + N_EXPERTS
    blk_start = jnp.arange(n_blocks, dtype=jnp.int32) * MOE_TM
    block_e = jnp.minimum(jnp.searchsorted(pad_end, blk_start, side='right'), N_EXPERTS - 1).astype(jnp.int32)
    first = jnp.concatenate([jnp.ones((1,), jnp.int32), (block_e[1:] != block_e[:-1]).astype(jnp.int32)])
    n_used = (pad_end[-1:] // MOE_TM).astype(jnp.int32)
    return dest, gates, block_e, first, n_used


def _dispatch_kernel(dest_ref, h_ref, xb_in_ref, xb_ref, sem):
    del xb_in_ref

    def start(r, c):
        for k in range(TOP_K):
            pltpu.make_async_copy(h_ref.at[0, pl.ds(r, 1)], xb_ref.at[pl.ds(dest_ref[r * TOP_K + k], 1)],
                                  sem).start()
        return c

    lax.fori_loop(0, ROW_TILE, start, 0, unroll=4)
    for k in range(TOP_K):
        pltpu.make_async_copy(h_ref.at[0], xb_ref.at[pl.ds(0, ROW_TILE)], sem).wait()


def moe_dispatch(h2p, dest, n_rows, t0, nt):
    B = h2p.shape[0]
    xb0 = jnp.zeros((n_rows, HALF_D), U32)
    return pl.pallas_call(
        _dispatch_kernel, grid=(B, nt),
        in_specs=[pl.BlockSpec((N_ROW_DMA,), lambda b, t: (b * nt + t,), memory_space=pltpu.SMEM),
                  pl.BlockSpec((1, ROW_TILE, HALF_D), lambda b, t: (b, t + t0, 0)),
                  pl.BlockSpec(memory_space=pl.ANY)],
        out_specs=pl.BlockSpec(memory_space=pl.ANY),
        out_shape=jax.ShapeDtypeStruct((n_rows, HALF_D), U32),
        scratch_shapes=[pltpu.SemaphoreType.DMA],
        input_output_aliases={2: 0},
        compiler_params=pltpu.CompilerParams(dimension_semantics=("arbitrary", "arbitrary")),
        name="moe_dispatch")(dest, h2p, xb0)


def _moe_kernel(be_ref, first_ref, nb_ref, x_ref, w1_ref, b1_ref, w2_ref, b2_ref, o_ref, w1_bf, w2_bf):
    i = pl.program_id(0)

    @pl.when(jnp.logical_and(i < nb_ref[0], first_ref[i] == 1))
    def _():
        w1_bf[...] = w1_ref[0].astype(BF16)
        w2_bf[...] = w2_ref[0].astype(BF16)

    @pl.when(i < nb_ref[0])
    def _():
        x = unpack_bf16_pairs(x_ref[...])
        hcat = jnp.dot(x, w1_bf[...], preferred_element_type=F32) + b1_ref[0]
        glu = jnp.minimum(hcat[:, :D_FF], SWIGLU_LIMIT)
        lin = jnp.clip(hcat[:, D_FF:], -SWIGLU_LIMIT, SWIGLU_LIMIT)
        act = glu * jax.nn.sigmoid(SWIGLU_ALPHA * glu) * (lin + 1)
        o_ref[...] = jnp.dot(act.astype(BF16), w2_bf[...], preferred_element_type=F32) + b2_ref[0]

    @pl.when(i >= nb_ref[0])
    def _():
        o_ref[...] = jnp.zeros_like(o_ref)


def moe_experts(xb, block_e, first, n_used, w1, b1, w2, b2):
    R = xb.shape[0]
    D = D_MODEL
    n_blocks = R // MOE_TM
    grid_spec = pltpu.PrefetchScalarGridSpec(
        num_scalar_prefetch=3,
        grid=(n_blocks,),
        in_specs=[pl.BlockSpec((MOE_TM, HALF_D), lambda i, be, fi, nb: (i, 0)),
                  pl.BlockSpec((1, D, 2 * D_FF), lambda i, be, fi, nb: (be[i], 0, 0)),
                  pl.BlockSpec((1, 1, 2 * D_FF), lambda i, be, fi, nb: (be[i], 0, 0)),
                  pl.BlockSpec((1, D_FF, D), lambda i, be, fi, nb: (be[i], 0, 0)),
                  pl.BlockSpec((1, 1, D), lambda i, be, fi, nb: (be[i], 0, 0))],
        out_specs=pl.BlockSpec((MOE_TM, D), lambda i, be, fi, nb: (i, 0)),
        scratch_shapes=[pltpu.VMEM((D, 2 * D_FF), BF16), pltpu.VMEM((D_FF, D), BF16)],
    )
    return pl.pallas_call(
        _moe_kernel,
        grid_spec=grid_spec,
        out_shape=jax.ShapeDtypeStruct((R, D), F32),
        compiler_params=pltpu.CompilerParams(
            dimension_semantics=("arbitrary",), vmem_limit_bytes=56 * 1024 * 1024),
        name="moe_experts",
    )(block_e, first, n_used, xb, w1, b1[:, None, :], w2, b2[:, None, :])


def _combine_kernel(dest_ref, x_ref, gate_ref, mod_ref, lg_ref, lb_ref, yb_ref, o_ref, buf, sem):
    def start(r, c):
        for k in range(TOP_K):
            pltpu.make_async_copy(yb_ref.at[pl.ds(dest_ref[r * TOP_K + k], 1)], buf.at[k, pl.ds(r, 1)], sem).start()
        return c

    lax.fori_loop(0, ROW_TILE, start, 0, unroll=4)
    for k in range(TOP_K):
        pltpu.make_async_copy(yb_ref.at[pl.ds(0, ROW_TILE)], buf.at[k], sem).wait()
    g = gate_ref[...]
    f = g[:, 0:1] * buf[0]
    for k in range(1, TOP_K):
        f = f + g[:, k:k + 1] * buf[k]
    o_ref[0] = _layer_norm(DEEPNORM_ALPHA * x_ref[0] + mod_ref[0, 0, 0:1] * f) * lg_ref[...] + lb_ref[...]


def moe_combine(yb, dest, gates, x1, mod, ln_g, ln_b, t0, nt, n_ctx_tiles):
    B, T, D = x1.shape
    row = lambda a: a.reshape(1, -1)
    full = lambda a: pl.BlockSpec(a.shape, lambda b, t: (0,) * a.ndim)
    return pl.pallas_call(
        _combine_kernel, grid=(B, nt),
        in_specs=[pl.BlockSpec((N_ROW_DMA,), lambda b, t: (b * nt + t,), memory_space=pltpu.SMEM),
                  pl.BlockSpec((1, ROW_TILE, D), lambda b, t: (b, t + t0, 0)),
                  pl.BlockSpec((ROW_TILE, TOP_K), lambda b, t: (b * nt + t, 0)),
                  pl.BlockSpec((1, 1, 1, D), lambda b, t: (b, jnp.where(t + t0 < n_ctx_tiles, 1, 0), 0, 0)),
                  full(row(ln_g)), full(row(ln_b)),
                  pl.BlockSpec(memory_space=pl.ANY)],
        out_specs=pl.BlockSpec((1, ROW_TILE, D), lambda b, t: (b, t, 0)),
        out_shape=jax.ShapeDtypeStruct((B, nt * ROW_TILE, D), F32),
        scratch_shapes=[pltpu.VMEM((TOP_K, ROW_TILE, D), F32), pltpu.SemaphoreType.DMA],
        compiler_params=pltpu.CompilerParams(dimension_semantics=("arbitrary", "arbitrary")),
        name="moe_combine")(dest, x1, gates, mod, row(ln_g), row(ln_b), yb)


def moe_ffn_residual(h2p, logits, x1, gate2, w1, b1, w2, b2, ln_g, ln_b, t0, nt, n_ctx_tiles):
    B = h2p.shape[0]
    N = B * nt * ROW_TILE
    dest, gates, block_e, first, n_used = moe_route(logits.reshape(N, N_EXPERTS))
    n_rows = (N * TOP_K // MOE_TM + N_EXPERTS) * MOE_TM
    xb = moe_dispatch(h2p, dest, n_rows, t0, nt)
    yb = moe_experts(xb, block_e, first, n_used, w1, b1, w2, b2)
    return moe_combine(yb, dest, gates, x1, gate2, ln_g, ln_b, t0, nt, n_ctx_tiles)


GLA_TILE = 256
GLA_BASE = 16
GLA_LEVELS = (16, 32, 64, 128)
N_SEL = 2 * (len(GLA_LEVELS) + 1)


def gla_constants():
    n = GLA_TILE
    i = np.arange(n)[:, None]
    t = np.arange(n)[None, :]
    sizes = GLA_LEVELS + (n,)
    P = [((t <= i) & (t // s == i // s)) for s in sizes]
    S = [((t > i) & (t // s == i // s)) for s in sizes]
    sel_f = np.concatenate(P + S, 0)
    sel_b = np.concatenate([p.T for p in P] + [s.T for s in S], 0)
    lv = [((i // (2 * s) == t // (2 * s)) & ((i // s) % 2 == 1) & ((t // s) % 2 == 0)) for s in GLA_LEVELS]
    dg = (i // GLA_BASE == t // GLA_BASE) & (t <= i)
    msk_f = np.stack(lv + [dg], 0)
    msk_b = np.stack([m.T for m in lv] + [dg.T], 0)
    return (jnp.asarray(np.stack([sel_f, sel_b], 0), BF16), jnp.asarray(np.stack([msk_f, msk_b], 0), F32))


def _gla_kernel(qf_ref, kf_ref, vf_ref, af_ref, qb_ref, kb_ref, vb_ref, ab_ref,
                wf_ref, bf_ref, wb_ref, bb_ref, sel_ref, msk_ref, of_ref, ob_ref, state):
    TL = GLA_TILE
    NL = len(GLA_LEVELS)

    @pl.when(pl.program_id(1) == 0)
    def _():
        state[...] = jnp.zeros_like(state)

    lane = lax.broadcasted_iota(jnp.int32, (1, 2 * GLA_DK), 1)
    nt = lambda a, b: lax.dot_general(a, b, (((1,), (1,)), ((), ())), preferred_element_type=F32)
    dirs = ((qf_ref, kf_ref, vf_ref, af_ref, wf_ref, bf_ref, of_ref, TL - 1),
            (qb_ref, kb_ref, vb_ref, ab_ref, wb_ref, bb_ref, ob_ref, 0))
    for d, (q_ref, k_ref, v_ref, a_ref, w_ref, b_ref, o_ref, last) in enumerate(dirs):
        x = jnp.dot(a_ref[0], w_ref[...], preferred_element_type=F32) + b_ref[...]
        g = jax.nn.log_sigmoid(x) * (1.0 / GLA_TAU)
        g1 = g.astype(BF16)
        g2 = (g - g1.astype(F32)).astype(BF16)
        sel = sel_ref[d]
        ps = jnp.dot(sel, g1, preferred_element_type=F32) + jnp.dot(sel, g2, preferred_element_type=F32)
        seg = lambda j: ps[j * TL:(j + 1) * TL]
        q = q_ref[0].astype(F32) * (GLA_DK ** -0.5)
        k = k_ref[0].astype(F32)
        e_pre = [jnp.exp(seg(j)) for j in range(NL + 1)]
        e_suf = [jnp.exp(seg(NL + 1 + j)) for j in range(NL + 1)]
        qh = [(q * e).astype(BF16) for e in e_pre]
        kh = [(k * e).astype(BF16) for e in e_suf]
        kd = (k * jnp.exp(-seg(0))).astype(BF16)
        dec = e_pre[NL][last:last + 1]
        for h in range(GLA_HEADS):
            pr = slice((h // 2) * 2 * GLA_DK, (h // 2 + 1) * 2 * GLA_DK)
            mine = (lane // GLA_DK) == (h % 2)
            qm = [jnp.where(mine, t[:, pr], 0) for t in qh[:NL]]
            sc = msk_ref[d, NL] * nt(qm[0], kd[:, pr])
            for j in range(NL):
                sc = sc + msk_ref[d, j] * nt(qm[j], kh[j][:, pr])
            v = v_ref[0, :, h * GLA_DV:(h + 1) * GLA_DV].astype(BF16)
            st = state[d, h]
            o = jnp.dot(sc.astype(BF16), v, preferred_element_type=F32) + nt(qh[NL][:, pr], st.astype(BF16))
            o_ref[0, :, h * GLA_DV:(h + 1) * GLA_DV] = o
            upd = lax.dot_general(v, kh[NL][:, pr], (((0,), (0,)), ((), ())), preferred_element_type=F32)
            state[d, h] = jnp.where(mine, dec[:, pr] * st + upd, 0)


def gla_scan(P, wa2_f, ba_f, wa2_b, ba_b, n_ctx):
    B, T, _ = P.shape
    assert GLA_TILE == ROW_TILE and T % ROW_TILE == 0 and n_ctx % ROW_TILE == 0
    NT, NC = T // ROW_TILE, n_ctx // ROW_TILE
    sel, msk = gla_constants()
    fwd = lambda b, n: n
    bwd = lambda b, n: jnp.where(n < NC, NC - 1 - n, NT - 1 - (n - NC))
    full = lambda a: pl.BlockSpec(a.shape, lambda b, n: (0,) * a.ndim)
    HK, HV = GLA_HEADS * GLA_DK, GLA_HEADS * GLA_DV
    pad = lambda w, off: jnp.zeros((HEAD_PAD, HK), F32).at[off:off + GLA_RANK].set(w).astype(BF16)
    wf, wb = pad(wa2_f, MISC_AF), pad(wa2_b, MISC_AB)
    ba_f2, ba_b2 = ba_f.reshape(1, HK), ba_b.reshape(1, HK)
    out = lambda rm: pl.BlockSpec((1, ROW_TILE, HV), lambda b, n: (b, rm(b, n), 0))
    return pl.pallas_call(
        _gla_kernel,
        grid=(B, NT),
        in_specs=[pcol('q', fwd), pcol('k', fwd), pcol('v', fwd), pcol('misc', fwd),
                  pcol('q', bwd), pcol('k', bwd), pcol('v', bwd), pcol('misc', bwd),
                  full(wf), full(ba_f2), full(wb), full(ba_b2), full(sel), full(msk)],
        out_specs=[out(fwd), out(bwd)],
        out_shape=[jax.ShapeDtypeStruct((B, T, HV), F32)] * 2,
        scratch_shapes=[pltpu.VMEM((2, GLA_HEADS, GLA_DV, 2 * GLA_DK), F32)],
        compiler_params=pltpu.CompilerParams(dimension_semantics=("arbitrary", "arbitrary"),
                                             vmem_limit_bytes=48 * 1024 * 1024),
        name="gla_scan",
    )(P, P, P, P, P, P, P, P, wf, ba_f2, wb, ba_b2, sel, msk)


def short_conv3(x, w, b):
    xp = jnp.pad(x, ((0, 0), (1, 1), (0, 0)))
    return xp[:, :-2] * w[0] + xp[:, 1:-1] * w[1] + xp[:, 2:] * w[2] + b


def hyena_filters(L, w1, b1, w2, b2, w3, freq):
    t = jnp.linspace(0.0, 1.0, L, dtype=F32)[:, None]
    w = 2 * math.pi * jnp.arange(L, dtype=F32)[:, None] / L
    f = jnp.linspace(1e-4, HY_BANDS - 1, HY_BANDS, dtype=F32)
    z = jnp.concatenate([t, jnp.cos(f * w), -jnp.sin(f * w)], -1)
    fr = freq.astype(F32)
    h = jnp.sin(fr * (z @ w1.astype(F32) + b1.astype(F32)))
    h = jnp.sin(fr * (h @ w2.astype(F32) + b2.astype(F32)))
    h = (h @ w3.astype(F32)).reshape(L, HY_ORDER, HY_DIRS, HY_W)
    deltas = jnp.abs(jnp.linspace(math.log(HY_DECAY_TARGET) / HY_SLOW_PCT,
                                  math.log(HY_DECAY_TARGET) / HY_FAST_PCT, HY_W, dtype=F32))
    h = h * jnp.exp(-t * deltas)[:, None, None, :]
    return h / jnp.sum(jnp.abs(h), axis=(0, 2), keepdims=True)


FFT_R = 128
FFT_N = FFT_R * FFT_R
FFT_COLS = 2048


def _dft_parts(n):
    ang = 2.0 * np.pi * np.outer(np.arange(n), np.arange(n)) / n
    return np.cos(ang), np.sin(ang)


def hyena_fft_constants():
    C, S = _dft_parts(FFT_R)
    h = FFT_R // 2
    m_first = np.block([[C[:, :h], S[:, :h]], [-S[:, :h], C[:, :h]]])
    m_first_real = np.concatenate([C, -S], 0)
    m_mid = np.block([[C, S], [-S, C]])
    m_mid_inv = np.block([[C, -S], [S, C]])
    m_last = np.block([[C[:h], -S[:h]], [S[:h], C[:h]]])
    ang = 2.0 * np.pi * np.outer(np.arange(FFT_R), np.arange(FFT_R)) / FFT_N
    tw = np.stack([np.cos(ang), np.sin(ang)], -1)
    bf = lambda a: jnp.asarray(a, BF16)
    return dict(first=bf(m_first), first_real=bf(m_first_real), mid=bf(m_mid), mid_inv=bf(m_mid_inv),
                last=bf(m_last), tw=jnp.asarray(tw, F32))


def _fft_outer_kernel(m_ref, x_ref, o_ref):
    o_ref[...] = jnp.dot(m_ref[...], x_ref[...].astype(BF16), preferred_element_type=F32).astype(o_ref.dtype)


def fft_first(m, x2d):
    R, W = x2d.shape
    return pl.pallas_call(
        _fft_outer_kernel, grid=(W // FFT_COLS,),
        in_specs=[pl.BlockSpec(m.shape, lambda j: (0, 0)), pl.BlockSpec((R, FFT_COLS), lambda j: (0, j))],
        out_specs=pl.BlockSpec((m.shape[0], FFT_COLS), lambda j: (0, j)),
        out_shape=jax.ShapeDtypeStruct((m.shape[0], W), BF16),
        compiler_params=pltpu.CompilerParams(dimension_semantics=("arbitrary",)),
        name="hyena_fft_first")(m, x2d)


def _fft_mid_kernel(*refs, conv):
    if conv:
        a_ref, tw_ref, mf_ref, mi_ref, h_ref, o_ref = refs
    else:
        a_ref, tw_ref, mf_ref, o_ref = refs
    R = FFT_R
    tc = tw_ref[0, :, 0:1]
    ts = tw_ref[0, :, 1:2]
    ar = a_ref[0, 0].astype(F32)
    ai = a_ref[1, 0].astype(F32)
    x = jnp.concatenate([ar * tc + ai * ts, ai * tc - ar * ts], 0).astype(BF16)
    X = jnp.dot(mf_ref[...], x, preferred_element_type=F32)
    if not conv:
        o_ref[0, 0] = X[:R]
        o_ref[1, 0] = X[R:]
        return
    xr, xi = X[:R], X[R:]
    hr, hi = h_ref[0, 0], h_ref[1, 0]
    y = jnp.concatenate([xr * hr - xi * hi, xr * hi + xi * hr], 0).astype(BF16)
    Bm = jnp.dot(mi_ref[...], y, preferred_element_type=F32)
    br, bi = Bm[:R], Bm[R:]
    o_ref[0, 0] = (br * tc - bi * ts).astype(o_ref.dtype)
    o_ref[1, 0] = (bi * tc + br * ts).astype(o_ref.dtype)


def fft_mid(a, consts, h=None):
    C = a.shape[-1]
    slab = pl.BlockSpec((2, 1, FFT_R, C), lambda k: (0, k, 0, 0))
    twspec = pl.BlockSpec((1, FFT_R, 2), lambda k: (k, 0, 0))
    mspec = pl.BlockSpec((2 * FFT_R, 2 * FFT_R), lambda k: (0, 0))
    conv = h is not None
    ins = [a, consts['tw'], consts['mid']] + ([consts['mid_inv'], h] if conv else [])
    specs = [slab, twspec, mspec] + ([mspec, slab] if conv else [])
    return pl.pallas_call(
        functools.partial(_fft_mid_kernel, conv=conv), grid=(FFT_R,),
        in_specs=specs, out_specs=slab,
        out_shape=jax.ShapeDtypeStruct((2, FFT_R, FFT_R, C), BF16 if conv else F32),
        compiler_params=pltpu.CompilerParams(dimension_semantics=("arbitrary",)),
        name="hyena_fft_mid")(*ins)


def _fft_last_kernel(m_ref, b_ref, z_ref, bias_ref, gate_ref, o_ref):
    y = jnp.dot(m_ref[...], b_ref[...], preferred_element_type=F32)
    o_ref[...] = gate_ref[...] * (y + bias_ref[...] * z_ref[...])


def fft_last(m, b2d, z2d, bias_row, gate2d):
    R, W = z2d.shape
    col = lambda r: pl.BlockSpec((r, FFT_COLS), lambda j: (0, j))
    return pl.pallas_call(
        _fft_last_kernel, grid=(W // FFT_COLS,),
        in_specs=[pl.BlockSpec(m.shape, lambda j: (0, 0)), col(b2d.shape[0]), col(R),
                  pl.BlockSpec((1, FFT_COLS), lambda j: (0, 0)), col(R)],
        out_specs=col(R),
        out_shape=jax.ShapeDtypeStruct((R, W), F32),
        compiler_params=pltpu.CompilerParams(dimension_semantics=("arbitrary",)),
        name="hyena_fft_last")(m, b2d, z2d, bias_row, gate2d)


def two_sided_kernel(h_fwd, h_bwd, n):
    C = h_fwd.shape[1]
    return jnp.concatenate([h_fwd, jnp.zeros((1, C), F32), jnp.flip(h_bwd[1:], 0)], 0) * (1.0 / n)


def filter_spectrum(h_fwd, h_bwd, consts):
    C = h_fwd.shape[1]
    k = two_sided_kernel(h_fwd, h_bwd, FFT_N)
    a = fft_first(consts['first_real'], k.reshape(FFT_R, FFT_R * C))
    return fft_mid(a.reshape(2, FFT_R, FFT_R, C), consts)


def long_conv(z, gate, spec, bias, consts):
    B, L, C = z.shape
    assert B == 2 and 2 * L == FFT_N and FFT_COLS % C == 0
    z2d = z.reshape(FFT_R, FFT_R * C)
    a = fft_first(consts['first'], z2d)
    b = fft_mid(a.reshape(2, FFT_R, FFT_R, C), consts, spec)
    bias_row = jnp.tile(bias.reshape(1, C), (1, FFT_COLS // C))
    y = fft_last(consts['last'], b.reshape(2 * FFT_R, FFT_R * C), z2d, bias_row, gate.reshape(FFT_R, FFT_R * C))
    return y.reshape(B, L, C)


def dense_dft_constants(L):
    C, S = _dft_parts(2 * L)
    m_fwd = np.block([[C[:, :L], S[:, :L]], [-S[:, :L], C[:, :L]]])
    m_fwd_real = np.concatenate([C, -S], 0)
    m_inv = np.block([[C[:L], -S[:L]], [S[:L], C[:L]]])
    bf = lambda a: jnp.asarray(a, BF16)
    return bf(m_fwd), bf(m_fwd_real), bf(m_inv)


def _short_conv_kernel(mf_ref, mfr_ref, mi_ref, z_ref, k_ref, bias_ref, gate_ref, o_ref):
    n = k_ref.shape[0]
    z = z_ref[...]
    X = jnp.dot(mf_ref[...], z.astype(BF16), preferred_element_type=F32)
    Hs = jnp.dot(mfr_ref[...], k_ref[...].astype(BF16), preferred_element_type=F32)
    xr, xi, hr, hi = X[:n], X[n:], Hs[:n], Hs[n:]
    y = jnp.concatenate([xr * hr - xi * hi, xr * hi + xi * hr], 0).astype(BF16)
    o_ref[...] = gate_ref[...] * (jnp.dot(mi_ref[...], y, preferred_element_type=F32) + bias_ref[...] * z)


def short_conv(z, gate, h_fwd, h_bwd, bias):
    B, L, C = z.shape
    assert B == 2
    mf, mfr, mi = dense_dft_constants(L)
    y = pl.pallas_call(
        _short_conv_kernel,
        out_shape=jax.ShapeDtypeStruct((2 * L, C), F32),
        name="hyena_short_conv")(mf, mfr, mi, z.reshape(2 * L, C), two_sided_kernel(h_fwd, h_bwd, 2 * L),
                                 bias.reshape(1, C), gate.reshape(2 * L, C))
    return y.reshape(B, L, C)


def hyena_branch(p, with_ctx, n_ctx, conv_w, conv_b, w1, b1, w2, b2, w3, freq, hbias):
    consts = hyena_fft_constants()
    hy_w = (w1, b1, w2, b2, w3, freq)
    v, x1, x2 = jnp.split(short_conv3(p[:, n_ctx:], conv_w, conv_b), 3, axis=-1)
    filt = hyena_filters(p.shape[1] - n_ctx, *hy_w)
    z = long_conv(v, x1, filter_spectrum(filt[:, 0, 0], filt[:, 0, 1], consts), hbias[0], consts)
    o = long_conv(z, x2, filter_spectrum(filt[:, 1, 0], filt[:, 1, 1], consts), hbias[1], consts)
    if with_ctx:
        vc, xc1, xc2 = jnp.split(short_conv3(p[:, :n_ctx], conv_w, conv_b), 3, axis=-1)
        fc = hyena_filters(n_ctx, *hy_w)
        zc = short_conv(vc, xc1, fc[:, 0, 0], fc[:, 0, 1], hbias[0])
        oc = short_conv(zc, xc2, fc[:, 1, 0], fc[:, 1, 1], hbias[1])
    else:
        oc = jnp.zeros((o.shape[0], n_ctx, o.shape[2]), o.dtype)
    return o, oc


ROUTER_PAD = 128


def _merge_kernel(of_ref, ob_ref, r_ref, om_ref, ohl_ref, ohc_ref, g_ref, x_ref, mod_ref,
                  wg_ref, wm_ref, wh_ref, wo_ref, gn_ref, l1g_ref, l1b_ref, rw_ref, rb_ref,
                  x1_ref, h2_ref, lg_ref, *, n_ctx_tiles):
    o = of_ref[0] + ob_ref[0]
    parts = [_rms(o[:, h * GLA_DV:(h + 1) * GLA_DV], gn_ref[...]) for h in range(GLA_HEADS)]
    og = (jnp.concatenate(parts, axis=1) * jax.nn.silu(r_ref[0].astype(F32))).astype(BF16)
    oh = jnp.where(pl.program_id(1) < n_ctx_tiles, ohc_ref[0], ohl_ref[0]).astype(BF16)
    D = D_MODEL
    g = g_ref[0]
    y = (jax.nn.sigmoid(g[:, :D].astype(F32)) * jnp.dot(og, wg_ref[...], preferred_element_type=F32)
         + jax.nn.sigmoid(g[:, D:2 * D].astype(F32)) * jnp.dot(om_ref[0], wm_ref[...], preferred_element_type=F32)
         + jax.nn.sigmoid(g[:, 2 * D:].astype(F32)) * jnp.dot(oh, wh_ref[...], preferred_element_type=F32))
    ym = jnp.dot(y.astype(BF16), wo_ref[...], preferred_element_type=F32)
    gate1, sh2, sc2 = mod_ref[0, 0, 0:1], mod_ref[0, 0, 1:2], mod_ref[0, 0, 2:3]
    x1 = _layer_norm(DEEPNORM_ALPHA * x_ref[0] + gate1 * ym) * l1g_ref[...] + l1b_ref[...]
    x1_ref[0] = x1
    h2f = _layer_norm(x1) * (1.0 + sc2) + sh2
    h2_ref[0] = pack_bf16_pairs(h2f)
    h2 = h2f.astype(BF16)
    lg_ref[0] = jnp.dot(h2, rw_ref[...], preferred_element_type=F32) + rb_ref[...]


def merge_block(o_f, o_b, P, o_mla, o_hy_lat, o_hy_ctx, xa, mod, w_br_gla, w_br_mla, w_br_hy, w_out, gla_norm,
                ln1_g, ln1_b, router_w, router_b, n_ctx):
    B, T, D = xa.shape
    NC = n_ctx // ROW_TILE
    bf = lambda a: a.astype(BF16)
    row = lambda a: a.reshape(1, -1)
    rw = jnp.pad(router_w, ((0, 0), (0, ROUTER_PAD - N_EXPERTS))).astype(BF16)
    rb = jnp.pad(router_b, (0, ROUTER_PAD - N_EXPERTS)).reshape(1, -1)
    full = lambda a: pl.BlockSpec(a.shape, lambda b, t: (0,) * a.ndim)
    tile = lambda w: pl.BlockSpec((1, ROW_TILE, w), lambda b, t: (b, t, 0))
    lat = pl.BlockSpec((1, ROW_TILE, BRANCH_W), lambda b, t: (b, jnp.maximum(t - NC, 0), 0))
    ctx = pl.BlockSpec((1, ROW_TILE, BRANCH_W), lambda b, t: (b, jnp.minimum(t, NC - 1), 0))
    ws = [bf(w_br_gla), bf(w_br_mla), bf(w_br_hy), bf(w_out), row(gla_norm), row(ln1_g), row(ln1_b), rw, rb]
    return pl.pallas_call(
        functools.partial(_merge_kernel, n_ctx_tiles=NC), grid=(B, T // ROW_TILE),
        in_specs=[tile(BRANCH_W), tile(BRANCH_W), pcol('r'), tile(BRANCH_W), lat, ctx, pcol('gate'), tile(D),
                  pl.BlockSpec((1, 1, 3, D), _tile_kind(NC))] + [full(w) for w in ws],
        out_specs=[tile(D), tile(HALF_D), tile(ROUTER_PAD)],
        out_shape=[jax.ShapeDtypeStruct((B, T, D), F32), jax.ShapeDtypeStruct((B, T, HALF_D), U32),
                   jax.ShapeDtypeStruct((B, T, ROUTER_PAD), F32)],
        compiler_params=pltpu.CompilerParams(dimension_semantics=("arbitrary", "arbitrary"),
                                             vmem_limit_bytes=48 * 1024 * 1024),
        name="merge_block")(o_f, o_b, P, o_mla, o_hy_lat, o_hy_ctx, P, xa, mod, *ws)


def kernel(x, c, ctx, c_ctx, ada_w, ada_b, w_in, gla_wa2_f, gla_ba_f, gla_wa2_b, gla_ba_b, gla_norm,
           mla_q_norm, mla_w_uq, mla_kv_norm, mla_w_ukv, hy_conv_w, hy_conv_b, hy_w1, hy_b1, hy_w2, hy_b2,
           hy_w3, hy_freq, hy_bias, w_br_gla, w_br_mla, w_br_hy, w_out, ln1_g, ln1_b, ln2_g, ln2_b,
           router_w, router_b, moe_w1, moe_b1, moe_w2, moe_b2):
    B, L, D = x.shape
    CL = ctx.shape[1]
    T = CL + L
    tables = rope_tables(L, CL)
    xa = jnp.concatenate([ctx, x], axis=1)
    cond = jnp.concatenate([jax.nn.silu(c), jax.nn.silu(c_ctx)[None], jnp.zeros((8 - B - 1, D), F32)], 0)
    for l in range(DEPTH):
        last = l == DEPTH - 1
        ada = pmm(cond, ada_w[l])[:B + 1] + ada_b[l]
        ada = jnp.stack([ada[:B], jnp.broadcast_to(ada[B:], (B, 6 * D))], 1).reshape(B, 2, 6, D)
        P = ln_mod_proj(xa, ada[:, :, 0:2], permute_w_in(w_in[l]), CL)
        o_f, o_b = gla_scan(P, gla_wa2_f[l], gla_ba_f[l], gla_wa2_b[l], gla_ba_b[l], CL)
        q, k, v = mla_prep(P, tables, mla_weights(mla_w_uq[l], mla_w_ukv[l]), mla_q_norm[l], mla_kv_norm[l])
        o_mla = mla_attention(q, k, v, CL)
        hy = P[:, :, COL['hy']:COL['hy'] + N_HY].astype(F32)
        o_hy, o_hy_ctx = hyena_branch(hy, not last, CL, hy_conv_w[l], hy_conv_b[l], hy_w1[l], hy_b1[l], hy_w2[l],
                                      hy_b2[l], hy_w3[l], hy_freq[l], hy_bias[l])
        x1, h2, logits = merge_block(o_f, o_b, P, o_mla, o_hy, o_hy_ctx, xa, ada[:, :, 2:5],
                                     w_br_gla[l], w_br_mla[l], w_br_hy[l], w_out[l], gla_norm[l],
                                     ln1_g[l], ln1_b[l], router_w[l], router_b[l], CL)
        t0 = CL // ROW_TILE if last else 0
        nt = T // ROW_TILE - t0
        xa = moe_ffn_residual(h2, logits[:, t0 * ROW_TILE:, :N_EXPERTS], x1, ada[:, :, 5:6],
                              moe_w1[l], moe_b1[l], moe_w2[l], moe_b2[l], ln2_g[l], ln2_b[l],
                              t0, nt, CL // ROW_TILE)
    return xa
```

```python
import functools
import math
import jax
import jax.numpy as jnp
from jax import lax
import numpy as np
from jax.experimental import pallas as pl
from jax.experimental.pallas import tpu as pltpu

D_MODEL = 1024
DEPTH = 2
GRID_W = 64
BRANCH_W = D_MODEL // 2
N_BRANCH = 3
GLA_HEADS = 4
GLA_DV = BRANCH_W // GLA_HEADS
GLA_DK = GLA_DV // 2
GLA_RANK = 16
GLA_TAU = 16.0
GLA_CHUNK = 16
MLA_HEADS = 8
MLA_DV = BRANCH_W // MLA_HEADS
MLA_NOPE = MLA_DV
MLA_ROPE = MLA_NOPE // 2
MLA_Q_RANK = 3 * D_MODEL // 8
MLA_KV_RANK = D_MODEL // 4
ROPE_BASE = 10000.0
Q_BLOCK = 128
HY_W = BRANCH_W
HY_ORDER = 2
HY_DIRS = 2
HY_BANDS = 16
HY_EMB = 2 * HY_BANDS + 1
HY_FFN = 64
HY_DECAY_TARGET = 1e-2
HY_FAST_PCT = 0.3
HY_SLOW_PCT = 1.5
N_EXPERTS = 32
TOP_K = 4
D_FF = D_MODEL
SWIGLU_LIMIT = 7.0
SWIGLU_ALPHA = 1.702
MOE_BLOCK = 128
LN_EPS = 1e-5
RMS_EPS = 1e-6
DEEPNORM_ALPHA = (2 * DEPTH) ** 0.25
IN_SPLITS = (
    GLA_HEADS * GLA_DK, GLA_HEADS * GLA_DV, GLA_RANK, GLA_RANK, MLA_KV_RANK, MLA_ROPE,
    GLA_HEADS * GLA_DK, GLA_HEADS * GLA_DV, MLA_Q_RANK, 3 * HY_W, N_BRANCH * D_MODEL,
)
N_KEY_GROUPS = 6
KEY_COLS = sum(IN_SPLITS[:N_KEY_GROUPS])
IN_OFFSETS = tuple(int(o) for o in np.cumsum(IN_SPLITS)[:-1])
IN_TOTAL = sum(IN_SPLITS)
F32 = jnp.float32
BF16 = jnp.bfloat16
U32 = jnp.uint32


def _mm_kernel(a_ref, w_ref, o_ref):
    o_ref[...] = jnp.dot(a_ref[...], w_ref[...], preferred_element_type=F32)


def _pick(n, cands):
    for c in cands:
        if n % c == 0:
            return c
    return n


def pmm(a, w):
    lead = a.shape[:-1]
    K = a.shape[-1]
    N = w.shape[-1]
    a2 = a.reshape(-1, K).astype(BF16)
    M = a2.shape[0]
    N0 = N
    if N % 128:
        N = -(-N // 256) * 256
        w = jnp.pad(w, ((0, 0), (0, N - N0)))
    tm = _pick(M, (512, 256, 128, 64, 32, 16, 8))
    tn = _pick(N, (512, 256, 128))
    out = pl.pallas_call(
        _mm_kernel,
        grid=(M // tm, N // tn),
        in_specs=[pl.BlockSpec((tm, K), lambda i, j: (i, 0)),
                  pl.BlockSpec((K, tn), lambda i, j: (0, j))],
        out_specs=pl.BlockSpec((tm, tn), lambda i, j: (i, j)),
        out_shape=jax.ShapeDtypeStruct((M, N), F32),
    )(a2, w.astype(BF16))
    return out[:, :N0].reshape(lead + (N0,))


ROW_TILE = 256
HEAD_PAD = 128
N_GATE = N_BRANCH * D_MODEL
N_HY = 3 * HY_W
COL = dict(gate=0, hy=N_GATE, qa=N_GATE + N_HY)
COL['misc'] = COL['qa'] + MLA_Q_RANK
COL['v'] = COL['misc'] + HEAD_PAD
COL['r'] = COL['v'] + GLA_HEADS * GLA_DV
COL['k'] = COL['r'] + GLA_HEADS * GLA_DV
COL['q'] = COL['k'] + GLA_HEADS * GLA_DK
COL['kva'] = COL['q'] + GLA_HEADS * GLA_DK
P_COLS = COL['kva'] + MLA_KV_RANK
WIDTH = dict(gate=N_GATE, hy=N_HY, qa=MLA_Q_RANK, misc=HEAD_PAD, v=GLA_HEADS * GLA_DV, r=GLA_HEADS * GLA_DV,
             k=GLA_HEADS * GLA_DK, q=GLA_HEADS * GLA_DK, kva=MLA_KV_RANK)
MISC_AF, MISC_AB, MISC_KR = 0, GLA_RANK, MLA_NOPE
P_CHUNK = 768
assert all(COL[g] % WIDTH[g] == 0 for g in COL) and P_COLS % P_CHUNK == 0


def permute_w_in(w_in):
    o = (0,) + IN_OFFSETS + (IN_TOTAL,)
    k, v, af, ab, kva, kr, q, r, qa, hy, gate = [w_in[:, o[i]:o[i + 1]] for i in range(len(IN_SPLITS))]
    z = lambda n: jnp.zeros((w_in.shape[0], n), w_in.dtype)
    misc = jnp.concatenate([af, ab, z(MISC_KR - 2 * GLA_RANK), kr, z(HEAD_PAD - MISC_KR - MLA_ROPE)], 1)
    return jnp.concatenate([gate, hy, qa, misc, v, r, k, q, kva], 1).astype(BF16)


def _layer_norm(x):
    xc = x - jnp.mean(x, axis=-1, keepdims=True)
    return xc * lax.rsqrt(jnp.mean(xc * xc, axis=-1, keepdims=True) + LN_EPS)


def _rms(x, gain):
    return x * lax.rsqrt(jnp.mean(x * x, axis=-1, keepdims=True) + RMS_EPS) * gain


def _proj_kernel(x_ref, mod_ref, w_ref, o_ref):
    h = _layer_norm(x_ref[0]) * (1.0 + mod_ref[0, 0, 1:2]) + mod_ref[0, 0, 0:1]
    hb = h.astype(BF16)
    for c in range(0, P_COLS, P_CHUNK):
        o_ref[0, :, c:c + P_CHUNK] = jnp.dot(hb, w_ref[:, c:c + P_CHUNK], preferred_element_type=F32).astype(BF16)


def _tile_kind(n_ctx_tiles):
    return lambda b, t: (b, jnp.where(t < n_ctx_tiles, 1, 0), 0, 0)


def ln_mod_proj(xa, mod, w_p, n_ctx):
    B, T, D = xa.shape
    return pl.pallas_call(
        _proj_kernel, grid=(B, T // ROW_TILE),
        in_specs=[pl.BlockSpec((1, ROW_TILE, D), lambda b, t: (b, t, 0)),
                  pl.BlockSpec((1, 1, 2, D), _tile_kind(n_ctx // ROW_TILE)),
                  pl.BlockSpec((D, P_COLS), lambda b, t: (0, 0), pipeline_mode=pl.Buffered(1))],
        out_specs=pl.BlockSpec((1, ROW_TILE, P_COLS), lambda b, t: (b, t, 0)),
        out_shape=jax.ShapeDtypeStruct((B, T, P_COLS), BF16),
        compiler_params=pltpu.CompilerParams(dimension_semantics=("arbitrary", "arbitrary"),
                                             vmem_limit_bytes=48 * 1024 * 1024),
        name="ln_mod_proj")(xa, mod, w_p)


def pcol(group, row_map=lambda b, t: t):
    w = WIDTH[group]
    return pl.BlockSpec((1, ROW_TILE, w), lambda b, t, _i=COL[group] // w: (b, row_map(b, t), _i))


ONES_LANE_EVEN = MLA_DV
ONES_LANE_ODD = 0
LOG2E = 1.4426950408889634


def rope_tables(L, n_ctx):
    rows = L // GRID_W
    row = jnp.repeat(jnp.arange(rows, dtype=F32), GRID_W)
    col = jnp.tile(jnp.arange(GRID_W, dtype=F32), rows)
    n_freq = MLA_ROPE // 4
    inv = ROPE_BASE ** (-jnp.arange(n_freq, dtype=F32) / n_freq)
    ang = jnp.concatenate([row[:, None] * inv, col[:, None] * inv], -1)
    ang = jnp.concatenate([jnp.zeros((n_ctx, ang.shape[1]), F32), ang], 0)
    cos, sin = jnp.cos(ang), jnp.sin(ang)
    T = ang.shape[0]
    half = MLA_ROPE // 2
    z = lambda n: jnp.zeros((T, n), F32)
    ct = jnp.concatenate([jnp.ones((T, MLA_NOPE), F32), cos, cos, z(HEAD_PAD - MLA_NOPE - MLA_ROPE)], 1)
    sa = jnp.concatenate([z(MLA_NOPE), -sin, z(HEAD_PAD - MLA_NOPE - half)], 1)
    sb = jnp.concatenate([z(MLA_NOPE + half), sin, z(HEAD_PAD - MLA_NOPE - MLA_ROPE)], 1)
    return ct, sa, sb


def mla_weights(w_uq, w_ukv):
    H = MLA_HEADS
    wq = w_uq.reshape(MLA_Q_RANK, H, MLA_NOPE + MLA_ROPE)
    wq = jnp.pad(wq, ((0, 0), (0, 0), (0, HEAD_PAD - MLA_NOPE - MLA_ROPE))).reshape(MLA_Q_RANK, H * HEAD_PAD)
    wkv = w_ukv.reshape(MLA_KV_RANK, H, MLA_NOPE + MLA_DV)
    wk = jnp.pad(wkv[:, :, :MLA_NOPE], ((0, 0), (0, 0), (0, HEAD_PAD - MLA_NOPE))).reshape(MLA_KV_RANK, H * HEAD_PAD)
    wv = wkv[:, :, MLA_NOPE:].reshape(MLA_KV_RANK, H // 2, 2, MLA_DV)
    z = jnp.zeros_like(wv[:, :, 0])
    wv = jnp.stack([jnp.concatenate([wv[:, :, 0], z], -1), jnp.concatenate([z, wv[:, :, 1]], -1)], 2)
    ones = np.zeros((H // 2, 2, HEAD_PAD), np.float32)
    ones[:, 0, ONES_LANE_EVEN] = 1.0
    ones[:, 1, ONES_LANE_ODD] = 1.0
    return (wq.astype(BF16), wk.astype(BF16), wv.reshape(MLA_KV_RANK, H * HEAD_PAD).astype(BF16),
            jnp.asarray(ones.reshape(1, H * HEAD_PAD)))


def _mla_prep_kernel(qa_ref, kva_ref, misc_ref, ct_ref, sa_ref, sb_ref, wq_ref, wk_ref, wv_ref, ones_ref,
                     qn_ref, kn_ref, q_ref, k_ref, v_ref):
    ct, sa, sb = ct_ref[...], sa_ref[...], sb_ref[...]
    half = MLA_ROPE // 2
    rope = lambda x: x * ct + pltpu.roll(x, HEAD_PAD - half, 1) * sa + pltpu.roll(x, half, 1) * sb
    qn = _rms(qa_ref[0].astype(F32), qn_ref[...]).astype(BF16)
    q = jnp.dot(qn, wq_ref[...], preferred_element_type=F32)
    kn = _rms(kva_ref[0].astype(F32), kn_ref[...]).astype(BF16)
    k = jnp.dot(kn, wk_ref[...], preferred_element_type=F32)
    v_ref[0] = (jnp.dot(kn, wv_ref[...], preferred_element_type=F32) + ones_ref[...]).astype(BF16)
    lane = lax.broadcasted_iota(jnp.int32, (1, HEAD_PAD), 1)
    is_kr = (lane >= MISC_KR) & (lane < MISC_KR + MLA_ROPE)
    kr = rope(jnp.where(is_kr, misc_ref[0].astype(F32), 0.0))
    qscale = (MLA_NOPE + MLA_ROPE) ** -0.5 * LOG2E
    for h in range(MLA_HEADS):
        hs = slice(h * HEAD_PAD, (h + 1) * HEAD_PAD)
        q_ref[0, :, hs] = (rope(q[:, hs]) * qscale).astype(BF16)
        k_ref[0, :, hs] = (k[:, hs] + kr).astype(BF16)


def mla_prep(P, tables, weights, q_norm, kv_norm):
    B, T, _ = P.shape
    ct, sa, sb = tables
    wq, wk, wv, ones = weights
    full = lambda a: pl.BlockSpec(a.shape, lambda b, t: (0,) * a.ndim)
    tab = pl.BlockSpec((ROW_TILE, HEAD_PAD), lambda b, t: (t, 0))
    qn, kn = q_norm.reshape(1, -1), kv_norm.reshape(1, -1)
    W = MLA_HEADS * HEAD_PAD
    out = pl.BlockSpec((1, ROW_TILE, W), lambda b, t: (b, t, 0))
    return pl.pallas_call(
        _mla_prep_kernel, grid=(B, T // ROW_TILE),
        in_specs=[pcol('qa'), pcol('kva'), pcol('misc'), tab, tab, tab, full(wq), full(wk), full(wv), full(ones),
                  full(qn), full(kn)],
        out_specs=[out, out, out],
        out_shape=[jax.ShapeDtypeStruct((B, T, W), BF16)] * 3,
        compiler_params=pltpu.CompilerParams(dimension_semantics=("arbitrary", "arbitrary")),
        name="mla_prep")(P, P, P, ct, sa, sb, wq, wk, wv, ones, qn, kn)


ATT_TQ = 256
ATT_TK = 768


def _attn_kernel(q_ref, k_ref, v_ref, o_ref, s_buf, p_buf, m_buf, a_buf, mrun, acc, *, n_ctx, tk):
    tq = q_ref.shape[1]
    T = k_ref.shape[1]
    in_ctx = pl.program_id(2) * tq < n_ctx
    lane = lax.broadcasted_iota(jnp.int32, (tq, HEAD_PAD), 1)
    hs = lambda h: slice(h * HEAD_PAD, (h + 1) * HEAD_PAD)

    def stage_a(c, slot, size):
        start = c * size
        for h in range(2):
            k = k_ref[0, pl.ds(start, size), hs(h)]
            s = lax.dot_general(q_ref[0, :, hs(h)], k, (((1,), (1,)), ((), ())), preferred_element_type=F32)
            s_buf[slot, h, :, :size] = s
            m_prev = mrun[h]
            m_new = jnp.maximum(m_prev, jnp.max(s, axis=-1, keepdims=True))
            mrun[h] = m_new
            m_buf[slot, h] = m_new
            a_buf[slot, h] = jnp.exp2(m_prev - m_new)

    def stage_b(slot, size):
        for h in range(2):
            p_buf[slot, h, :, :size] = jnp.exp2(s_buf[slot, h, :, :size] - m_buf[slot, h]).astype(BF16)

    def stage_c(c, slot, size):
        start = c * size
        for h in range(2):
            v = v_ref[0, pl.ds(start, size), hs(h)]
            acc[h] = a_buf[slot, h] * acc[h] + jnp.dot(p_buf[slot, h, :, :size], v, preferred_element_type=F32)

    def run(n, size):
        for h in range(2):
            mrun[h] = jnp.full((tq, 1), -jnp.inf, F32)
            acc[h] = jnp.zeros((tq, HEAD_PAD), F32)
        if n == 1:
            stage_a(0, 0, size)
            stage_b(0, size)
            stage_c(0, 0, size)
        else:
            stage_a(0, 0, size)
            stage_b(0, size)
            stage_a(1, 1, size)

            for c in range(n - 2):
                stage_c(c, c % 2, size)
                stage_b(1 - c % 2, size)
                stage_a(c + 2, c % 2, size)
            stage_c(n - 2, (n - 2) % 2, size)
            stage_b((n - 1) % 2, size)
            stage_c(n - 1, (n - 1) % 2, size)
        a0, a1 = acc[0], acc[1]
        l0 = a0[:, ONES_LANE_EVEN:ONES_LANE_EVEN + 1]
        l1 = a1[:, ONES_LANE_ODD:ONES_LANE_ODD + 1]
        o_ref[0] = jnp.where(lane < MLA_DV, a0 / l0, a1 / l1).astype(o_ref.dtype)

    @pl.when(in_ctx)
    def _():
        run(1, n_ctx)

    @pl.when(jnp.logical_not(in_ctx))
    def _():
        run(T // tk, tk)


def mla_attention(q, k, v, n_ctx):
    B, T, _ = q.shape
    assert T % ATT_TQ == 0 and T % ATT_TK == 0 and n_ctx % ATT_TQ == 0 and n_ctx <= ATT_TK
    kern = functools.partial(_attn_kernel, n_ctx=n_ctx, tk=ATT_TK)
    return pl.pallas_call(
        kern,
        grid=(B, MLA_HEADS // 2, T // ATT_TQ),
        in_specs=[pl.BlockSpec((1, ATT_TQ, 2 * HEAD_PAD), lambda b, h, i: (b, i, h)),
                  pl.BlockSpec((1, T, 2 * HEAD_PAD), lambda b, h, i: (b, 0, h)),
                  pl.BlockSpec((1, T, 2 * HEAD_PAD), lambda b, h, i: (b, 0, h))],
        out_specs=pl.BlockSpec((1, ATT_TQ, 2 * MLA_DV), lambda b, h, i: (b, i, h)),
        out_shape=jax.ShapeDtypeStruct((B, T, MLA_HEADS * MLA_DV), BF16),
        scratch_shapes=[pltpu.VMEM((2, 2, ATT_TQ, ATT_TK), F32), pltpu.VMEM((2, 2, ATT_TQ, ATT_TK), BF16),
                        pltpu.VMEM((2, 2, ATT_TQ, 1), F32), pltpu.VMEM((2, 2, ATT_TQ, 1), F32),
                        pltpu.VMEM((2, ATT_TQ, 1), F32), pltpu.VMEM((2, ATT_TQ, HEAD_PAD), F32)],
        compiler_params=pltpu.CompilerParams(
            dimension_semantics=("arbitrary", "arbitrary", "arbitrary"),
            vmem_limit_bytes=56 * 1024 * 1024),
        name="mla_attention",
    )(q, k, v)


MOE_TM = 256
HALF_D = D_MODEL // 2
HI_MASK = 0xFFFF0000
N_ROW_DMA = ROW_TILE * TOP_K


def pack_bf16_pairs(x):
    u = pltpu.bitcast(x.astype(BF16).astype(F32), U32)
    return (u[:, :HALF_D] >> 16) | (u[:, HALF_D:] & jnp.uint32(HI_MASK))


def unpack_bf16_pairs(w):
    lo = pltpu.bitcast(w << 16, F32)
    hi = pltpu.bitcast(w & jnp.uint32(HI_MASK), F32)
    return jnp.concatenate([lo, hi], axis=1).astype(BF16)


def moe_route(logits):
    N = logits.shape[0]
    top_val, top_idx = lax.top_k(logits, TOP_K)
    gates = jax.nn.softmax(top_val, axis=-1)
    NK = N * TOP_K
    flat_e = top_idx.reshape(NK)
    onehot = (flat_e[:, None] == jnp.arange(N_EXPERTS, dtype=flat_e.dtype)[None, :]).astype(jnp.int32)
    csum = jnp.cumsum(onehot, axis=0)
    rank = jnp.take_along_axis(csum, flat_e[:, None], axis=1)[:, 0] - 1
    counts = csum[-1]
    padded = (counts + MOE_TM - 1) // MOE_TM * MOE_TM
    pad_end = jnp.cumsum(padded)
    dest = ((pad_end - padded)[flat_e] + rank).astype(jnp.int32)
    n_blocks = NK // MOE_TM + N_EXPERTS
    blk_start = jnp.arange(n_blocks, dtype=jnp.int32) * MOE_TM
    block_e = jnp.minimum(jnp.searchsorted(pad_end, blk_start, side='right'), N_EXPERTS - 1).astype(jnp.int32)
    first = jnp.concatenate([jnp.ones((1,), jnp.int32), (block_e[1:] != block_e[:-1]).astype(jnp.int32)])
    n_used = (pad_end[-1:] // MOE_TM).astype(jnp.int32)
    return dest, gates, block_e, first, n_used


def _dispatch_kernel(dest_ref, h_ref, xb_in_ref, xb_ref, sem):
    del xb_in_ref

    def start(r, c):
        for k in range(TOP_K):
            pltpu.make_async_copy(h_ref.at[0, pl.ds(r, 1)], xb_ref.at[pl.ds(dest_ref[r * TOP_K + k], 1)],
                                  sem).start()
        return c

    lax.fori_loop(0, ROW_TILE, start, 0, unroll=4)
    for k in range(TOP_K):
        pltpu.make_async_copy(h_ref.at[0], xb_ref.at[pl.ds(0, ROW_TILE)], sem).wait()


def moe_dispatch(h2p, dest, n_rows, t0, nt):
    B = h2p.shape[0]
    xb0 = jnp.zeros((n_rows, HALF_D), U32)
    return pl.pallas_call(
        _dispatch_kernel, grid=(B, nt),
        in_specs=[pl.BlockSpec((N_ROW_DMA,), lambda b, t: (b * nt + t,), memory_space=pltpu.SMEM),
                  pl.BlockSpec((1, ROW_TILE, HALF_D), lambda b, t: (b, t + t0, 0)),
                  pl.BlockSpec(memory_space=pl.ANY)],
        out_specs=pl.BlockSpec(memory_space=pl.ANY),
        out_shape=jax.ShapeDtypeStruct((n_rows, HALF_D), U32),
        scratch_shapes=[pltpu.SemaphoreType.DMA],
        input_output_aliases={2: 0},
        compiler_params=pltpu.CompilerParams(dimension_semantics=("arbitrary", "arbitrary")),
        name="moe_dispatch")(dest, h2p, xb0)


def _moe_kernel(be_ref, first_ref, nb_ref, x_ref, w1_ref, b1_ref, w2_ref, b2_ref, o_ref, w1_bf, w2_bf):
    i = pl.program_id(0)

    @pl.when(jnp.logical_and(i < nb_ref[0], first_ref[i] == 1))
    def _():
        w1_bf[...] = w1_ref[0].astype(BF16)
        w2_bf[...] = w2_ref[0].astype(BF16)

    @pl.when(i < nb_ref[0])
    def _():
        x = unpack_bf16_pairs(x_ref[...])
        hcat = jnp.dot(x, w1_bf[...], preferred_element_type=F32) + b1_ref[0]
        glu = jnp.minimum(hcat[:, :D_FF], SWIGLU_LIMIT)
        lin = jnp.clip(hcat[:, D_FF:], -SWIGLU_LIMIT, SWIGLU_LIMIT)
        act = glu * jax.nn.sigmoid(SWIGLU_ALPHA * glu) * (lin + 1)
        o_ref[...] = jnp.dot(act.astype(BF16), w2_bf[...], preferred_element_type=F32) + b2_ref[0]

    @pl.when(i >= nb_ref[0])
    def _():
        o_ref[...] = jnp.zeros_like(o_ref)


def moe_experts(xb, block_e, first, n_used, w1, b1, w2, b2):
    R = xb.shape[0]
    D = D_MODEL
    n_blocks = R // MOE_TM
    grid_spec = pltpu.PrefetchScalarGridSpec(
        num_scalar_prefetch=3,
        grid=(n_blocks,),
        in_specs=[pl.BlockSpec((MOE_TM, HALF_D), lambda i, be, fi, nb: (i, 0)),
                  pl.BlockSpec((1, D, 2 * D_FF), lambda i, be, fi, nb: (be[i], 0, 0)),
                  pl.BlockSpec((1, 1, 2 * D_FF), lambda i, be, fi, nb: (be[i], 0, 0)),
                  pl.BlockSpec((1, D_FF, D), lambda i, be, fi, nb: (be[i], 0, 0)),
                  pl.BlockSpec((1, 1, D), lambda i, be, fi, nb: (be[i], 0, 0))],
        out_specs=pl.BlockSpec((MOE_TM, D), lambda i, be, fi, nb: (i, 0)),
        scratch_shapes=[pltpu.VMEM((D, 2 * D_FF), BF16), pltpu.VMEM((D_FF, D), BF16)],
    )
    return pl.pallas_call(
        _moe_kernel,
        grid_spec=grid_spec,
        out_shape=jax.ShapeDtypeStruct((R, D), F32),
        compiler_params=pltpu.CompilerParams(
            dimension_semantics=("arbitrary",), vmem_limit_bytes=56 * 1024 * 1024),
        name="moe_experts",
    )(block_e, first, n_used, xb, w1, b1[:, None, :], w2, b2[:, None, :])


def _combine_kernel(dest_ref, x_ref, gate_ref, mod_ref, lg_ref, lb_ref, yb_ref, o_ref, buf, sem):
    def start(r, c):
        for k in range(TOP_K):
            pltpu.make_async_copy(yb_ref.at[pl.ds(dest_ref[r * TOP_K + k], 1)], buf.at[k, pl.ds(r, 1)], sem).start()
        return c

    lax.fori_loop(0, ROW_TILE, start, 0, unroll=4)
    for k in range(TOP_K):
        pltpu.make_async_copy(yb_ref.at[pl.ds(0, ROW_TILE)], buf.at[k], sem).wait()
    g = gate_ref[...]
    f = g[:, 0:1] * buf[0]
    for k in range(1, TOP_K):
        f = f + g[:, k:k + 1] * buf[k]
    o_ref[0] = _layer_norm(DEEPNORM_ALPHA * x_ref[0] + mod_ref[0, 0, 0:1] * f) * lg_ref[...] + lb_ref[...]


def moe_combine(yb, dest, gates, x1, mod, ln_g, ln_b, t0, nt, n_ctx_tiles):
    B, T, D = x1.shape
    row = lambda a: a.reshape(1, -1)
    full = lambda a: pl.BlockSpec(a.shape, lambda b, t: (0,) * a.ndim)
    return pl.pallas_call(
        _combine_kernel, grid=(B, nt),
        in_specs=[pl.BlockSpec((N_ROW_DMA,), lambda b, t: (b * nt + t,), memory_space=pltpu.SMEM),
                  pl.BlockSpec((1, ROW_TILE, D), lambda b, t: (b, t + t0, 0)),
                  pl.BlockSpec((ROW_TILE, TOP_K), lambda b, t: (b * nt + t, 0)),
                  pl.BlockSpec((1, 1, 1, D), lambda b, t: (b, jnp.where(t + t0 < n_ctx_tiles, 1, 0), 0, 0)),
                  full(row(ln_g)), full(row(ln_b)),
                  pl.BlockSpec(memory_space=pl.ANY)],
        out_specs=pl.BlockSpec((1, ROW_TILE, D), lambda b, t: (b, t, 0)),
        out_shape=jax.ShapeDtypeStruct((B, nt * ROW_TILE, D), F32),
        scratch_shapes=[pltpu.VMEM((TOP_K, ROW_TILE, D), F32), pltpu.SemaphoreType.DMA],
        compiler_params=pltpu.CompilerParams(dimension_semantics=("arbitrary", "arbitrary")),
        name="moe_combine")(dest, x1, gates, mod, row(ln_g), row(ln_b), yb)


def moe_ffn_residual(h2p, logits, x1, gate2, w1, b1, w2, b2, ln_g, ln_b, t0, nt, n_ctx_tiles):
    B = h2p.shape[0]
    N = B * nt * ROW_TILE
    dest, gates, block_e, first, n_used = moe_route(logits.reshape(N, N_EXPERTS))
    n_rows = (N * TOP_K // MOE_TM + N_EXPERTS) * MOE_TM
    xb = moe_dispatch(h2p, dest, n_rows, t0, nt)
    yb = moe_experts(xb, block_e, first, n_used, w1, b1, w2, b2)
    return moe_combine(yb, dest, gates, x1, gate2, ln_g, ln_b, t0, nt, n_ctx_tiles)


GLA_TILE = 256
GLA_BASE = 16
GLA_LEVELS = (16, 32, 64, 128)
N_SEL = 2 * (len(GLA_LEVELS) + 1)


def gla_constants():
    n = GLA_TILE
    i = np.arange(n)[:, None]
    t = np.arange(n)[None, :]
    sizes = GLA_LEVELS + (n,)
    P = [((t <= i) & (t // s == i // s)) for s in sizes]
    S = [((t > i) & (t // s == i // s)) for s in sizes]
    sel_f = np.concatenate(P + S, 0)
    sel_b = np.concatenate([p.T for p in P] + [s.T for s in S], 0)
    lv = [((i // (2 * s) == t // (2 * s)) & ((i // s) % 2 == 1) & ((t // s) % 2 == 0)) for s in GLA_LEVELS]
    dg = (i // GLA_BASE == t // GLA_BASE) & (t <= i)
    msk_f = np.stack(lv + [dg], 0)
    msk_b = np.stack([m.T for m in lv] + [dg.T], 0)
    return (jnp.asarray(np.stack([sel_f, sel_b], 0), BF16), jnp.asarray(np.stack([msk_f, msk_b], 0), F32))


def _gla_kernel(qf_ref, kf_ref, vf_ref, af_ref, qb_ref, kb_ref, vb_ref, ab_ref,
                wf_ref, bf_ref, wb_ref, bb_ref, sel_ref, msk_ref, of_ref, ob_ref, state):
    TL = GLA_TILE
    NL = len(GLA_LEVELS)

    @pl.when(pl.program_id(1) == 0)
    def _():
        state[...] = jnp.zeros_like(state)

    lane = lax.broadcasted_iota(jnp.int32, (1, 2 * GLA_DK), 1)
    nt = lambda a, b: lax.dot_general(a, b, (((1,), (1,)), ((), ())), preferred_element_type=F32)
    dirs = ((qf_ref, kf_ref, vf_ref, af_ref, wf_ref, bf_ref, of_ref, TL - 1),
            (qb_ref, kb_ref, vb_ref, ab_ref, wb_ref, bb_ref, ob_ref, 0))
    for d, (q_ref, k_ref, v_ref, a_ref, w_ref, b_ref, o_ref, last) in enumerate(dirs):
        x = jnp.dot(a_ref[0], w_ref[...], preferred_element_type=F32) + b_ref[...]
        g = jax.nn.log_sigmoid(x) * (1.0 / GLA_TAU)
        g1 = g.astype(BF16)
        g2 = (g - g1.astype(F32)).astype(BF16)
        sel = sel_ref[d]
        ps = jnp.dot(sel, g1, preferred_element_type=F32) + jnp.dot(sel, g2, preferred_element_type=F32)
        seg = lambda j: ps[j * TL:(j + 1) * TL]
        q = q_ref[0].astype(F32) * (GLA_DK ** -0.5)
        k = k_ref[0].astype(F32)
        e_pre = [jnp.exp(seg(j)) for j in range(NL + 1)]
        e_suf = [jnp.exp(seg(NL + 1 + j)) for j in range(NL + 1)]
        qh = [(q * e).astype(BF16) for e in e_pre]
        kh = [(k * e).astype(BF16) for e in e_suf]
        kd = (k * jnp.exp(-seg(0))).astype(BF16)
        dec = e_pre[NL][last:last + 1]
        for h in range(GLA_HEADS):
            pr = slice((h // 2) * 2 * GLA_DK, (h // 2 + 1) * 2 * GLA_DK)
            mine = (lane // GLA_DK) == (h % 2)
            qm = [jnp.where(mine, t[:, pr], 0) for t in qh[:NL]]
            sc = msk_ref[d, NL] * nt(qm[0], kd[:, pr])
            for j in range(NL):
                sc = sc + msk_ref[d, j] * nt(qm[j], kh[j][:, pr])
            v = v_ref[0, :, h * GLA_DV:(h + 1) * GLA_DV].astype(BF16)
            st = state[d, h]
            o = jnp.dot(sc.astype(BF16), v, preferred_element_type=F32) + nt(qh[NL][:, pr], st.astype(BF16))
            o_ref[0, :, h * GLA_DV:(h + 1) * GLA_DV] = o
            upd = lax.dot_general(v, kh[NL][:, pr], (((0,), (0,)), ((), ())), preferred_element_type=F32)
            state[d, h] = jnp.where(mine, dec[:, pr] * st + upd, 0)


def gla_scan(P, wa2_f, ba_f, wa2_b, ba_b, n_ctx):
    B, T, _ = P.shape
    assert GLA_TILE == ROW_TILE and T % ROW_TILE == 0 and n_ctx % ROW_TILE == 0
    NT, NC = T // ROW_TILE, n_ctx // ROW_TILE
    sel, msk = gla_constants()
    fwd = lambda b, n: n
    bwd = lambda b, n: jnp.where(n < NC, NC - 1 - n, NT - 1 - (n - NC))
    full = lambda a: pl.BlockSpec(a.shape, lambda b, n: (0,) * a.ndim)
    HK, HV = GLA_HEADS * GLA_DK, GLA_HEADS * GLA_DV
    pad = lambda w, off: jnp.zeros((HEAD_PAD, HK), F32).at[off:off + GLA_RANK].set(w).astype(BF16)
    wf, wb = pad(wa2_f, MISC_AF), pad(wa2_b, MISC_AB)
    ba_f2, ba_b2 = ba_f.reshape(1, HK), ba_b.reshape(1, HK)
    out = lambda rm: pl.BlockSpec((1, ROW_TILE, HV), lambda b, n: (b, rm(b, n), 0))
    return pl.pallas_call(
        _gla_kernel,
        grid=(B, NT),
        in_specs=[pcol('q', fwd), pcol('k', fwd), pcol('v', fwd), pcol('misc', fwd),
                  pcol('q', bwd), pcol('k', bwd), pcol('v', bwd), pcol('misc', bwd),
                  full(wf), full(ba_f2), full(wb), full(ba_b2), full(sel), full(msk)],
        out_specs=[out(fwd), out(bwd)],
        out_shape=[jax.ShapeDtypeStruct((B, T, HV), F32)] * 2,
        scratch_shapes=[pltpu.VMEM((2, GLA_HEADS, GLA_DV, 2 * GLA_DK), F32)],
        compiler_params=pltpu.CompilerParams(dimension_semantics=("arbitrary", "arbitrary"),
                                             vmem_limit_bytes=48 * 1024 * 1024),
        name="gla_scan",
    )(P, P, P, P, P, P, P, P, wf, ba_f2, wb, ba_b2, sel, msk)


def hyena_kernels(L, w1, b1, w2, b2, w3, freq):
    t = jnp.linspace(0.0, 1.0, L, dtype=F32)[:, None]
    w = 2 * math.pi * jnp.arange(L, dtype=F32)[:, None] / L
    f = jnp.linspace(1e-4, HY_BANDS - 1, HY_BANDS, dtype=F32)
    z = jnp.concatenate([t, jnp.cos(f * w), -jnp.sin(f * w)], -1)
    tap = np.concatenate([np.arange(L), [0], np.arange(L - 1, 0, -1)])
    z, t = z[tap], t[tap]
    fr = freq.astype(F32)
    h = jnp.sin(fr * (z @ w1.astype(F32) + b1.astype(F32)))
    h = jnp.sin(fr * (h @ w2.astype(F32) + b2.astype(F32)))
    h = (h @ w3.astype(F32)).reshape(2 * L, HY_ORDER, HY_DIRS, HY_W)
    deltas = jnp.abs(jnp.linspace(math.log(HY_DECAY_TARGET) / HY_SLOW_PCT,
                                  math.log(HY_DECAY_TARGET) / HY_FAST_PCT, HY_W, dtype=F32))
    h = h * jnp.exp(-t * deltas)[:, None, None, :]
    norm = jnp.sum(jnp.abs(h[:L]), axis=(0, 2))
    row = jnp.arange(2 * L)[:, None, None]
    k = jnp.where(row < L, h[:, :, 0], jnp.where(row > L, h[:, :, 1], 0.0))
    return jnp.moveaxis(k / (norm * (2 * L)), 1, 0)


FFT_R = 128
FFT_N = FFT_R * FFT_R
FFT_COLS = 2048


def _dft_parts(n):
    ang = 2.0 * np.pi * np.outer(np.arange(n), np.arange(n)) / n
    return np.cos(ang), np.sin(ang)


def hyena_fft_constants():
    C, S = _dft_parts(FFT_R)
    h = FFT_R // 2
    m_first = np.block([[C[:, :h], S[:, :h]], [-S[:, :h], C[:, :h]]])
    m_first_real = np.concatenate([C, -S], 0)
    m_mid = np.block([[C, S], [-S, C]])
    m_mid_inv = np.block([[C, -S], [S, C]])
    m_last = np.block([[C[:h], -S[:h]], [S[:h], C[:h]]])
    ang = 2.0 * np.pi * np.outer(np.arange(FFT_R), np.arange(FFT_R)) / FFT_N
    tw = np.stack([np.cos(ang), np.sin(ang)], -1)
    bf = lambda a: jnp.asarray(a, BF16)
    return dict(first=bf(m_first), first_real=bf(m_first_real), mid=bf(m_mid), mid_inv=bf(m_mid_inv),
                last=bf(m_last), tw=jnp.asarray(tw, F32))


def _fft_outer_kernel(m_ref, x_ref, o_ref):
    o_ref[...] = jnp.dot(m_ref[...], x_ref[...].astype(BF16), preferred_element_type=F32).astype(o_ref.dtype)


def fft_first(m, x2d):
    R, W = x2d.shape
    return pl.pallas_call(
        _fft_outer_kernel, grid=(W // FFT_COLS,),
        in_specs=[pl.BlockSpec(m.shape, lambda j: (0, 0)), pl.BlockSpec((R, FFT_COLS), lambda j: (0, j))],
        out_specs=pl.BlockSpec((m.shape[0], FFT_COLS), lambda j: (0, j)),
        out_shape=jax.ShapeDtypeStruct((m.shape[0], W), BF16),
        compiler_params=pltpu.CompilerParams(dimension_semantics=("arbitrary",)),
        name="hyena_fft_first")(m, x2d)


def _fft_mid_kernel(*refs, conv):
    if conv:
        a_ref, tw_ref, mf_ref, mi_ref, h_ref, o_ref = refs
    else:
        a_ref, tw_ref, mf_ref, o_ref = refs
    R = FFT_R
    tc = tw_ref[0, :, 0:1]
    ts = tw_ref[0, :, 1:2]
    ar = a_ref[0, 0].astype(F32)
    ai = a_ref[1, 0].astype(F32)
    x = jnp.concatenate([ar * tc + ai * ts, ai * tc - ar * ts], 0).astype(BF16)
    X = jnp.dot(mf_ref[...], x, preferred_element_type=F32)
    if not conv:
        o_ref[0, 0] = X[:R]
        o_ref[1, 0] = X[R:]
        return
    xr, xi = X[:R], X[R:]
    hr, hi = h_ref[0, 0], h_ref[1, 0]
    y = jnp.concatenate([xr * hr - xi * hi, xr * hi + xi * hr], 0).astype(BF16)
    Bm = jnp.dot(mi_ref[...], y, preferred_element_type=F32)
    br, bi = Bm[:R], Bm[R:]
    o_ref[0, 0] = (br * tc - bi * ts).astype(o_ref.dtype)
    o_ref[1, 0] = (bi * tc + br * ts).astype(o_ref.dtype)


def fft_mid(a, consts, h=None):
    C = a.shape[-1]
    slab = pl.BlockSpec((2, 1, FFT_R, C), lambda k: (0, k, 0, 0))
    twspec = pl.BlockSpec((1, FFT_R, 2), lambda k: (k, 0, 0))
    mspec = pl.BlockSpec((2 * FFT_R, 2 * FFT_R), lambda k: (0, 0))
    conv = h is not None
    ins = [a, consts['tw'], consts['mid']] + ([consts['mid_inv'], h] if conv else [])
    specs = [slab, twspec, mspec] + ([mspec, slab] if conv else [])
    return pl.pallas_call(
        functools.partial(_fft_mid_kernel, conv=conv), grid=(FFT_R,),
        in_specs=specs, out_specs=slab,
        out_shape=jax.ShapeDtypeStruct((2, FFT_R, FFT_R, C), BF16 if conv else F32),
        compiler_params=pltpu.CompilerParams(dimension_semantics=("arbitrary",)),
        name="hyena_fft_mid")(*ins)


def _fft_last_kernel(m_ref, b_ref, z_ref, bias_ref, gate_ref, o_ref):
    y = jnp.dot(m_ref[...], b_ref[...], preferred_element_type=F32)
    o_ref[...] = gate_ref[...] * (y + bias_ref[...] * z_ref[...])


def fft_last(m, b2d, z2d, bias_row, gate2d):
    R, W = z2d.shape
    col = lambda r: pl.BlockSpec((r, FFT_COLS), lambda j: (0, j))
    return pl.pallas_call(
        _fft_last_kernel, grid=(W // FFT_COLS,),
        in_specs=[pl.BlockSpec(m.shape, lambda j: (0, 0)), col(b2d.shape[0]), col(R),
                  pl.BlockSpec((1, FFT_COLS), lambda j: (0, 0)), col(R)],
        out_specs=col(R),
        out_shape=jax.ShapeDtypeStruct((R, W), F32),
        compiler_params=pltpu.CompilerParams(dimension_semantics=("arbitrary",)),
        name="hyena_fft_last")(m, b2d, z2d, bias_row, gate2d)


def filter_spectrum(k, consts):
    C = k.shape[1]
    a = fft_first(consts['first_real'], k.reshape(FFT_R, FFT_R * C))
    return fft_mid(a.reshape(2, FFT_R, FFT_R, C), consts)


def long_conv(z, gate, spec, bias, consts):
    B, L, C = z.shape
    assert B == 2 and 2 * L == FFT_N and FFT_COLS % C == 0
    z2d = z.reshape(FFT_R, FFT_R * C)
    a = fft_first(consts['first'], z2d)
    b = fft_mid(a.reshape(2, FFT_R, FFT_R, C), consts, spec)
    bias_row = jnp.tile(bias.reshape(1, C), (1, FFT_COLS // C))
    y = fft_last(consts['last'], b.reshape(2 * FFT_R, FFT_R * C), z2d, bias_row, gate.reshape(FFT_R, FFT_R * C))
    return y.reshape(B, L, C)


def dense_dft_constants(L):
    C, S = _dft_parts(2 * L)
    m_fwd = np.block([[C[:, :L], S[:, :L]], [-S[:, :L], C[:, :L]]])
    m_fwd_real = np.concatenate([C, -S], 0)
    m_inv = np.block([[C[:L], -S[:L]], [S[:L], C[:L]]])
    bf = lambda a: jnp.asarray(a, BF16)
    return bf(m_fwd), bf(m_fwd_real), bf(m_inv)


def _short_conv_kernel(mf_ref, mfr_ref, mi_ref, z_ref, k_ref, bias_ref, gate_ref, o_ref):
    n = k_ref.shape[0]
    z = z_ref[...]
    X = jnp.dot(mf_ref[...], z.astype(BF16), preferred_element_type=F32)
    Hs = jnp.dot(mfr_ref[...], k_ref[...].astype(BF16), preferred_element_type=F32)
    xr, xi, hr, hi = X[:n], X[n:], Hs[:n], Hs[n:]
    y = jnp.concatenate([xr * hr - xi * hi, xr * hi + xi * hr], 0).astype(BF16)
    o_ref[...] = gate_ref[...] * (jnp.dot(mi_ref[...], y, preferred_element_type=F32) + bias_ref[...] * z)


def short_conv(z, gate, k, bias):
    B, L, C = z.shape
    assert B == 2
    mf, mfr, mi = dense_dft_constants(L)
    y = pl.pallas_call(
        _short_conv_kernel,
        out_shape=jax.ShapeDtypeStruct((2 * L, C), F32),
        name="hyena_short_conv")(mf, mfr, mi, z.reshape(2 * L, C), k, bias.reshape(1, C), gate.reshape(2 * L, C))
    return y.reshape(B, L, C)


HALO = 8


def _hy_pre_kernel(x_ref, prev_ref, next_ref, w_ref, b_ref, v_ref, x1_ref, x2_ref, *, nt):
    t = pl.program_id(1)
    x = x_ref[0].astype(F32)
    row = lax.broadcasted_iota(jnp.int32, (ROW_TILE, 1), 0)
    prev_row = jnp.where(t > 0, prev_ref[0, HALO - 1:HALO].astype(F32), 0.0)
    next_row = jnp.where(t < nt - 1, next_ref[0, 0:1].astype(F32), 0.0)
    below = jnp.where(row == 0, prev_row, pltpu.roll(x, 1, 0))
    above = jnp.where(row == ROW_TILE - 1, next_row, pltpu.roll(x, ROW_TILE - 1, 0))
    y = below * w_ref[0:1] + x * w_ref[1:2] + above * w_ref[2:3] + b_ref[...]
    v_ref[0] = y[:, :HY_W]
    x1_ref[0] = y[:, HY_W:2 * HY_W]
    x2_ref[0] = y[:, 2 * HY_W:]


def hyena_pre(P, t0, nt, conv_w, conv_b):
    B, T, _ = P.shape
    cb = COL['hy'] // N_HY
    per = ROW_TILE // HALO
    last_blk = T // HALO - 1
    out = pl.BlockSpec((1, ROW_TILE, HY_W), lambda b, t: (b, t, 0))
    return pl.pallas_call(
        functools.partial(_hy_pre_kernel, nt=nt), grid=(B, nt),
        in_specs=[pl.BlockSpec((1, ROW_TILE, N_HY), lambda b, t: (b, t + t0, cb)),
                  pl.BlockSpec((1, HALO, N_HY), lambda b, t: (b, jnp.maximum((t + t0) * per - 1, 0), cb)),
                  pl.BlockSpec((1, HALO, N_HY), lambda b, t: (b, jnp.minimum((t + t0 + 1) * per, last_blk), cb)),
                  pl.BlockSpec((3, N_HY), lambda b, t: (0, 0)),
                  pl.BlockSpec((1, N_HY), lambda b, t: (0, 0))],
        out_specs=[out, out, out],
        out_shape=[jax.ShapeDtypeStruct((B, nt * ROW_TILE, HY_W), F32)] * 3,
        compiler_params=pltpu.CompilerParams(dimension_semantics=("arbitrary", "arbitrary")),
        name="hyena_pre")(P, P, P, conv_w, conv_b.reshape(1, N_HY))


def hyena_branch(P, with_ctx, n_ctx, conv_w, conv_b, w1, b1, w2, b2, w3, freq, hbias):
    B, T, _ = P.shape
    consts = hyena_fft_constants()
    hy_w = (w1, b1, w2, b2, w3, freq)
    nc = n_ctx // ROW_TILE
    v, x1, x2 = hyena_pre(P, nc, T // ROW_TILE - nc, conv_w, conv_b)
    k = hyena_kernels(T - n_ctx, *hy_w)
    z = long_conv(v, x1, filter_spectrum(k[0], consts), hbias[0], consts)
    o = long_conv(z, x2, filter_spectrum(k[1], consts), hbias[1], consts)
    if with_ctx:
        vc, xc1, xc2 = hyena_pre(P, 0, nc, conv_w, conv_b)
        kc = hyena_kernels(n_ctx, *hy_w)
        oc = short_conv(short_conv(vc, xc1, kc[0], hbias[0]), xc2, kc[1], hbias[1])
    else:
        oc = jnp.zeros((B, n_ctx, HY_W), F32)
    return o, oc


ROUTER_PAD = 128


def _merge_kernel(of_ref, ob_ref, r_ref, om_ref, ohl_ref, ohc_ref, g_ref, x_ref, mod_ref,
                  wg_ref, wm_ref, wh_ref, wo_ref, gn_ref, l1g_ref, l1b_ref, rw_ref, rb_ref,
                  x1_ref, h2_ref, lg_ref, *, n_ctx_tiles):
    o = of_ref[0] + ob_ref[0]
    parts = [_rms(o[:, h * GLA_DV:(h + 1) * GLA_DV], gn_ref[...]) for h in range(GLA_HEADS)]
    og = (jnp.concatenate(parts, axis=1) * jax.nn.silu(r_ref[0].astype(F32))).astype(BF16)
    oh = jnp.where(pl.program_id(1) < n_ctx_tiles, ohc_ref[0], ohl_ref[0]).astype(BF16)
    D = D_MODEL
    g = g_ref[0]
    y = (jax.nn.sigmoid(g[:, :D].astype(F32)) * jnp.dot(og, wg_ref[...], preferred_element_type=F32)
         + jax.nn.sigmoid(g[:, D:2 * D].astype(F32)) * jnp.dot(om_ref[0], wm_ref[...], preferred_element_type=F32)
         + jax.nn.sigmoid(g[:, 2 * D:].astype(F32)) * jnp.dot(oh, wh_ref[...], preferred_element_type=F32))
    ym = jnp.dot(y.astype(BF16), wo_ref[...], preferred_element_type=F32)
    gate1, sh2, sc2 = mod_ref[0, 0, 0:1], mod_ref[0, 0, 1:2], mod_ref[0, 0, 2:3]
    x1 = _layer_norm(DEEPNORM_ALPHA * x_ref[0] + gate1 * ym) * l1g_ref[...] + l1b_ref[...]
    x1_ref[0] = x1
    h2f = _layer_norm(x1) * (1.0 + sc2) + sh2
    h2_ref[0] = pack_bf16_pairs(h2f)
    h2 = h2f.astype(BF16)
    lg_ref[0] = jnp.dot(h2, rw_ref[...], preferred_element_type=F32) + rb_ref[...]


def merge_block(o_f, o_b, P, o_mla, o_hy_lat, o_hy_ctx, xa, mod, w_br_gla, w_br_mla, w_br_hy, w_out, gla_norm,
                ln1_g, ln1_b, router_w, router_b, n_ctx):
    B, T, D = xa.shape
    NC = n_ctx // ROW_TILE
    bf = lambda a: a.astype(BF16)
    row = lambda a: a.reshape(1, -1)
    rw = jnp.pad(router_w, ((0, 0), (0, ROUTER_PAD - N_EXPERTS))).astype(BF16)
    rb = jnp.pad(router_b, (0, ROUTER_PAD - N_EXPERTS)).reshape(1, -1)
    full = lambda a: pl.BlockSpec(a.shape, lambda b, t: (0,) * a.ndim)
    tile = lambda w: pl.BlockSpec((1, ROW_TILE, w), lambda b, t: (b, t, 0))
    lat = pl.BlockSpec((1, ROW_TILE, BRANCH_W), lambda b, t: (b, jnp.maximum(t - NC, 0), 0))
    ctx = pl.BlockSpec((1, ROW_TILE, BRANCH_W), lambda b, t: (b, jnp.minimum(t, NC - 1), 0))
    ws = [bf(w_br_gla), bf(w_br_mla), bf(w_br_hy), bf(w_out), row(gla_norm), row(ln1_g), row(ln1_b), rw, rb]
    return pl.pallas_call(
        functools.partial(_merge_kernel, n_ctx_tiles=NC), grid=(B, T // ROW_TILE),
        in_specs=[tile(BRANCH_W), tile(BRANCH_W), pcol('r'), tile(BRANCH_W), lat, ctx, pcol('gate'), tile(D),
                  pl.BlockSpec((1, 1, 3, D), _tile_kind(NC))] + [full(w) for w in ws],
        out_specs=[tile(D), tile(HALF_D), tile(ROUTER_PAD)],
        out_shape=[jax.ShapeDtypeStruct((B, T, D), F32), jax.ShapeDtypeStruct((B, T, HALF_D), U32),
                   jax.ShapeDtypeStruct((B, T, ROUTER_PAD), F32)],
        compiler_params=pltpu.CompilerParams(dimension_semantics=("arbitrary", "arbitrary"),
                                             vmem_limit_bytes=48 * 1024 * 1024),
        name="merge_block")(o_f, o_b, P, o_mla, o_hy_lat, o_hy_ctx, P, xa, mod, *ws)


def kernel(x, c, ctx, c_ctx, ada_w, ada_b, w_in, gla_wa2_f, gla_ba_f, gla_wa2_b, gla_ba_b, gla_norm,
           mla_q_norm, mla_w_uq, mla_kv_norm, mla_w_ukv, hy_conv_w, hy_conv_b, hy_w1, hy_b1, hy_w2, hy_b2,
           hy_w3, hy_freq, hy_bias, w_br_gla, w_br_mla, w_br_hy, w_out, ln1_g, ln1_b, ln2_g, ln2_b,
           router_w, router_b, moe_w1, moe_b1, moe_w2, moe_b2):
    B, L, D = x.shape
    CL = ctx.shape[1]
    T = CL + L
    tables = rope_tables(L, CL)
    xa = jnp.concatenate([ctx, x], axis=1)
    cond = jnp.concatenate([jax.nn.silu(c), jax.nn.silu(c_ctx)[None], jnp.zeros((8 - B - 1, D), F32)], 0)
    for l in range(DEPTH):
        last = l == DEPTH - 1
        ada = pmm(cond, ada_w[l])[:B + 1] + ada_b[l]
        ada = jnp.stack([ada[:B], jnp.broadcast_to(ada[B:], (B, 6 * D))], 1).reshape(B, 2, 6, D)
        P = ln_mod_proj(xa, ada[:, :, 0:2], permute_w_in(w_in[l]), CL)
        o_f, o_b = gla_scan(P, gla_wa2_f[l], gla_ba_f[l], gla_wa2_b[l], gla_ba_b[l], CL)
        q, k, v = mla_prep(P, tables, mla_weights(mla_w_uq[l], mla_w_ukv[l]), mla_q_norm[l], mla_kv_norm[l])
        o_mla = mla_attention(q, k, v, CL)
        o_hy, o_hy_ctx = hyena_branch(P, not last, CL, hy_conv_w[l], hy_conv_b[l], hy_w1[l], hy_b1[l], hy_w2[l],
                                      hy_b2[l], hy_w3[l], hy_freq[l], hy_bias[l])
        x1, h2, logits = merge_block(o_f, o_b, P, o_mla, o_hy, o_hy_ctx, xa, ada[:, :, 2:5],
                                     w_br_gla[l], w_br_mla[l], w_br_hy[l], w_out[l], gla_norm[l],
                                     ln1_g[l], ln1_b[l], router_w[l], router_b[l], CL)
        t0 = CL // ROW_TILE if last else 0
        nt = T // ROW_TILE - t0
        xa = moe_ffn_residual(h2, logits[:, t0 * ROW_TILE:, :N_EXPERTS], x1, ada[:, :, 5:6],
                              moe_w1[l], moe_b1[l], moe_w2[l], moe_b2[l], ln2_g[l], ln2_b[l],
                              t0, nt, CL // ROW_TILE)
    return xa
```

```python
import functools
import math
import jax
import jax.numpy as jnp
from jax import lax
import numpy as np
from jax.experimental import pallas as pl
from jax.experimental.pallas import tpu as pltpu

D_MODEL = 1024
DEPTH = 2
GRID_W = 64
BRANCH_W = D_MODEL // 2
N_BRANCH = 3
GLA_HEADS = 4
GLA_DV = BRANCH_W // GLA_HEADS
GLA_DK = GLA_DV // 2
GLA_RANK = 16
GLA_TAU = 16.0
GLA_CHUNK = 16
MLA_HEADS = 8
MLA_DV = BRANCH_W // MLA_HEADS
MLA_NOPE = MLA_DV
MLA_ROPE = MLA_NOPE // 2
MLA_Q_RANK = 3 * D_MODEL // 8
MLA_KV_RANK = D_MODEL // 4
ROPE_BASE = 10000.0
Q_BLOCK = 128
HY_W = BRANCH_W
HY_ORDER = 2
HY_DIRS = 2
HY_BANDS = 16
HY_EMB = 2 * HY_BANDS + 1
HY_FFN = 64
HY_DECAY_TARGET = 1e-2
HY_FAST_PCT = 0.3
HY_SLOW_PCT = 1.5
N_EXPERTS = 32
TOP_K = 4
D_FF = D_MODEL
SWIGLU_LIMIT = 7.0
SWIGLU_ALPHA = 1.702
MOE_BLOCK = 128
LN_EPS = 1e-5
RMS_EPS = 1e-6
DEEPNORM_ALPHA = (2 * DEPTH) ** 0.25
IN_SPLITS = (
    GLA_HEADS * GLA_DK, GLA_HEADS * GLA_DV, GLA_RANK, GLA_RANK, MLA_KV_RANK, MLA_ROPE,
    GLA_HEADS * GLA_DK, GLA_HEADS * GLA_DV, MLA_Q_RANK, 3 * HY_W, N_BRANCH * D_MODEL,
)
N_KEY_GROUPS = 6
KEY_COLS = sum(IN_SPLITS[:N_KEY_GROUPS])
IN_OFFSETS = tuple(int(o) for o in np.cumsum(IN_SPLITS)[:-1])
IN_TOTAL = sum(IN_SPLITS)
F32 = jnp.float32
BF16 = jnp.bfloat16
U32 = jnp.uint32


def _mm_kernel(a_ref, w_ref, o_ref):
    o_ref[...] = jnp.dot(a_ref[...], w_ref[...], preferred_element_type=F32)


def _pick(n, cands):
    for c in cands:
        if n % c == 0:
            return c
    return n


def pmm(a, w):
    lead = a.shape[:-1]
    K = a.shape[-1]
    N = w.shape[-1]
    a2 = a.reshape(-1, K).astype(BF16)
    M = a2.shape[0]
    N0 = N
    if N % 128:
        N = -(-N // 256) * 256
        w = jnp.pad(w, ((0, 0), (0, N - N0)))
    tm = _pick(M, (512, 256, 128, 64, 32, 16, 8))
    tn = _pick(N, (512, 256, 128))
    out = pl.pallas_call(
        _mm_kernel,
        grid=(M // tm, N // tn),
        in_specs=[pl.BlockSpec((tm, K), lambda i, j: (i, 0)),
                  pl.BlockSpec((K, tn), lambda i, j: (0, j))],
        out_specs=pl.BlockSpec((tm, tn), lambda i, j: (i, j)),
        out_shape=jax.ShapeDtypeStruct((M, N), F32),
    )(a2, w.astype(BF16))
    return out[:, :N0].reshape(lead + (N0,))


ROW_TILE = 256
HEAD_PAD = 128
N_GATE = N_BRANCH * D_MODEL
N_HY = 3 * HY_W
COL = dict(gate=0, hy=N_GATE, qa=N_GATE + N_HY)
COL['misc'] = COL['qa'] + MLA_Q_RANK
COL['v'] = COL['misc'] + HEAD_PAD
COL['r'] = COL['v'] + GLA_HEADS * GLA_DV
COL['k'] = COL['r'] + GLA_HEADS * GLA_DV
COL['q'] = COL['k'] + GLA_HEADS * GLA_DK
COL['kva'] = COL['q'] + GLA_HEADS * GLA_DK
P_COLS = COL['kva'] + MLA_KV_RANK
WIDTH = dict(gate=N_GATE, hy=N_HY, qa=MLA_Q_RANK, misc=HEAD_PAD, v=GLA_HEADS * GLA_DV, r=GLA_HEADS * GLA_DV,
             k=GLA_HEADS * GLA_DK, q=GLA_HEADS * GLA_DK, kva=MLA_KV_RANK)
MISC_AF, MISC_AB, MISC_KR = 0, GLA_RANK, MLA_NOPE
P_CHUNK = 768
assert all(COL[g] % WIDTH[g] == 0 for g in COL) and P_COLS % P_CHUNK == 0


def permute_w_in(w_in):
    o = (0,) + IN_OFFSETS + (IN_TOTAL,)
    k, v, af, ab, kva, kr, q, r, qa, hy, gate = [w_in[:, o[i]:o[i + 1]] for i in range(len(IN_SPLITS))]
    z = lambda n: jnp.zeros((w_in.shape[0], n), w_in.dtype)
    misc = jnp.concatenate([af, ab, z(MISC_KR - 2 * GLA_RANK), kr, z(HEAD_PAD - MISC_KR - MLA_ROPE)], 1)
    return jnp.concatenate([gate, hy, qa, misc, v, r, k, q, kva], 1).astype(BF16)


def _layer_norm(x):
    xc = x - jnp.mean(x, axis=-1, keepdims=True)
    return xc * lax.rsqrt(jnp.mean(xc * xc, axis=-1, keepdims=True) + LN_EPS)


def _rms(x, gain):
    return x * lax.rsqrt(jnp.mean(x * x, axis=-1, keepdims=True) + RMS_EPS) * gain


def _proj_kernel(x_ref, mod_ref, w_ref, o_ref):
    h = _layer_norm(x_ref[0]) * (1.0 + mod_ref[0, 0, 1:2]) + mod_ref[0, 0, 0:1]
    hb = h.astype(BF16)
    for c in range(0, P_COLS, P_CHUNK):
        o_ref[0, :, c:c + P_CHUNK] = jnp.dot(hb, w_ref[:, c:c + P_CHUNK], preferred_element_type=F32).astype(BF16)


def _tile_kind(n_ctx_tiles):
    return lambda b, t: (b, jnp.where(t < n_ctx_tiles, 1, 0), 0, 0)


def ln_mod_proj(xa, mod, w_p, n_ctx):
    B, T, D = xa.shape
    return pl.pallas_call(
        _proj_kernel, grid=(B, T // ROW_TILE),
        in_specs=[pl.BlockSpec((1, ROW_TILE, D), lambda b, t: (b, t, 0)),
                  pl.BlockSpec((1, 1, 2, D), _tile_kind(n_ctx // ROW_TILE)),
                  pl.BlockSpec((D, P_COLS), lambda b, t: (0, 0), pipeline_mode=pl.Buffered(1))],
        out_specs=pl.BlockSpec((1, ROW_TILE, P_COLS), lambda b, t: (b, t, 0)),
        out_shape=jax.ShapeDtypeStruct((B, T, P_COLS), BF16),
        compiler_params=pltpu.CompilerParams(dimension_semantics=("arbitrary", "arbitrary"),
                                             vmem_limit_bytes=48 * 1024 * 1024),
        name="ln_mod_proj")(xa, mod, w_p)


def pcol(group, row_map=lambda b, t: t):
    w = WIDTH[group]
    return pl.BlockSpec((1, ROW_TILE, w), lambda b, t, _i=COL[group] // w: (b, row_map(b, t), _i))


ONES_LANE_EVEN = MLA_DV
ONES_LANE_ODD = 0
LOG2E = 1.4426950408889634


def rope_tables(L, n_ctx):
    rows = L // GRID_W
    row = jnp.repeat(jnp.arange(rows, dtype=F32), GRID_W)
    col = jnp.tile(jnp.arange(GRID_W, dtype=F32), rows)
    n_freq = MLA_ROPE // 4
    inv = ROPE_BASE ** (-jnp.arange(n_freq, dtype=F32) / n_freq)
    ang = jnp.concatenate([row[:, None] * inv, col[:, None] * inv], -1)
    ang = jnp.concatenate([jnp.zeros((n_ctx, ang.shape[1]), F32), ang], 0)
    cos, sin = jnp.cos(ang), jnp.sin(ang)
    T = ang.shape[0]
    half = MLA_ROPE // 2
    z = lambda n: jnp.zeros((T, n), F32)
    ct = jnp.concatenate([jnp.ones((T, MLA_NOPE), F32), cos, cos, z(HEAD_PAD - MLA_NOPE - MLA_ROPE)], 1)
    sa = jnp.concatenate([z(MLA_NOPE), -sin, z(HEAD_PAD - MLA_NOPE - half)], 1)
    sb = jnp.concatenate([z(MLA_NOPE + half), sin, z(HEAD_PAD - MLA_NOPE - MLA_ROPE)], 1)
    return ct, sa, sb


def mla_weights(w_uq, w_ukv):
    H = MLA_HEADS
    wq = w_uq.reshape(MLA_Q_RANK, H, MLA_NOPE + MLA_ROPE)
    wq = jnp.pad(wq, ((0, 0), (0, 0), (0, HEAD_PAD - MLA_NOPE - MLA_ROPE))).reshape(MLA_Q_RANK, H * HEAD_PAD)
    wkv = w_ukv.reshape(MLA_KV_RANK, H, MLA_NOPE + MLA_DV)
    wk = jnp.pad(wkv[:, :, :MLA_NOPE], ((0, 0), (0, 0), (0, HEAD_PAD - MLA_NOPE))).reshape(MLA_KV_RANK, H * HEAD_PAD)
    wv = wkv[:, :, MLA_NOPE:].reshape(MLA_KV_RANK, H // 2, 2, MLA_DV)
    z = jnp.zeros_like(wv[:, :, 0])
    wv = jnp.stack([jnp.concatenate([wv[:, :, 0], z], -1), jnp.concatenate([z, wv[:, :, 1]], -1)], 2)
    ones = np.zeros((H // 2, 2, HEAD_PAD), np.float32)
    ones[:, 0, ONES_LANE_EVEN] = 1.0
    ones[:, 1, ONES_LANE_ODD] = 1.0
    return (wq.astype(BF16), wk.astype(BF16), wv.reshape(MLA_KV_RANK, H * HEAD_PAD).astype(BF16),
            jnp.asarray(ones.reshape(1, H * HEAD_PAD)))


def _mla_prep_kernel(qa_ref, kva_ref, misc_ref, ct_ref, sa_ref, sb_ref, wq_ref, wk_ref, wv_ref, ones_ref,
                     qn_ref, kn_ref, q_ref, k_ref, v_ref):
    ct, sa, sb = ct_ref[...], sa_ref[...], sb_ref[...]
    half = MLA_ROPE // 2
    rope = lambda x: x * ct + pltpu.roll(x, HEAD_PAD - half, 1) * sa + pltpu.roll(x, half, 1) * sb
    qn = _rms(qa_ref[0].astype(F32), qn_ref[...]).astype(BF16)
    q = jnp.dot(qn, wq_ref[...], preferred_element_type=F32)
    kn = _rms(kva_ref[0].astype(F32), kn_ref[...]).astype(BF16)
    k = jnp.dot(kn, wk_ref[...], preferred_element_type=F32)
    v_ref[0] = (jnp.dot(kn, wv_ref[...], preferred_element_type=F32) + ones_ref[...]).astype(BF16)
    lane = lax.broadcasted_iota(jnp.int32, (1, HEAD_PAD), 1)
    is_kr = (lane >= MISC_KR) & (lane < MISC_KR + MLA_ROPE)
    kr = rope(jnp.where(is_kr, misc_ref[0].astype(F32), 0.0))
    qscale = (MLA_NOPE + MLA_ROPE) ** -0.5 * LOG2E
    for h in range(MLA_HEADS):
        hs = slice(h * HEAD_PAD, (h + 1) * HEAD_PAD)
        q_ref[0, :, hs] = (rope(q[:, hs]) * qscale).astype(BF16)
        k_ref[0, :, hs] = (k[:, hs] + kr).astype(BF16)


def mla_prep(P, tables, weights, q_norm, kv_norm):
    B, T, _ = P.shape
    ct, sa, sb = tables
    wq, wk, wv, ones = weights
    full = lambda a: pl.BlockSpec(a.shape, lambda b, t: (0,) * a.ndim)
    tab = pl.BlockSpec((ROW_TILE, HEAD_PAD), lambda b, t: (t, 0))
    qn, kn = q_norm.reshape(1, -1), kv_norm.reshape(1, -1)
    W = MLA_HEADS * HEAD_PAD
    out = pl.BlockSpec((1, ROW_TILE, W), lambda b, t: (b, t, 0))
    return pl.pallas_call(
        _mla_prep_kernel, grid=(B, T // ROW_TILE),
        in_specs=[pcol('qa'), pcol('kva'), pcol('misc'), tab, tab, tab, full(wq), full(wk), full(wv), full(ones),
                  full(qn), full(kn)],
        out_specs=[out, out, out],
        out_shape=[jax.ShapeDtypeStruct((B, T, W), BF16)] * 3,
        compiler_params=pltpu.CompilerParams(dimension_semantics=("arbitrary", "arbitrary")),
        name="mla_prep")(P, P, P, ct, sa, sb, wq, wk, wv, ones, qn, kn)


ATT_TQ = 256
ATT_TK = 768


def _attn_kernel(q_ref, k_ref, v_ref, o_ref, s_buf, p_buf, m_buf, a_buf, mrun, acc, *, n_ctx, tk):
    tq = q_ref.shape[1]
    T = k_ref.shape[1]
    in_ctx = pl.program_id(2) * tq < n_ctx
    lane = lax.broadcasted_iota(jnp.int32, (tq, HEAD_PAD), 1)
    hs = lambda h: slice(h * HEAD_PAD, (h + 1) * HEAD_PAD)

    def stage_a(c, slot, size):
        start = c * size
        for h in range(2):
            k = k_ref[0, pl.ds(start, size), hs(h)]
            s = lax.dot_general(q_ref[0, :, hs(h)], k, (((1,), (1,)), ((), ())), preferred_element_type=F32)
            s_buf[slot, h, :, :size] = s
            m_prev = mrun[h]
            m_new = jnp.maximum(m_prev, jnp.max(s, axis=-1, keepdims=True))
            mrun[h] = m_new
            m_buf[slot, h] = m_new
            a_buf[slot, h] = jnp.exp2(m_prev - m_new)

    def stage_b(slot, size):
        for h in range(2):
            p_buf[slot, h, :, :size] = jnp.exp2(s_buf[slot, h, :, :size] - m_buf[slot, h]).astype(BF16)

    def stage_c(c, slot, size):
        start = c * size
        for h in range(2):
            v = v_ref[0, pl.ds(start, size), hs(h)]
            acc[h] = a_buf[slot, h] * acc[h] + jnp.dot(p_buf[slot, h, :, :size], v, preferred_element_type=F32)

    def run(n, size):
        for h in range(2):
            mrun[h] = jnp.full((tq, 1), -jnp.inf, F32)
            acc[h] = jnp.zeros((tq, HEAD_PAD), F32)
        if n == 1:
            stage_a(0, 0, size)
            stage_b(0, size)
            stage_c(0, 0, size)
        else:
            stage_a(0, 0, size)
            stage_b(0, size)
            stage_a(1, 1, size)

            for c in range(n - 2):
                stage_c(c, c % 2, size)
                stage_b(1 - c % 2, size)
                stage_a(c + 2, c % 2, size)
            stage_c(n - 2, (n - 2) % 2, size)
            stage_b((n - 1) % 2, size)
            stage_c(n - 1, (n - 1) % 2, size)
        a0, a1 = acc[0], acc[1]
        l0 = a0[:, ONES_LANE_EVEN:ONES_LANE_EVEN + 1]
        l1 = a1[:, ONES_LANE_ODD:ONES_LANE_ODD + 1]
        o_ref[0] = jnp.where(lane < MLA_DV, a0 / l0, a1 / l1).astype(o_ref.dtype)

    @pl.when(in_ctx)
    def _():
        run(1, n_ctx)

    @pl.when(jnp.logical_not(in_ctx))
    def _():
        run(T // tk, tk)


def mla_attention(q, k, v, n_ctx):
    B, T, _ = q.shape
    assert T % ATT_TQ == 0 and T % ATT_TK == 0 and n_ctx % ATT_TQ == 0 and n_ctx <= ATT_TK
    kern = functools.partial(_attn_kernel, n_ctx=n_ctx, tk=ATT_TK)
    return pl.pallas_call(
        kern,
        grid=(B, MLA_HEADS // 2, T // ATT_TQ),
        in_specs=[pl.BlockSpec((1, ATT_TQ, 2 * HEAD_PAD), lambda b, h, i: (b, i, h)),
                  pl.BlockSpec((1, T, 2 * HEAD_PAD), lambda b, h, i: (b, 0, h)),
                  pl.BlockSpec((1, T, 2 * HEAD_PAD), lambda b, h, i: (b, 0, h))],
        out_specs=pl.BlockSpec((1, ATT_TQ, 2 * MLA_DV), lambda b, h, i: (b, i, h)),
        out_shape=jax.ShapeDtypeStruct((B, T, MLA_HEADS * MLA_DV), BF16),
        scratch_shapes=[pltpu.VMEM((2, 2, ATT_TQ, ATT_TK), F32), pltpu.VMEM((2, 2, ATT_TQ, ATT_TK), BF16),
                        pltpu.VMEM((2, 2, ATT_TQ, 1), F32), pltpu.VMEM((2, 2, ATT_TQ, 1), F32),
                        pltpu.VMEM((2, ATT_TQ, 1), F32), pltpu.VMEM((2, ATT_TQ, HEAD_PAD), F32)],
        compiler_params=pltpu.CompilerParams(
            dimension_semantics=("arbitrary", "arbitrary", "arbitrary"),
            vmem_limit_bytes=56 * 1024 * 1024),
        name="mla_attention",
    )(q, k, v)


MOE_TM = 256
HALF_D = D_MODEL // 2
HI_MASK = 0xFFFF0000
N_ROW_DMA = ROW_TILE * TOP_K


def pack_bf16_pairs(x):
    u = pltpu.bitcast(x.astype(BF16).astype(F32), U32)
    return (u[:, :HALF_D] >> 16) | (u[:, HALF_D:] & jnp.uint32(HI_MASK))


def unpack_bf16_pairs(w):
    lo = pltpu.bitcast(w << 16, F32)
    hi = pltpu.bitcast(w & jnp.uint32(HI_MASK), F32)
    return jnp.concatenate([lo, hi], axis=1).astype(BF16)


def moe_route(logits):
    N = logits.shape[0]
    top_val, top_idx = lax.top_k(logits, TOP_K)
    gates = jax.nn.softmax(top_val, axis=-1)
    NK = N * TOP_K
    flat_e = top_idx.reshape(NK)
    onehot = (flat_e[:, None] == jnp.arange(N_EXPERTS, dtype=flat_e.dtype)[None, :]).astype(F32)
    nb = NK // MOE_TM
    within = jnp.einsum('ij,bje->bie', jnp.tril(jnp.ones((MOE_TM, MOE_TM), F32)),
                        onehot.reshape(nb, MOE_TM, N_EXPERTS))
    btot = within[:, -1, :].astype(jnp.int32)
    boff = jnp.cumsum(btot, axis=0) - btot
    csum = (within.astype(jnp.int32) + boff[:, None, :]).reshape(NK, N_EXPERTS)
    rank = jnp.take_along_axis(csum, flat_e[:, None], axis=1)[:, 0] - 1
    counts = csum[-1]
    padded = (counts + MOE_TM - 1) // MOE_TM * MOE_TM
    pad_end = jnp.cumsum(padded)
    dest = ((pad_end - padded)[flat_e] + rank).astype(jnp.int32)
    n_blocks = NK // MOE_TM + N_EXPERTS
    blk_start = jnp.arange(n_blocks, dtype=jnp.int32) * MOE_TM
    block_e = jnp.minimum(jnp.searchsorted(pad_end, blk_start, side='right'), N_EXPERTS - 1).astype(jnp.int32)
    first = jnp.concatenate([jnp.ones((1,), jnp.int32), (block_e[1:] != block_e[:-1]).astype(jnp.int32)])
    n_used = (pad_end[-1:] // MOE_TM).astype(jnp.int32)
    return dest, gates, block_e, first, n_used


def _dispatch_kernel(dest_ref, h_ref, xb_in_ref, xb_ref, sem):
    del xb_in_ref

    def start(r, c):
        for k in range(TOP_K):
            pltpu.make_async_copy(h_ref.at[0, pl.ds(r, 1)], xb_ref.at[pl.ds(dest_ref[r * TOP_K + k], 1)],
                                  sem).start()
        return c

    lax.fori_loop(0, ROW_TILE, start, 0, unroll=4)
    for k in range(TOP_K):
        pltpu.make_async_copy(h_ref.at[0], xb_ref.at[pl.ds(0, ROW_TILE)], sem).wait()


def moe_dispatch(h2p, dest, n_rows, t0, nt):
    B = h2p.shape[0]
    xb0 = jnp.zeros((n_rows, HALF_D), U32)
    return pl.pallas_call(
        _dispatch_kernel, grid=(B, nt),
        in_specs=[pl.BlockSpec((N_ROW_DMA,), lambda b, t: (b * nt + t,), memory_space=pltpu.SMEM),
                  pl.BlockSpec((1, ROW_TILE, HALF_D), lambda b, t: (b, t + t0, 0)),
                  pl.BlockSpec(memory_space=pl.ANY)],
        out_specs=pl.BlockSpec(memory_space=pl.ANY),
        out_shape=jax.ShapeDtypeStruct((n_rows, HALF_D), U32),
        scratch_shapes=[pltpu.SemaphoreType.DMA],
        input_output_aliases={2: 0},
        compiler_params=pltpu.CompilerParams(dimension_semantics=("arbitrary", "arbitrary")),
        name="moe_dispatch")(dest, h2p, xb0)


def _moe_kernel(be_ref, first_ref, nb_ref, x_ref, w1_ref, b1_ref, w2_ref, b2_ref, o_ref, w1_bf, w2_bf):
    i = pl.program_id(0)

    @pl.when(jnp.logical_and(i < nb_ref[0], first_ref[i] == 1))
    def _():
        w1_bf[...] = w1_ref[0].astype(BF16)
        w2_bf[...] = w2_ref[0].astype(BF16)

    @pl.when(i < nb_ref[0])
    def _():
        x = unpack_bf16_pairs(x_ref[...])
        hcat = jnp.dot(x, w1_bf[...], preferred_element_type=F32) + b1_ref[0]
        glu = jnp.minimum(hcat[:, :D_FF], SWIGLU_LIMIT)
        lin = jnp.clip(hcat[:, D_FF:], -SWIGLU_LIMIT, SWIGLU_LIMIT)
        act = glu * jax.nn.sigmoid(SWIGLU_ALPHA * glu) * (lin + 1)
        o_ref[...] = jnp.dot(act.astype(BF16), w2_bf[...], preferred_element_type=F32) + b2_ref[0]

    @pl.when(i >= nb_ref[0])
    def _():
        o_ref[...] = jnp.zeros_like(o_ref)


def moe_experts(xb, block_e, first, n_used, w1, b1, w2, b2):
    R = xb.shape[0]
    D = D_MODEL
    n_blocks = R // MOE_TM
    grid_spec = pltpu.PrefetchScalarGridSpec(
        num_scalar_prefetch=3,
        grid=(n_blocks,),
        in_specs=[pl.BlockSpec((MOE_TM, HALF_D), lambda i, be, fi, nb: (i, 0)),
                  pl.BlockSpec((1, D, 2 * D_FF), lambda i, be, fi, nb: (be[i], 0, 0)),
                  pl.BlockSpec((1, 1, 2 * D_FF), lambda i, be, fi, nb: (be[i], 0, 0)),
                  pl.BlockSpec((1, D_FF, D), lambda i, be, fi, nb: (be[i], 0, 0)),
                  pl.BlockSpec((1, 1, D), lambda i, be, fi, nb: (be[i], 0, 0))],
        out_specs=pl.BlockSpec((MOE_TM, D), lambda i, be, fi, nb: (i, 0)),
        scratch_shapes=[pltpu.VMEM((D, 2 * D_FF), BF16), pltpu.VMEM((D_FF, D), BF16)],
    )
    return pl.pallas_call(
        _moe_kernel,
        grid_spec=grid_spec,
        out_shape=jax.ShapeDtypeStruct((R, D), F32),
        compiler_params=pltpu.CompilerParams(
            dimension_semantics=("arbitrary",), vmem_limit_bytes=56 * 1024 * 1024),
        name="moe_experts",
    )(block_e, first, n_used, xb, w1, b1[:, None, :], w2, b2[:, None, :])


def _combine_kernel(dest_ref, x_ref, gate_ref, mod_ref, lg_ref, lb_ref, yb_ref, o_ref, buf, sem):
    def start(r, c):
        for k in range(TOP_K):
            pltpu.make_async_copy(yb_ref.at[pl.ds(dest_ref[r * TOP_K + k], 1)], buf.at[k, pl.ds(r, 1)], sem).start()
        return c

    lax.fori_loop(0, ROW_TILE, start, 0, unroll=4)
    for k in range(TOP_K):
        pltpu.make_async_copy(yb_ref.at[pl.ds(0, ROW_TILE)], buf.at[k], sem).wait()
    g = gate_ref[...]
    f = g[:, 0:1] * buf[0]
    for k in range(1, TOP_K):
        f = f + g[:, k:k + 1] * buf[k]
    o_ref[0] = _layer_norm(DEEPNORM_ALPHA * x_ref[0] + mod_ref[0, 0, 0:1] * f) * lg_ref[...] + lb_ref[...]


def moe_combine(yb, dest, gates, x1, mod, ln_g, ln_b, t0, nt, n_ctx_tiles):
    B, T, D = x1.shape
    row = lambda a: a.reshape(1, -1)
    full = lambda a: pl.BlockSpec(a.shape, lambda b, t: (0,) * a.ndim)
    return pl.pallas_call(
        _combine_kernel, grid=(B, nt),
        in_specs=[pl.BlockSpec((N_ROW_DMA,), lambda b, t: (b * nt + t,), memory_space=pltpu.SMEM),
                  pl.BlockSpec((1, ROW_TILE, D), lambda b, t: (b, t + t0, 0)),
                  pl.BlockSpec((ROW_TILE, TOP_K), lambda b, t: (b * nt + t, 0)),
                  pl.BlockSpec((1, 1, 1, D), lambda b, t: (b, jnp.where(t + t0 < n_ctx_tiles, 1, 0), 0, 0)),
                  full(row(ln_g)), full(row(ln_b)),
                  pl.BlockSpec(memory_space=pl.ANY)],
        out_specs=pl.BlockSpec((1, ROW_TILE, D), lambda b, t: (b, t, 0)),
        out_shape=jax.ShapeDtypeStruct((B, nt * ROW_TILE, D), F32),
        scratch_shapes=[pltpu.VMEM((TOP_K, ROW_TILE, D), F32), pltpu.SemaphoreType.DMA],
        compiler_params=pltpu.CompilerParams(dimension_semantics=("arbitrary", "arbitrary")),
        name="moe_combine")(dest, x1, gates, mod, row(ln_g), row(ln_b), yb)


def moe_ffn_residual(h2p, logits, x1, gate2, w1, b1, w2, b2, ln_g, ln_b, t0, nt, n_ctx_tiles):
    B = h2p.shape[0]
    N = B * nt * ROW_TILE
    dest, gates, block_e, first, n_used = moe_route(logits.reshape(N, N_EXPERTS))
    n_rows = (N * TOP_K // MOE_TM + N_EXPERTS) * MOE_TM
    xb = moe_dispatch(h2p, dest, n_rows, t0, nt)
    yb = moe_experts(xb, block_e, first, n_used, w1, b1, w2, b2)
    return moe_combine(yb, dest, gates, x1, gate2, ln_g, ln_b, t0, nt, n_ctx_tiles)


GLA_TILE = 256
GLA_BASE = 16
GLA_LEVELS = (16, 32, 64, 128)
N_SEL = 2 * (len(GLA_LEVELS) + 1)


def gla_constants():
    n = GLA_TILE
    i = np.arange(n)[:, None]
    t = np.arange(n)[None, :]
    sizes = GLA_LEVELS + (n,)
    P = [((t <= i) & (t // s == i // s)) for s in sizes]
    S = [((t > i) & (t // s == i // s)) for s in sizes]
    sel_f = np.concatenate(P + S, 0)
    sel_b = np.concatenate([p.T for p in P] + [s.T for s in S], 0)
    lv = [((i // (2 * s) == t // (2 * s)) & ((i // s) % 2 == 1) & ((t // s) % 2 == 0)) for s in GLA_LEVELS]
    dg = (i // GLA_BASE == t // GLA_BASE) & (t <= i)
    msk_f = np.stack(lv + [dg], 0)
    msk_b = np.stack([m.T for m in lv] + [dg.T], 0)
    return (jnp.asarray(np.stack([sel_f, sel_b], 0), BF16), jnp.asarray(np.stack([msk_f, msk_b], 0), F32))


def _gla_kernel(qf_ref, kf_ref, vf_ref, af_ref, qb_ref, kb_ref, vb_ref, ab_ref,
                wf_ref, bf_ref, wb_ref, bb_ref, sel_ref, msk_ref, of_ref, ob_ref, state):
    TL = GLA_TILE
    NL = len(GLA_LEVELS)

    @pl.when(pl.program_id(1) == 0)
    def _():
        state[...] = jnp.zeros_like(state)

    lane = lax.broadcasted_iota(jnp.int32, (1, 2 * GLA_DK), 1)
    nt = lambda a, b: lax.dot_general(a, b, (((1,), (1,)), ((), ())), preferred_element_type=F32)
    dirs = ((qf_ref, kf_ref, vf_ref, af_ref, wf_ref, bf_ref, of_ref, TL - 1),
            (qb_ref, kb_ref, vb_ref, ab_ref, wb_ref, bb_ref, ob_ref, 0))
    for d, (q_ref, k_ref, v_ref, a_ref, w_ref, b_ref, o_ref, last) in enumerate(dirs):
        x = jnp.dot(a_ref[0], w_ref[...], preferred_element_type=F32) + b_ref[...]
        g = jax.nn.log_sigmoid(x) * (1.0 / GLA_TAU)
        g1 = g.astype(BF16)
        g2 = (g - g1.astype(F32)).astype(BF16)
        sel = sel_ref[d]
        ps = jnp.dot(sel, g1, preferred_element_type=F32) + jnp.dot(sel, g2, preferred_element_type=F32)
        seg = lambda j: ps[j * TL:(j + 1) * TL]
        q = q_ref[0].astype(F32) * (GLA_DK ** -0.5)
        k = k_ref[0].astype(F32)
        e_pre = [jnp.exp(seg(j)) for j in range(NL + 1)]
        e_suf = [jnp.exp(seg(NL + 1 + j)) for j in range(NL + 1)]
        qh = [(q * e).astype(BF16) for e in e_pre]
        kh = [(k * e).astype(BF16) for e in e_suf]
        kd = (k * jnp.exp(-seg(0))).astype(BF16)
        dec = e_pre[NL][last:last + 1]
        for h in range(GLA_HEADS):
            pr = slice((h // 2) * 2 * GLA_DK, (h // 2 + 1) * 2 * GLA_DK)
            mine = (lane // GLA_DK) == (h % 2)
            qm = [jnp.where(mine, t[:, pr], 0) for t in qh[:NL]]
            sc = msk_ref[d, NL] * nt(qm[0], kd[:, pr])
            for j in range(NL):
                sc = sc + msk_ref[d, j] * nt(qm[j], kh[j][:, pr])
            v = v_ref[0, :, h * GLA_DV:(h + 1) * GLA_DV].astype(BF16)
            st = state[d, h]
            o = jnp.dot(sc.astype(BF16), v, preferred_element_type=F32) + nt(qh[NL][:, pr], st.astype(BF16))
            o_ref[0, :, h * GLA_DV:(h + 1) * GLA_DV] = o
            upd = lax.dot_general(v, kh[NL][:, pr], (((0,), (0,)), ((), ())), preferred_element_type=F32)
            state[d, h] = jnp.where(mine, dec[:, pr] * st + upd, 0)


def gla_scan(P, wa2_f, ba_f, wa2_b, ba_b, n_ctx):
    B, T, _ = P.shape
    assert GLA_TILE == ROW_TILE and T % ROW_TILE == 0 and n_ctx % ROW_TILE == 0
    NT, NC = T // ROW_TILE, n_ctx // ROW_TILE
    sel, msk = gla_constants()
    fwd = lambda b, n: n
    bwd = lambda b, n: jnp.where(n < NC, NC - 1 - n, NT - 1 - (n - NC))
    full = lambda a: pl.BlockSpec(a.shape, lambda b, n: (0,) * a.ndim)
    HK, HV = GLA_HEADS * GLA_DK, GLA_HEADS * GLA_DV
    pad = lambda w, off: jnp.zeros((HEAD_PAD, HK), F32).at[off:off + GLA_RANK].set(w).astype(BF16)
    wf, wb = pad(wa2_f, MISC_AF), pad(wa2_b, MISC_AB)
    ba_f2, ba_b2 = ba_f.reshape(1, HK), ba_b.reshape(1, HK)
    out = lambda rm: pl.BlockSpec((1, ROW_TILE, HV), lambda b, n: (b, rm(b, n), 0))
    return pl.pallas_call(
        _gla_kernel,
        grid=(B, NT),
        in_specs=[pcol('q', fwd), pcol('k', fwd), pcol('v', fwd), pcol('misc', fwd),
                  pcol('q', bwd), pcol('k', bwd), pcol('v', bwd), pcol('misc', bwd),
                  full(wf), full(ba_f2), full(wb), full(ba_b2), full(sel), full(msk)],
        out_specs=[out(fwd), out(bwd)],
        out_shape=[jax.ShapeDtypeStruct((B, T, HV), F32)] * 2,
        scratch_shapes=[pltpu.VMEM((2, GLA_HEADS, GLA_DV, 2 * GLA_DK), F32)],
        compiler_params=pltpu.CompilerParams(dimension_semantics=("arbitrary", "arbitrary"),
                                             vmem_limit_bytes=48 * 1024 * 1024),
        name="gla_scan",
    )(P, P, P, P, P, P, P, P, wf, ba_f2, wb, ba_b2, sel, msk)


def hyena_kernels(L, w1, b1, w2, b2, w3, freq):
    t = jnp.linspace(0.0, 1.0, L, dtype=F32)[:, None]
    w = 2 * math.pi * jnp.arange(L, dtype=F32)[:, None] / L
    f = jnp.linspace(1e-4, HY_BANDS - 1, HY_BANDS, dtype=F32)
    z = jnp.concatenate([t, jnp.cos(f * w), -jnp.sin(f * w)], -1)
    ext = lambda a: jnp.concatenate([a, a[:1], jnp.flip(a[1:], 0)], 0)
    z, t = ext(z), ext(t)
    fr = freq.astype(F32)
    h = jnp.sin(fr * (z @ w1.astype(F32) + b1.astype(F32)))
    h = jnp.sin(fr * (h @ w2.astype(F32) + b2.astype(F32)))
    h = (h @ w3.astype(F32)).reshape(2 * L, HY_ORDER, HY_DIRS, HY_W)
    deltas = jnp.abs(jnp.linspace(math.log(HY_DECAY_TARGET) / HY_SLOW_PCT,
                                  math.log(HY_DECAY_TARGET) / HY_FAST_PCT, HY_W, dtype=F32))
    h = h * jnp.exp(-t * deltas)[:, None, None, :]
    norm = jnp.sum(jnp.abs(h[:L]), axis=(0, 2))
    row = jnp.arange(2 * L)[:, None, None]
    k = jnp.where(row < L, h[:, :, 0], jnp.where(row > L, h[:, :, 1], 0.0))
    return jnp.moveaxis(k / (norm * (2 * L)), 1, 0)


FFT_R = 128
FFT_N = FFT_R * FFT_R
FFT_COLS = 2048


def _dft_parts(n):
    ang = 2.0 * np.pi * np.outer(np.arange(n), np.arange(n)) / n
    return np.cos(ang), np.sin(ang)


def hyena_fft_constants():
    C, S = _dft_parts(FFT_R)
    h = FFT_R // 2
    m_first = np.block([[C[:, :h], S[:, :h]], [-S[:, :h], C[:, :h]]])
    m_first_real = np.concatenate([C, -S], 0)
    m_mid = np.block([[C, S], [-S, C]])
    m_mid_inv = np.block([[C, -S], [S, C]])
    m_last = np.block([[C[:h], -S[:h]], [S[:h], C[:h]]])
    ang = 2.0 * np.pi * np.outer(np.arange(FFT_R), np.arange(FFT_R)) / FFT_N
    tw = np.stack([np.cos(ang), np.sin(ang)], -1)
    bf = lambda a: jnp.asarray(a, BF16)
    return dict(first=bf(m_first), first_real=bf(m_first_real), mid=bf(m_mid), mid_inv=bf(m_mid_inv),
                last=bf(m_last), tw=jnp.asarray(tw, F32))


def _fft_outer_kernel(m_ref, x_ref, o_ref):
    o_ref[...] = jnp.dot(m_ref[...], x_ref[...].astype(BF16), preferred_element_type=F32).astype(o_ref.dtype)


def fft_first(m, x2d):
    R, W = x2d.shape
    return pl.pallas_call(
        _fft_outer_kernel, grid=(W // FFT_COLS,),
        in_specs=[pl.BlockSpec(m.shape, lambda j: (0, 0)), pl.BlockSpec((R, FFT_COLS), lambda j: (0, j))],
        out_specs=pl.BlockSpec((m.shape[0], FFT_COLS), lambda j: (0, j)),
        out_shape=jax.ShapeDtypeStruct((m.shape[0], W), BF16),
        compiler_params=pltpu.CompilerParams(dimension_semantics=("arbitrary",)),
        name="hyena_fft_first")(m, x2d)


def _fft_mid_kernel(*refs, conv):
    if conv:
        a_ref, tw_ref, mf_ref, mi_ref, h_ref, o_ref = refs
    else:
        a_ref, tw_ref, mf_ref, o_ref = refs
    R = FFT_R
    tc = tw_ref[0, :, 0:1]
    ts = tw_ref[0, :, 1:2]
    ar = a_ref[0, 0].astype(F32)
    ai = a_ref[1, 0].astype(F32)
    x = jnp.concatenate([ar * tc + ai * ts, ai * tc - ar * ts], 0).astype(BF16)
    X = jnp.dot(mf_ref[...], x, preferred_element_type=F32)
    if not conv:
        o_ref[0, 0] = X[:R]
        o_ref[1, 0] = X[R:]
        return
    xr, xi = X[:R], X[R:]
    hr, hi = h_ref[0, 0], h_ref[1, 0]
    y = jnp.concatenate([xr * hr - xi * hi, xr * hi + xi * hr], 0).astype(BF16)
    Bm = jnp.dot(mi_ref[...], y, preferred_element_type=F32)
    br, bi = Bm[:R], Bm[R:]
    o_ref[0, 0] = (br * tc - bi * ts).astype(o_ref.dtype)
    o_ref[1, 0] = (bi * tc + br * ts).astype(o_ref.dtype)


def fft_mid(a, consts, h=None):
    C = a.shape[-1]
    slab = pl.BlockSpec((2, 1, FFT_R, C), lambda k: (0, k, 0, 0))
    twspec = pl.BlockSpec((1, FFT_R, 2), lambda k: (k, 0, 0))
    mspec = pl.BlockSpec((2 * FFT_R, 2 * FFT_R), lambda k: (0, 0))
    conv = h is not None
    ins = [a, consts['tw'], consts['mid']] + ([consts['mid_inv'], h] if conv else [])
    specs = [slab, twspec, mspec] + ([mspec, slab] if conv else [])
    return pl.pallas_call(
        functools.partial(_fft_mid_kernel, conv=conv), grid=(FFT_R,),
        in_specs=specs, out_specs=slab,
        out_shape=jax.ShapeDtypeStruct((2, FFT_R, FFT_R, C), BF16 if conv else F32),
        compiler_params=pltpu.CompilerParams(dimension_semantics=("arbitrary",)),
        name="hyena_fft_mid")(*ins)


def _fft_last_kernel(m_ref, b_ref, z_ref, bias_ref, gate_ref, o_ref):
    y = jnp.dot(m_ref[...], b_ref[...], preferred_element_type=F32)
    o_ref[...] = gate_ref[...] * (y + bias_ref[...] * z_ref[...])


def fft_last(m, b2d, z2d, bias_row, gate2d):
    R, W = z2d.shape
    col = lambda r: pl.BlockSpec((r, FFT_COLS), lambda j: (0, j))
    return pl.pallas_call(
        _fft_last_kernel, grid=(W // FFT_COLS,),
        in_specs=[pl.BlockSpec(m.shape, lambda j: (0, 0)), col(b2d.shape[0]), col(R),
                  pl.BlockSpec((1, FFT_COLS), lambda j: (0, 0)), col(R)],
        out_specs=col(R),
        out_shape=jax.ShapeDtypeStruct((R, W), F32),
        compiler_params=pltpu.CompilerParams(dimension_semantics=("arbitrary",)),
        name="hyena_fft_last")(m, b2d, z2d, bias_row, gate2d)


def filter_spectrum(k, consts):
    C = k.shape[1]
    a = fft_first(consts['first_real'], k.reshape(FFT_R, FFT_R * C))
    return fft_mid(a.reshape(2, FFT_R, FFT_R, C), consts)


def long_conv(z, gate, spec, bias, consts):
    B, L, C = z.shape
    assert B == 2 and 2 * L == FFT_N and FFT_COLS % C == 0
    z2d = z.reshape(FFT_R, FFT_R * C)
    a = fft_first(consts['first'], z2d)
    b = fft_mid(a.reshape(2, FFT_R, FFT_R, C), consts, spec)
    bias_row = jnp.tile(bias.reshape(1, C), (1, FFT_COLS // C))
    y = fft_last(consts['last'], b.reshape(2 * FFT_R, FFT_R * C), z2d, bias_row, gate.reshape(FFT_R, FFT_R * C))
    return y.reshape(B, L, C)


def dense_dft_constants(L):
    C, S = _dft_parts(2 * L)
    m_fwd = np.block([[C[:, :L], S[:, :L]], [-S[:, :L], C[:, :L]]])
    m_fwd_real = np.concatenate([C, -S], 0)
    m_inv = np.block([[C[:L], -S[:L]], [S[:L], C[:L]]])
    bf = lambda a: jnp.asarray(a, BF16)
    return bf(m_fwd), bf(m_fwd_real), bf(m_inv)


def _short_conv_kernel(mf_ref, mfr_ref, mi_ref, z_ref, k_ref, bias_ref, gate_ref, o_ref):
    n = k_ref.shape[0]
    z = z_ref[...]
    X = jnp.dot(mf_ref[...], z.astype(BF16), preferred_element_type=F32)
    Hs = jnp.dot(mfr_ref[...], k_ref[...].astype(BF16), preferred_element_type=F32)
    xr, xi, hr, hi = X[:n], X[n:], Hs[:n], Hs[n:]
    y = jnp.concatenate([xr * hr - xi * hi, xr * hi + xi * hr], 0).astype(BF16)
    o_ref[...] = gate_ref[...] * (jnp.dot(mi_ref[...], y, preferred_element_type=F32) + bias_ref[...] * z)


def short_conv(z, gate, k, bias):
    B, L, C = z.shape
    assert B == 2
    mf, mfr, mi = dense_dft_constants(L)
    y = pl.pallas_call(
        _short_conv_kernel,
        out_shape=jax.ShapeDtypeStruct((2 * L, C), F32),
        name="hyena_short_conv")(mf, mfr, mi, z.reshape(2 * L, C), k, bias.reshape(1, C), gate.reshape(2 * L, C))
    return y.reshape(B, L, C)


HALO = 8


def _hy_pre_kernel(x_ref, prev_ref, next_ref, w_ref, b_ref, v_ref, x1_ref, x2_ref, *, nt):
    t = pl.program_id(1)
    x = x_ref[0].astype(F32)
    row = lax.broadcasted_iota(jnp.int32, (ROW_TILE, 1), 0)
    prev_row = jnp.where(t > 0, prev_ref[0, HALO - 1:HALO].astype(F32), 0.0)
    next_row = jnp.where(t < nt - 1, next_ref[0, 0:1].astype(F32), 0.0)
    below = jnp.where(row == 0, prev_row, pltpu.roll(x, 1, 0))
    above = jnp.where(row == ROW_TILE - 1, next_row, pltpu.roll(x, ROW_TILE - 1, 0))
    y = below * w_ref[0:1] + x * w_ref[1:2] + above * w_ref[2:3] + b_ref[...]
    v_ref[0] = y[:, :HY_W]
    x1_ref[0] = y[:, HY_W:2 * HY_W]
    x2_ref[0] = y[:, 2 * HY_W:]


def hyena_pre(P, t0, nt, conv_w, conv_b):
    B, T, _ = P.shape
    cb = COL['hy'] // N_HY
    per = ROW_TILE // HALO
    last_blk = T // HALO - 1
    out = pl.BlockSpec((1, ROW_TILE, HY_W), lambda b, t: (b, t, 0))
    return pl.pallas_call(
        functools.partial(_hy_pre_kernel, nt=nt), grid=(B, nt),
        in_specs=[pl.BlockSpec((1, ROW_TILE, N_HY), lambda b, t: (b, t + t0, cb)),
                  pl.BlockSpec((1, HALO, N_HY), lambda b, t: (b, jnp.maximum((t + t0) * per - 1, 0), cb)),
                  pl.BlockSpec((1, HALO, N_HY), lambda b, t: (b, jnp.minimum((t + t0 + 1) * per, last_blk), cb)),
                  pl.BlockSpec((3, N_HY), lambda b, t: (0, 0)),
                  pl.BlockSpec((1, N_HY), lambda b, t: (0, 0))],
        out_specs=[out, out, out],
        out_shape=[jax.ShapeDtypeStruct((B, nt * ROW_TILE, HY_W), F32)] * 3,
        compiler_params=pltpu.CompilerParams(dimension_semantics=("arbitrary", "arbitrary")),
        name="hyena_pre")(P, P, P, conv_w, conv_b.reshape(1, N_HY))


def hyena_branch(P, with_ctx, n_ctx, conv_w, conv_b, w1, b1, w2, b2, w3, freq, hbias):
    B, T, _ = P.shape
    consts = hyena_fft_constants()
    hy_w = (w1, b1, w2, b2, w3, freq)
    nc = n_ctx // ROW_TILE
    v, x1, x2 = hyena_pre(P, nc, T // ROW_TILE - nc, conv_w, conv_b)
    k = hyena_kernels(T - n_ctx, *hy_w)
    z = long_conv(v, x1, filter_spectrum(k[0], consts), hbias[0], consts)
    o = long_conv(z, x2, filter_spectrum(k[1], consts), hbias[1], consts)
    if with_ctx:
        vc, xc1, xc2 = hyena_pre(P, 0, nc, conv_w, conv_b)
        kc = hyena_kernels(n_ctx, *hy_w)
        oc = short_conv(short_conv(vc, xc1, kc[0], hbias[0]), xc2, kc[1], hbias[1])
    else:
        oc = jnp.zeros((B, n_ctx, HY_W), F32)
    return o, oc


ROUTER_PAD = 128


def _merge_kernel(of_ref, ob_ref, r_ref, om_ref, ohl_ref, ohc_ref, g_ref, x_ref, mod_ref,
                  wg_ref, wm_ref, wh_ref, wo_ref, gn_ref, l1g_ref, l1b_ref, rw_ref, rb_ref,
                  x1_ref, h2_ref, lg_ref, *, n_ctx_tiles):
    o = of_ref[0] + ob_ref[0]
    parts = [_rms(o[:, h * GLA_DV:(h + 1) * GLA_DV], gn_ref[...]) for h in range(GLA_HEADS)]
    og = (jnp.concatenate(parts, axis=1) * jax.nn.silu(r_ref[0].astype(F32))).astype(BF16)
    oh = jnp.where(pl.program_id(1) < n_ctx_tiles, ohc_ref[0], ohl_ref[0]).astype(BF16)
    D = D_MODEL
    g = g_ref[0]
    y = (jax.nn.sigmoid(g[:, :D].astype(F32)) * jnp.dot(og, wg_ref[...], preferred_element_type=F32)
         + jax.nn.sigmoid(g[:, D:2 * D].astype(F32)) * jnp.dot(om_ref[0], wm_ref[...], preferred_element_type=F32)
         + jax.nn.sigmoid(g[:, 2 * D:].astype(F32)) * jnp.dot(oh, wh_ref[...], preferred_element_type=F32))
    ym = jnp.dot(y.astype(BF16), wo_ref[...], preferred_element_type=F32)
    gate1, sh2, sc2 = mod_ref[0, 0, 0:1], mod_ref[0, 0, 1:2], mod_ref[0, 0, 2:3]
    x1 = _layer_norm(DEEPNORM_ALPHA * x_ref[0] + gate1 * ym) * l1g_ref[...] + l1b_ref[...]
    x1_ref[0] = x1
    h2f = _layer_norm(x1) * (1.0 + sc2) + sh2
    h2_ref[0] = pack_bf16_pairs(h2f)
    h2 = h2f.astype(BF16)
    lg_ref[0] = jnp.dot(h2, rw_ref[...], preferred_element_type=F32) + rb_ref[...]


def merge_block(o_f, o_b, P, o_mla, o_hy_lat, o_hy_ctx, xa, mod, w_br_gla, w_br_mla, w_br_hy, w_out, gla_norm,
                ln1_g, ln1_b, router_w, router_b, n_ctx):
    B, T, D = xa.shape
    NC = n_ctx // ROW_TILE
    bf = lambda a: a.astype(BF16)
    row = lambda a: a.reshape(1, -1)
    rw = jnp.pad(router_w, ((0, 0), (0, ROUTER_PAD - N_EXPERTS))).astype(BF16)
    rb = jnp.pad(router_b, (0, ROUTER_PAD - N_EXPERTS)).reshape(1, -1)
    full = lambda a: pl.BlockSpec(a.shape, lambda b, t: (0,) * a.ndim)
    tile = lambda w: pl.BlockSpec((1, ROW_TILE, w), lambda b, t: (b, t, 0))
    lat = pl.BlockSpec((1, ROW_TILE, BRANCH_W), lambda b, t: (b, jnp.maximum(t - NC, 0), 0))
    ctx = pl.BlockSpec((1, ROW_TILE, BRANCH_W), lambda b, t: (b, jnp.minimum(t, NC - 1), 0))
    ws = [bf(w_br_gla), bf(w_br_mla), bf(w_br_hy), bf(w_out), row(gla_norm), row(ln1_g), row(ln1_b), rw, rb]
    return pl.pallas_call(
        functools.partial(_merge_kernel, n_ctx_tiles=NC), grid=(B, T // ROW_TILE),
        in_specs=[tile(BRANCH_W), tile(BRANCH_W), pcol('r'), tile(BRANCH_W), lat, ctx, pcol('gate'), tile(D),
                  pl.BlockSpec((1, 1, 3, D), _tile_kind(NC))] + [full(w) for w in ws],
        out_specs=[tile(D), tile(HALF_D), tile(ROUTER_PAD)],
        out_shape=[jax.ShapeDtypeStruct((B, T, D), F32), jax.ShapeDtypeStruct((B, T, HALF_D), U32),
                   jax.ShapeDtypeStruct((B, T, ROUTER_PAD), F32)],
        compiler_params=pltpu.CompilerParams(dimension_semantics=("arbitrary", "arbitrary"),
                                             vmem_limit_bytes=48 * 1024 * 1024),
        name="merge_block")(o_f, o_b, P, o_mla, o_hy_lat, o_hy_ctx, P, xa, mod, *ws)


def kernel(x, c, ctx, c_ctx, ada_w, ada_b, w_in, gla_wa2_f, gla_ba_f, gla_wa2_b, gla_ba_b, gla_norm,
           mla_q_norm, mla_w_uq, mla_kv_norm, mla_w_ukv, hy_conv_w, hy_conv_b, hy_w1, hy_b1, hy_w2, hy_b2,
           hy_w3, hy_freq, hy_bias, w_br_gla, w_br_mla, w_br_hy, w_out, ln1_g, ln1_b, ln2_g, ln2_b,
           router_w, router_b, moe_w1, moe_b1, moe_w2, moe_b2):
    B, L, D = x.shape
    CL = ctx.shape[1]
    T = CL + L
    tables = rope_tables(L, CL)
    xa = jnp.concatenate([ctx, x], axis=1)
    cond = jnp.concatenate([jax.nn.silu(c), jax.nn.silu(c_ctx)[None], jnp.zeros((8 - B - 1, D), F32)], 0)
    for l in range(DEPTH):
        last = l == DEPTH - 1
        ada = pmm(cond, ada_w[l])[:B + 1] + ada_b[l]
        ada = jnp.stack([ada[:B], jnp.broadcast_to(ada[B:], (B, 6 * D))], 1).reshape(B, 2, 6, D)
        P = ln_mod_proj(xa, ada[:, :, 0:2], permute_w_in(w_in[l]), CL)
        o_f, o_b = gla_scan(P, gla_wa2_f[l], gla_ba_f[l], gla_wa2_b[l], gla_ba_b[l], CL)
        q, k, v = mla_prep(P, tables, mla_weights(mla_w_uq[l], mla_w_ukv[l]), mla_q_norm[l], mla_kv_norm[l])
        o_mla = mla_attention(q, k, v, CL)
        o_hy, o_hy_ctx = hyena_branch(P, not last, CL, hy_conv_w[l], hy_conv_b[l], hy_w1[l], hy_b1[l], hy_w2[l],
                                      hy_b2[l], hy_w3[l], hy_freq[l], hy_bias[l])
        x1, h2, logits = merge_block(o_f, o_b, P, o_mla, o_hy, o_hy_ctx, xa, ada[:, :, 2:5],
                                     w_br_gla[l], w_br_mla[l], w_br_hy[l], w_out[l], gla_norm[l],
                                     ln1_g[l], ln1_b[l], router_w[l], router_b[l], CL)
        t0 = CL // ROW_TILE if last else 0
        nt = T // ROW_TILE - t0
        xa = moe_ffn_residual(h2, logits[:, t0 * ROW_TILE:, :N_EXPERTS], x1, ada[:, :, 5:6],
                              moe_w1[l], moe_b1[l], moe_w2[l], moe_b2[l], ln2_g[l], ln2_b[l],
                              t0, nt, CL // ROW_TILE)
    return xa
```

```python
import functools
import math
import jax
import jax.numpy as jnp
from jax import lax
import numpy as np
from jax.experimental import pallas as pl
from jax.experimental.pallas import tpu as pltpu

D_MODEL = 1024
DEPTH = 2
GRID_W = 64
BRANCH_W = D_MODEL // 2
N_BRANCH = 3
GLA_HEADS = 4
GLA_DV = BRANCH_W // GLA_HEADS
GLA_DK = GLA_DV // 2
GLA_RANK = 16
GLA_TAU = 16.0
GLA_CHUNK = 16
MLA_HEADS = 8
MLA_DV = BRANCH_W // MLA_HEADS
MLA_NOPE = MLA_DV
MLA_ROPE = MLA_NOPE // 2
MLA_Q_RANK = 3 * D_MODEL // 8
MLA_KV_RANK = D_MODEL // 4
ROPE_BASE = 10000.0
Q_BLOCK = 128
HY_W = BRANCH_W
HY_ORDER = 2
HY_DIRS = 2
HY_BANDS = 16
HY_EMB = 2 * HY_BANDS + 1
HY_FFN = 64
HY_DECAY_TARGET = 1e-2
HY_FAST_PCT = 0.3
HY_SLOW_PCT = 1.5
N_EXPERTS = 32
TOP_K = 4
D_FF = D_MODEL
SWIGLU_LIMIT = 7.0
SWIGLU_ALPHA = 1.702
MOE_BLOCK = 128
LN_EPS = 1e-5
RMS_EPS = 1e-6
DEEPNORM_ALPHA = (2 * DEPTH) ** 0.25
IN_SPLITS = (
    GLA_HEADS * GLA_DK, GLA_HEADS * GLA_DV, GLA_RANK, GLA_RANK, MLA_KV_RANK, MLA_ROPE,
    GLA_HEADS * GLA_DK, GLA_HEADS * GLA_DV, MLA_Q_RANK, 3 * HY_W, N_BRANCH * D_MODEL,
)
N_KEY_GROUPS = 6
KEY_COLS = sum(IN_SPLITS[:N_KEY_GROUPS])
IN_OFFSETS = tuple(int(o) for o in np.cumsum(IN_SPLITS)[:-1])
IN_TOTAL = sum(IN_SPLITS)
F32 = jnp.float32
BF16 = jnp.bfloat16
U32 = jnp.uint32


def _mm_kernel(a_ref, w_ref, o_ref):
    o_ref[...] = jnp.dot(a_ref[...], w_ref[...], preferred_element_type=F32)


def _pick(n, cands):
    for c in cands:
        if n % c == 0:
            return c
    return n


def pmm(a, w):
    lead = a.shape[:-1]
    K = a.shape[-1]
    N = w.shape[-1]
    a2 = a.reshape(-1, K).astype(BF16)
    M = a2.shape[0]
    N0 = N
    if N % 128:
        N = -(-N // 256) * 256
        w = jnp.pad(w, ((0, 0), (0, N - N0)))
    tm = _pick(M, (512, 256, 128, 64, 32, 16, 8))
    tn = _pick(N, (512, 256, 128))
    out = pl.pallas_call(
        _mm_kernel,
        grid=(M // tm, N // tn),
        in_specs=[pl.BlockSpec((tm, K), lambda i, j: (i, 0)),
                  pl.BlockSpec((K, tn), lambda i, j: (0, j))],
        out_specs=pl.BlockSpec((tm, tn), lambda i, j: (i, j)),
        out_shape=jax.ShapeDtypeStruct((M, N), F32),
    )(a2, w.astype(BF16))
    return out[:, :N0].reshape(lead + (N0,))


ROW_TILE = 256
HEAD_PAD = 128
N_GATE = N_BRANCH * D_MODEL
N_HY = 3 * HY_W
COL = dict(gate=0, hy=N_GATE, qa=N_GATE + N_HY)
COL['misc'] = COL['qa'] + MLA_Q_RANK
COL['v'] = COL['misc'] + HEAD_PAD
COL['r'] = COL['v'] + GLA_HEADS * GLA_DV
COL['k'] = COL['r'] + GLA_HEADS * GLA_DV
COL['q'] = COL['k'] + GLA_HEADS * GLA_DK
COL['kva'] = COL['q'] + GLA_HEADS * GLA_DK
P_COLS = COL['kva'] + MLA_KV_RANK
WIDTH = dict(gate=N_GATE, hy=N_HY, qa=MLA_Q_RANK, misc=HEAD_PAD, v=GLA_HEADS * GLA_DV, r=GLA_HEADS * GLA_DV,
             k=GLA_HEADS * GLA_DK, q=GLA_HEADS * GLA_DK, kva=MLA_KV_RANK)
MISC_AF, MISC_AB, MISC_KR = 0, GLA_RANK, MLA_NOPE
P_CHUNK = 768
assert all(COL[g] % WIDTH[g] == 0 for g in COL) and P_COLS % P_CHUNK == 0


def permute_w_in(w_in):
    o = (0,) + IN_OFFSETS + (IN_TOTAL,)
    k, v, af, ab, kva, kr, q, r, qa, hy, gate = [w_in[:, o[i]:o[i + 1]] for i in range(len(IN_SPLITS))]
    z = lambda n: jnp.zeros((w_in.shape[0], n), w_in.dtype)
    misc = jnp.concatenate([af, ab, z(MISC_KR - 2 * GLA_RANK), kr, z(HEAD_PAD - MISC_KR - MLA_ROPE)], 1)
    return jnp.concatenate([gate, hy, qa, misc, v, r, k, q, kva], 1).astype(BF16)


def _layer_norm(x):
    xc = x - jnp.mean(x, axis=-1, keepdims=True)
    return xc * lax.rsqrt(jnp.mean(xc * xc, axis=-1, keepdims=True) + LN_EPS)


def _rms(x, gain):
    return x * lax.rsqrt(jnp.mean(x * x, axis=-1, keepdims=True) + RMS_EPS) * gain


def _proj_kernel(x_ref, mod_ref, w_ref, o_ref):
    h = _layer_norm(x_ref[0]) * (1.0 + mod_ref[0, 0, 1:2]) + mod_ref[0, 0, 0:1]
    hb = h.astype(BF16)
    for c in range(0, P_COLS, P_CHUNK):
        o_ref[0, :, c:c + P_CHUNK] = jnp.dot(hb, w_ref[:, c:c + P_CHUNK], preferred_element_type=F32).astype(BF16)


def _tile_kind(n_ctx_tiles):
    return lambda b, t: (b, jnp.where(t < n_ctx_tiles, 1, 0), 0, 0)


def ln_mod_proj(xa, mod, w_p, n_ctx):
    B, T, D = xa.shape
    return pl.pallas_call(
        _proj_kernel, grid=(B, T // ROW_TILE),
        in_specs=[pl.BlockSpec((1, ROW_TILE, D), lambda b, t: (b, t, 0)),
                  pl.BlockSpec((1, 1, 2, D), _tile_kind(n_ctx // ROW_TILE)),
                  pl.BlockSpec((D, P_COLS), lambda b, t: (0, 0), pipeline_mode=pl.Buffered(1))],
        out_specs=pl.BlockSpec((1, ROW_TILE, P_COLS), lambda b, t: (b, t, 0)),
        out_shape=jax.ShapeDtypeStruct((B, T, P_COLS), BF16),
        compiler_params=pltpu.CompilerParams(dimension_semantics=("arbitrary", "arbitrary"),
                                             vmem_limit_bytes=48 * 1024 * 1024),
        name="ln_mod_proj")(xa, mod, w_p)


def pcol(group, row_map=lambda b, t: t):
    w = WIDTH[group]
    return pl.BlockSpec((1, ROW_TILE, w), lambda b, t, _i=COL[group] // w: (b, row_map(b, t), _i))


ONES_LANE_EVEN = MLA_DV
ONES_LANE_ODD = 0
LOG2E = 1.4426950408889634


def rope_tables(L, n_ctx):
    rows = L // GRID_W
    row = jnp.repeat(jnp.arange(rows, dtype=F32), GRID_W)
    col = jnp.tile(jnp.arange(GRID_W, dtype=F32), rows)
    n_freq = MLA_ROPE // 4
    inv = ROPE_BASE ** (-jnp.arange(n_freq, dtype=F32) / n_freq)
    ang = jnp.concatenate([row[:, None] * inv, col[:, None] * inv], -1)
    ang = jnp.concatenate([jnp.zeros((n_ctx, ang.shape[1]), F32), ang], 0)
    cos, sin = jnp.cos(ang), jnp.sin(ang)
    T = ang.shape[0]
    half = MLA_ROPE // 2
    z = lambda n: jnp.zeros((T, n), F32)
    ct = jnp.concatenate([jnp.ones((T, MLA_NOPE), F32), cos, cos, z(HEAD_PAD - MLA_NOPE - MLA_ROPE)], 1)
    sa = jnp.concatenate([z(MLA_NOPE), -sin, z(HEAD_PAD - MLA_NOPE - half)], 1)
    sb = jnp.concatenate([z(MLA_NOPE + half), sin, z(HEAD_PAD - MLA_NOPE - MLA_ROPE)], 1)
    return ct, sa, sb


def mla_weights(w_uq, w_ukv):
    H = MLA_HEADS
    wq = w_uq.reshape(MLA_Q_RANK, H, MLA_NOPE + MLA_ROPE)
    wq = jnp.pad(wq, ((0, 0), (0, 0), (0, HEAD_PAD - MLA_NOPE - MLA_ROPE))).reshape(MLA_Q_RANK, H * HEAD_PAD)
    wkv = w_ukv.reshape(MLA_KV_RANK, H, MLA_NOPE + MLA_DV)
    wk = jnp.pad(wkv[:, :, :MLA_NOPE], ((0, 0), (0, 0), (0, HEAD_PAD - MLA_NOPE))).reshape(MLA_KV_RANK, H * HEAD_PAD)
    wv = wkv[:, :, MLA_NOPE:].reshape(MLA_KV_RANK, H // 2, 2, MLA_DV)
    z = jnp.zeros_like(wv[:, :, 0])
    wv = jnp.stack([jnp.concatenate([wv[:, :, 0], z], -1), jnp.concatenate([z, wv[:, :, 1]], -1)], 2)
    ones = np.zeros((H // 2, 2, HEAD_PAD), np.float32)
    ones[:, 0, ONES_LANE_EVEN] = 1.0
    ones[:, 1, ONES_LANE_ODD] = 1.0
    return (wq.astype(BF16), wk.astype(BF16), wv.reshape(MLA_KV_RANK, H * HEAD_PAD).astype(BF16),
            jnp.asarray(ones.reshape(1, H * HEAD_PAD)))


def _mla_prep_kernel(qa_ref, kva_ref, misc_ref, ct_ref, sa_ref, sb_ref, wq_ref, wk_ref, wv_ref, ones_ref,
                     qn_ref, kn_ref, q_ref, k_ref, v_ref):
    ct, sa, sb = ct_ref[...], sa_ref[...], sb_ref[...]
    half = MLA_ROPE // 2
    rope = lambda x: x * ct + pltpu.roll(x, HEAD_PAD - half, 1) * sa + pltpu.roll(x, half, 1) * sb
    qn = _rms(qa_ref[0].astype(F32), qn_ref[...]).astype(BF16)
    q = jnp.dot(qn, wq_ref[...], preferred_element_type=F32)
    kn = _rms(kva_ref[0].astype(F32), kn_ref[...]).astype(BF16)
    k = jnp.dot(kn, wk_ref[...], preferred_element_type=F32)
    v_ref[0] = (jnp.dot(kn, wv_ref[...], preferred_element_type=F32) + ones_ref[...]).astype(BF16)
    lane = lax.broadcasted_iota(jnp.int32, (1, HEAD_PAD), 1)
    is_kr = (lane >= MISC_KR) & (lane < MISC_KR + MLA_ROPE)
    kr = rope(jnp.where(is_kr, misc_ref[0].astype(F32), 0.0))
    qscale = (MLA_NOPE + MLA_ROPE) ** -0.5 * LOG2E
    for h in range(MLA_HEADS):
        hs = slice(h * HEAD_PAD, (h + 1) * HEAD_PAD)
        q_ref[0, :, hs] = (rope(q[:, hs]) * qscale).astype(BF16)
        k_ref[0, :, hs] = (k[:, hs] + kr).astype(BF16)


def mla_prep(P, tables, weights, q_norm, kv_norm):
    B, T, _ = P.shape
    ct, sa, sb = tables
    wq, wk, wv, ones = weights
    full = lambda a: pl.BlockSpec(a.shape, lambda b, t: (0,) * a.ndim)
    tab = pl.BlockSpec((ROW_TILE, HEAD_PAD), lambda b, t: (t, 0))
    qn, kn = q_norm.reshape(1, -1), kv_norm.reshape(1, -1)
    W = MLA_HEADS * HEAD_PAD
    out = pl.BlockSpec((1, ROW_TILE, W), lambda b, t: (b, t, 0))
    return pl.pallas_call(
        _mla_prep_kernel, grid=(B, T // ROW_TILE),
        in_specs=[pcol('qa'), pcol('kva'), pcol('misc'), tab, tab, tab, full(wq), full(wk), full(wv), full(ones),
                  full(qn), full(kn)],
        out_specs=[out, out, out],
        out_shape=[jax.ShapeDtypeStruct((B, T, W), BF16)] * 3,
        compiler_params=pltpu.CompilerParams(dimension_semantics=("arbitrary", "arbitrary")),
        name="mla_prep")(P, P, P, ct, sa, sb, wq, wk, wv, ones, qn, kn)


ATT_TQ = 256
ATT_TK = 768


def _attn_kernel(q_ref, k_ref, v_ref, o_ref, s_buf, p_buf, m_buf, a_buf, mrun, acc, *, n_ctx, tk):
    tq = q_ref.shape[1]
    T = k_ref.shape[1]
    in_ctx = pl.program_id(2) * tq < n_ctx
    lane = lax.broadcasted_iota(jnp.int32, (tq, HEAD_PAD), 1)
    hs = lambda h: slice(h * HEAD_PAD, (h + 1) * HEAD_PAD)

    def stage_a(c, slot, size):
        start = c * size
        for h in range(2):
            k = k_ref[0, pl.ds(start, size), hs(h)]
            s = lax.dot_general(q_ref[0, :, hs(h)], k, (((1,), (1,)), ((), ())), preferred_element_type=F32)
            s_buf[slot, h, :, :size] = s
            m_prev = mrun[h]
            m_new = jnp.maximum(m_prev, jnp.max(s, axis=-1, keepdims=True))
            mrun[h] = m_new
            m_buf[slot, h] = m_new
            a_buf[slot, h] = jnp.exp2(m_prev - m_new)

    def stage_b(slot, size):
        for h in range(2):
            p_buf[slot, h, :, :size] = jnp.exp2(s_buf[slot, h, :, :size] - m_buf[slot, h]).astype(BF16)

    def stage_c(c, slot, size):
        start = c * size
        for h in range(2):
            v = v_ref[0, pl.ds(start, size), hs(h)]
            acc[h] = a_buf[slot, h] * acc[h] + jnp.dot(p_buf[slot, h, :, :size], v, preferred_element_type=F32)

    def run(n, size):
        for h in range(2):
            mrun[h] = jnp.full((tq, 1), -jnp.inf, F32)
            acc[h] = jnp.zeros((tq, HEAD_PAD), F32)
        if n == 1:
            stage_a(0, 0, size)
            stage_b(0, size)
            stage_c(0, 0, size)
        else:
            stage_a(0, 0, size)
            stage_b(0, size)
            stage_a(1, 1, size)

            for c in range(n - 2):
                stage_c(c, c % 2, size)
                stage_b(1 - c % 2, size)
                stage_a(c + 2, c % 2, size)
            stage_c(n - 2, (n - 2) % 2, size)
            stage_b((n - 1) % 2, size)
            stage_c(n - 1, (n - 1) % 2, size)
        a0, a1 = acc[0], acc[1]
        l0 = a0[:, ONES_LANE_EVEN:ONES_LANE_EVEN + 1]
        l1 = a1[:, ONES_LANE_ODD:ONES_LANE_ODD + 1]
        o_ref[0] = jnp.where(lane < MLA_DV, a0 / l0, a1 / l1).astype(o_ref.dtype)

    @pl.when(in_ctx)
    def _():
        run(1, n_ctx)

    @pl.when(jnp.logical_not(in_ctx))
    def _():
        run(T // tk, tk)


def mla_attention(q, k, v, n_ctx):
    B, T, _ = q.shape
    assert T % ATT_TQ == 0 and T % ATT_TK == 0 and n_ctx % ATT_TQ == 0 and n_ctx <= ATT_TK
    kern = functools.partial(_attn_kernel, n_ctx=n_ctx, tk=ATT_TK)
    return pl.pallas_call(
        kern,
        grid=(B, MLA_HEADS // 2, T // ATT_TQ),
        in_specs=[pl.BlockSpec((1, ATT_TQ, 2 * HEAD_PAD), lambda b, h, i: (b, i, h)),
                  pl.BlockSpec((1, T, 2 * HEAD_PAD), lambda b, h, i: (b, 0, h)),
                  pl.BlockSpec((1, T, 2 * HEAD_PAD), lambda b, h, i: (b, 0, h))],
        out_specs=pl.BlockSpec((1, ATT_TQ, 2 * MLA_DV), lambda b, h, i: (b, i, h)),
        out_shape=jax.ShapeDtypeStruct((B, T, MLA_HEADS * MLA_DV), BF16),
        scratch_shapes=[pltpu.VMEM((2, 2, ATT_TQ, ATT_TK), F32), pltpu.VMEM((2, 2, ATT_TQ, ATT_TK), BF16),
                        pltpu.VMEM((2, 2, ATT_TQ, 1), F32), pltpu.VMEM((2, 2, ATT_TQ, 1), F32),
                        pltpu.VMEM((2, ATT_TQ, 1), F32), pltpu.VMEM((2, ATT_TQ, HEAD_PAD), F32)],
        compiler_params=pltpu.CompilerParams(
            dimension_semantics=("arbitrary", "arbitrary", "arbitrary"),
            vmem_limit_bytes=56 * 1024 * 1024),
        name="mla_attention",
    )(q, k, v)


MOE_TM = 256
HALF_D = D_MODEL // 2
HI_MASK = 0xFFFF0000
N_ROW_DMA = ROW_TILE * TOP_K


def pack_bf16_pairs(x):
    u = pltpu.bitcast(x.astype(BF16).astype(F32), U32)
    return (u[:, :HALF_D] >> 16) | (u[:, HALF_D:] & jnp.uint32(HI_MASK))


def unpack_bf16_pairs(w):
    lo = pltpu.bitcast(w << 16, F32)
    hi = pltpu.bitcast(w & jnp.uint32(HI_MASK), F32)
    return jnp.concatenate([lo, hi], axis=1).astype(BF16)


def moe_route(logits):
    N = logits.shape[0]
    top_val, top_idx = lax.top_k(logits, TOP_K)
    gates = jax.nn.softmax(top_val, axis=-1)
    NK = N * TOP_K
    flat_e = top_idx.reshape(NK)
    onehot = (flat_e[:, None] == jnp.arange(N_EXPERTS, dtype=flat_e.dtype)[None, :]).astype(F32)
    nb = NK // MOE_TM
    within = jnp.einsum('ij,bje->bie', jnp.tril(jnp.ones((MOE_TM, MOE_TM), F32)),
                        onehot.reshape(nb, MOE_TM, N_EXPERTS))
    btot = within[:, -1, :].astype(jnp.int32)
    boff = jnp.cumsum(btot, axis=0) - btot
    csum = (within.astype(jnp.int32) + boff[:, None, :]).reshape(NK, N_EXPERTS)
    rank = jnp.take_along_axis(csum, flat_e[:, None], axis=1)[:, 0] - 1
    counts = csum[-1]
    padded = (counts + MOE_TM - 1) // MOE_TM * MOE_TM
    pad_end = jnp.cumsum(padded)
    dest = ((pad_end - padded)[flat_e] + rank).astype(jnp.int32)
    n_blocks = NK // MOE_TM + N_EXPERTS
    blk_start = jnp.arange(n_blocks, dtype=jnp.int32) * MOE_TM
    block_e = jnp.minimum(jnp.sum(blk_start[:, None] >= pad_end[None, :], axis=1), N_EXPERTS - 1).astype(jnp.int32)
    first = jnp.concatenate([jnp.ones((1,), jnp.int32), (block_e[1:] != block_e[:-1]).astype(jnp.int32)])
    n_used = (pad_end[-1:] // MOE_TM).astype(jnp.int32)
    return dest, gates, block_e, first, n_used


def _dispatch_kernel(dest_ref, h_ref, xb_in_ref, xb_ref, sem):
    del xb_in_ref

    def start(r, c):
        for k in range(TOP_K):
            pltpu.make_async_copy(h_ref.at[0, pl.ds(r, 1)], xb_ref.at[pl.ds(dest_ref[r * TOP_K + k], 1)],
                                  sem).start()
        return c

    lax.fori_loop(0, ROW_TILE, start, 0, unroll=4)
    for k in range(TOP_K):
        pltpu.make_async_copy(h_ref.at[0], xb_ref.at[pl.ds(0, ROW_TILE)], sem).wait()


def moe_dispatch(h2p, dest, n_rows, t0, nt):
    B = h2p.shape[0]
    xb0 = jnp.zeros((n_rows, HALF_D), U32)
    return pl.pallas_call(
        _dispatch_kernel, grid=(B, nt),
        in_specs=[pl.BlockSpec((N_ROW_DMA,), lambda b, t: (b * nt + t,), memory_space=pltpu.SMEM),
                  pl.BlockSpec((1, ROW_TILE, HALF_D), lambda b, t: (b, t + t0, 0)),
                  pl.BlockSpec(memory_space=pl.ANY)],
        out_specs=pl.BlockSpec(memory_space=pl.ANY),
        out_shape=jax.ShapeDtypeStruct((n_rows, HALF_D), U32),
        scratch_shapes=[pltpu.SemaphoreType.DMA],
        input_output_aliases={2: 0},
        compiler_params=pltpu.CompilerParams(dimension_semantics=("arbitrary", "arbitrary")),
        name="moe_dispatch")(dest, h2p, xb0)


def _moe_kernel(be_ref, first_ref, nb_ref, x_ref, w1_ref, b1_ref, w2_ref, b2_ref, o_ref, w1_bf, w2_bf):
    i = pl.program_id(0)

    @pl.when(jnp.logical_and(i < nb_ref[0], first_ref[i] == 1))
    def _():
        w1_bf[...] = w1_ref[0].astype(BF16)
        w2_bf[...] = w2_ref[0].astype(BF16)

    @pl.when(i < nb_ref[0])
    def _():
        x = unpack_bf16_pairs(x_ref[...])
        hcat = jnp.dot(x, w1_bf[...], preferred_element_type=F32) + b1_ref[0]
        glu = jnp.minimum(hcat[:, :D_FF], SWIGLU_LIMIT)
        lin = jnp.clip(hcat[:, D_FF:], -SWIGLU_LIMIT, SWIGLU_LIMIT)
        act = glu * jax.nn.sigmoid(SWIGLU_ALPHA * glu) * (lin + 1)
        o_ref[...] = jnp.dot(act.astype(BF16), w2_bf[...], preferred_element_type=F32) + b2_ref[0]

    @pl.when(i >= nb_ref[0])
    def _():
        o_ref[...] = jnp.zeros_like(o_ref)


def moe_experts(xb, block_e, first, n_used, w1, b1, w2, b2):
    R = xb.shape[0]
    D = D_MODEL
    n_blocks = R // MOE_TM
    grid_spec = pltpu.PrefetchScalarGridSpec(
        num_scalar_prefetch=3,
        grid=(n_blocks,),
        in_specs=[pl.BlockSpec((MOE_TM, HALF_D), lambda i, be, fi, nb: (i, 0)),
                  pl.BlockSpec((1, D, 2 * D_FF), lambda i, be, fi, nb: (be[i], 0, 0)),
                  pl.BlockSpec((1, 1, 2 * D_FF), lambda i, be, fi, nb: (be[i], 0, 0)),
                  pl.BlockSpec((1, D_FF, D), lambda i, be, fi, nb: (be[i], 0, 0)),
                  pl.BlockSpec((1, 1, D), lambda i, be, fi, nb: (be[i], 0, 0))],
        out_specs=pl.BlockSpec((MOE_TM, D), lambda i, be, fi, nb: (i, 0)),
        scratch_shapes=[pltpu.VMEM((D, 2 * D_FF), BF16), pltpu.VMEM((D_FF, D), BF16)],
    )
    return pl.pallas_call(
        _moe_kernel,
        grid_spec=grid_spec,
        out_shape=jax.ShapeDtypeStruct((R, D), F32),
        compiler_params=pltpu.CompilerParams(
            dimension_semantics=("arbitrary",), vmem_limit_bytes=56 * 1024 * 1024),
        name="moe_experts",
    )(block_e, first, n_used, xb, w1, b1[:, None, :], w2, b2[:, None, :])


def _combine_kernel(dest_ref, x_ref, gate_ref, mod_ref, lg_ref, lb_ref, yb_ref, o_ref, buf, sem):
    def start(r, c):
        for k in range(TOP_K):
            pltpu.make_async_copy(yb_ref.at[pl.ds(dest_ref[r * TOP_K + k], 1)], buf.at[k, pl.ds(r, 1)], sem).start()
        return c

    lax.fori_loop(0, ROW_TILE, start, 0, unroll=4)
    for k in range(TOP_K):
        pltpu.make_async_copy(yb_ref.at[pl.ds(0, ROW_TILE)], buf.at[k], sem).wait()
    g = gate_ref[...]
    f = g[:, 0:1] * buf[0]
    for k in range(1, TOP_K):
        f = f + g[:, k:k + 1] * buf[k]
    o_ref[0] = _layer_norm(DEEPNORM_ALPHA * x_ref[0] + mod_ref[0, 0, 0:1] * f) * lg_ref[...] + lb_ref[...]


def moe_combine(yb, dest, gates, x1, mod, ln_g, ln_b, t0, nt, n_ctx_tiles):
    B, T, D = x1.shape
    row = lambda a: a.reshape(1, -1)
    full = lambda a: pl.BlockSpec(a.shape, lambda b, t: (0,) * a.ndim)
    return pl.pallas_call(
        _combine_kernel, grid=(B, nt),
        in_specs=[pl.BlockSpec((N_ROW_DMA,), lambda b, t: (b * nt + t,), memory_space=pltpu.SMEM),
                  pl.BlockSpec((1, ROW_TILE, D), lambda b, t: (b, t + t0, 0)),
                  pl.BlockSpec((ROW_TILE, TOP_K), lambda b, t: (b * nt + t, 0)),
                  pl.BlockSpec((1, 1, 1, D), lambda b, t: (b, jnp.where(t + t0 < n_ctx_tiles, 1, 0), 0, 0)),
                  full(row(ln_g)), full(row(ln_b)),
                  pl.BlockSpec(memory_space=pl.ANY)],
        out_specs=pl.BlockSpec((1, ROW_TILE, D), lambda b, t: (b, t, 0)),
        out_shape=jax.ShapeDtypeStruct((B, nt * ROW_TILE, D), F32),
        scratch_shapes=[pltpu.VMEM((TOP_K, ROW_TILE, D), F32), pltpu.SemaphoreType.DMA],
        compiler_params=pltpu.CompilerParams(dimension_semantics=("arbitrary", "arbitrary")),
        name="moe_combine")(dest, x1, gates, mod, row(ln_g), row(ln_b), yb)


def moe_ffn_residual(h2p, logits, x1, gate2, w1, b1, w2, b2, ln_g, ln_b, t0, nt, n_ctx_tiles):
    B = h2p.shape[0]
    N = B * nt * ROW_TILE
    dest, gates, block_e, first, n_used = moe_route(logits.reshape(N, N_EXPERTS))
    n_rows = (N * TOP_K // MOE_TM + N_EXPERTS) * MOE_TM
    xb = moe_dispatch(h2p, dest, n_rows, t0, nt)
    yb = moe_experts(xb, block_e, first, n_used, w1, b1, w2, b2)
    return moe_combine(yb, dest, gates, x1, gate2, ln_g, ln_b, t0, nt, n_ctx_tiles)


GLA_TILE = 256
GLA_BASE = 16
GLA_LEVELS = (16, 32, 64, 128)
N_SEL = 2 * (len(GLA_LEVELS) + 1)


def gla_constants():
    n = GLA_TILE
    i = np.arange(n)[:, None]
    t = np.arange(n)[None, :]
    sizes = GLA_LEVELS + (n,)
    P = [((t <= i) & (t // s == i // s)) for s in sizes]
    S = [((t > i) & (t // s == i // s)) for s in sizes]
    sel_f = np.concatenate(P + S, 0)
    sel_b = np.concatenate([p.T for p in P] + [s.T for s in S], 0)
    lv = [((i // (2 * s) == t // (2 * s)) & ((i // s) % 2 == 1) & ((t // s) % 2 == 0)) for s in GLA_LEVELS]
    dg = (i // GLA_BASE == t // GLA_BASE) & (t <= i)
    msk_f = np.stack(lv + [dg], 0)
    msk_b = np.stack([m.T for m in lv] + [dg.T], 0)
    return (jnp.asarray(np.stack([sel_f, sel_b], 0), BF16), jnp.asarray(np.stack([msk_f, msk_b], 0), F32))


def _gla_kernel(qf_ref, kf_ref, vf_ref, af_ref, qb_ref, kb_ref, vb_ref, ab_ref,
                wf_ref, bf_ref, wb_ref, bb_ref, sel_ref, msk_ref, of_ref, ob_ref, state):
    TL = GLA_TILE
    NL = len(GLA_LEVELS)

    @pl.when(pl.program_id(1) == 0)
    def _():
        state[...] = jnp.zeros_like(state)

    lane = lax.broadcasted_iota(jnp.int32, (1, 2 * GLA_DK), 1)
    nt = lambda a, b: lax.dot_general(a, b, (((1,), (1,)), ((), ())), preferred_element_type=F32)
    dirs = ((qf_ref, kf_ref, vf_ref, af_ref, wf_ref, bf_ref, of_ref, TL - 1),
            (qb_ref, kb_ref, vb_ref, ab_ref, wb_ref, bb_ref, ob_ref, 0))
    for d, (q_ref, k_ref, v_ref, a_ref, w_ref, b_ref, o_ref, last) in enumerate(dirs):
        x = jnp.dot(a_ref[0], w_ref[...], preferred_element_type=F32) + b_ref[...]
        g = jax.nn.log_sigmoid(x) * (1.0 / GLA_TAU)
        g1 = g.astype(BF16)
        g2 = (g - g1.astype(F32)).astype(BF16)
        sel = sel_ref[d]
        ps = jnp.dot(sel, g1, preferred_element_type=F32) + jnp.dot(sel, g2, preferred_element_type=F32)
        seg = lambda j: ps[j * TL:(j + 1) * TL]
        q = q_ref[0].astype(F32) * (GLA_DK ** -0.5)
        k = k_ref[0].astype(F32)
        e_pre = [jnp.exp(seg(j)) for j in range(NL + 1)]
        e_suf = [jnp.exp(seg(NL + 1 + j)) for j in range(NL + 1)]
        qh = [(q * e).astype(BF16) for e in e_pre]
        kh = [(k * e).astype(BF16) for e in e_suf]
        kd = (k * jnp.exp(-seg(0))).astype(BF16)
        dec = e_pre[NL][last:last + 1]
        for h in range(GLA_HEADS):
            pr = slice((h // 2) * 2 * GLA_DK, (h // 2 + 1) * 2 * GLA_DK)
            mine = (lane // GLA_DK) == (h % 2)
            qm = [jnp.where(mine, t[:, pr], 0) for t in qh[:NL]]
            sc = msk_ref[d, NL] * nt(qm[0], kd[:, pr])
            for j in range(NL):
                sc = sc + msk_ref[d, j] * nt(qm[j], kh[j][:, pr])
            v = v_ref[0, :, h * GLA_DV:(h + 1) * GLA_DV].astype(BF16)
            st = state[d, h]
            o = jnp.dot(sc.astype(BF16), v, preferred_element_type=F32) + nt(qh[NL][:, pr], st.astype(BF16))
            o_ref[0, :, h * GLA_DV:(h + 1) * GLA_DV] = o
            upd = lax.dot_general(v, kh[NL][:, pr], (((0,), (0,)), ((), ())), preferred_element_type=F32)
            state[d, h] = jnp.where(mine, dec[:, pr] * st + upd, 0)


def gla_scan(P, wa2_f, ba_f, wa2_b, ba_b, n_ctx):
    B, T, _ = P.shape
    assert GLA_TILE == ROW_TILE and T % ROW_TILE == 0 and n_ctx % ROW_TILE == 0
    NT, NC = T // ROW_TILE, n_ctx // ROW_TILE
    sel, msk = gla_constants()
    fwd = lambda b, n: n
    bwd = lambda b, n: jnp.where(n < NC, NC - 1 - n, NT - 1 - (n - NC))
    full = lambda a: pl.BlockSpec(a.shape, lambda b, n: (0,) * a.ndim)
    HK, HV = GLA_HEADS * GLA_DK, GLA_HEADS * GLA_DV
    pad = lambda w, off: jnp.zeros((HEAD_PAD, HK), F32).at[off:off + GLA_RANK].set(w).astype(BF16)
    wf, wb = pad(wa2_f, MISC_AF), pad(wa2_b, MISC_AB)
    ba_f2, ba_b2 = ba_f.reshape(1, HK), ba_b.reshape(1, HK)
    out = lambda rm: pl.BlockSpec((1, ROW_TILE, HV), lambda b, n: (b, rm(b, n), 0))
    return pl.pallas_call(
        _gla_kernel,
        grid=(B, NT),
        in_specs=[pcol('q', fwd), pcol('k', fwd), pcol('v', fwd), pcol('misc', fwd),
                  pcol('q', bwd), pcol('k', bwd), pcol('v', bwd), pcol('misc', bwd),
                  full(wf), full(ba_f2), full(wb), full(ba_b2), full(sel), full(msk)],
        out_specs=[out(fwd), out(bwd)],
        out_shape=[jax.ShapeDtypeStruct((B, T, HV), F32)] * 2,
        scratch_shapes=[pltpu.VMEM((2, GLA_HEADS, GLA_DV, 2 * GLA_DK), F32)],
        compiler_params=pltpu.CompilerParams(dimension_semantics=("arbitrary", "arbitrary"),
                                             vmem_limit_bytes=48 * 1024 * 1024),
        name="gla_scan",
    )(P, P, P, P, P, P, P, P, wf, ba_f2, wb, ba_b2, sel, msk)


def hyena_kernels(L, w1, b1, w2, b2, w3, freq):
    t = jnp.linspace(0.0, 1.0, L, dtype=F32)[:, None]
    w = 2 * math.pi * jnp.arange(L, dtype=F32)[:, None] / L
    f = jnp.linspace(1e-4, HY_BANDS - 1, HY_BANDS, dtype=F32)
    z = jnp.concatenate([t, jnp.cos(f * w), -jnp.sin(f * w)], -1)
    ext = lambda a: jnp.concatenate([a, a[:1], jnp.flip(a[1:], 0)], 0)
    z, t = ext(z), ext(t)
    fr = freq.astype(F32)
    h = jnp.sin(fr * (z @ w1.astype(F32) + b1.astype(F32)))
    h = jnp.sin(fr * (h @ w2.astype(F32) + b2.astype(F32)))
    h = (h @ w3.astype(F32)).reshape(2 * L, HY_ORDER, HY_DIRS, HY_W)
    deltas = jnp.abs(jnp.linspace(math.log(HY_DECAY_TARGET) / HY_SLOW_PCT,
                                  math.log(HY_DECAY_TARGET) / HY_FAST_PCT, HY_W, dtype=F32))
    h = h * jnp.exp(-t * deltas)[:, None, None, :]
    norm = jnp.sum(jnp.abs(h[:L]), axis=(0, 2))
    row = jnp.arange(2 * L)[:, None, None]
    k = jnp.where(row < L, h[:, :, 0], jnp.where(row > L, h[:, :, 1], 0.0))
    return jnp.moveaxis(k / (norm * (2 * L)), 1, 0)


FFT_R = 128
FFT_N = FFT_R * FFT_R
FFT_COLS = 2048


def _dft_parts(n):
    ang = 2.0 * np.pi * np.outer(np.arange(n), np.arange(n)) / n
    return np.cos(ang), np.sin(ang)


def hyena_fft_constants():
    C, S = _dft_parts(FFT_R)
    h = FFT_R // 2
    m_first = np.block([[C[:, :h], S[:, :h]], [-S[:, :h], C[:, :h]]])
    m_first_real = np.concatenate([C, -S], 0)
    m_mid = np.block([[C, S], [-S, C]])
    m_mid_inv = np.block([[C, -S], [S, C]])
    m_last = np.block([[C[:h], -S[:h]], [S[:h], C[:h]]])
    ang = 2.0 * np.pi * np.outer(np.arange(FFT_R), np.arange(FFT_R)) / FFT_N
    tw = np.stack([np.cos(ang), np.sin(ang)], -1)
    bf = lambda a: jnp.asarray(a, BF16)
    return dict(first=bf(m_first), first_real=bf(m_first_real), mid=bf(m_mid), mid_inv=bf(m_mid_inv),
                last=bf(m_last), tw=jnp.asarray(tw, F32))


def _fft_outer_kernel(m_ref, x_ref, o_ref):
    o_ref[...] = jnp.dot(m_ref[...], x_ref[...].astype(BF16), preferred_element_type=F32).astype(o_ref.dtype)


def fft_first(m, x2d):
    R, W = x2d.shape
    return pl.pallas_call(
        _fft_outer_kernel, grid=(W // FFT_COLS,),
        in_specs=[pl.BlockSpec(m.shape, lambda j: (0, 0)), pl.BlockSpec((R, FFT_COLS), lambda j: (0, j))],
        out_specs=pl.BlockSpec((m.shape[0], FFT_COLS), lambda j: (0, j)),
        out_shape=jax.ShapeDtypeStruct((m.shape[0], W), BF16),
        compiler_params=pltpu.CompilerParams(dimension_semantics=("arbitrary",)),
        name="hyena_fft_first")(m, x2d)


def _fft_mid_kernel(*refs, conv):
    if conv:
        a_ref, tw_ref, mf_ref, mi_ref, h_ref, o_ref = refs
    else:
        a_ref, tw_ref, mf_ref, o_ref = refs
    R = FFT_R
    tc = tw_ref[0, :, 0:1]
    ts = tw_ref[0, :, 1:2]
    ar = a_ref[0, 0].astype(F32)
    ai = a_ref[1, 0].astype(F32)
    x = jnp.concatenate([ar * tc + ai * ts, ai * tc - ar * ts], 0).astype(BF16)
    X = jnp.dot(mf_ref[...], x, preferred_element_type=F32)
    if not conv:
        o_ref[0, 0] = X[:R]
        o_ref[1, 0] = X[R:]
        return
    xr, xi = X[:R], X[R:]
    hr, hi = h_ref[0, 0], h_ref[1, 0]
    y = jnp.concatenate([xr * hr - xi * hi, xr * hi + xi * hr], 0).astype(BF16)
    Bm = jnp.dot(mi_ref[...], y, preferred_element_type=F32)
    br, bi = Bm[:R], Bm[R:]
    o_ref[0, 0] = (br * tc - bi * ts).astype(o_ref.dtype)
    o_ref[1, 0] = (bi * tc + br * ts).astype(o_ref.dtype)


def fft_mid(a, consts, h=None):
    C = a.shape[-1]
    slab = pl.BlockSpec((2, 1, FFT_R, C), lambda k: (0, k, 0, 0))
    twspec = pl.BlockSpec((1, FFT_R, 2), lambda k: (k, 0, 0))
    mspec = pl.BlockSpec((2 * FFT_R, 2 * FFT_R), lambda k: (0, 0))
    conv = h is not None
    ins = [a, consts['tw'], consts['mid']] + ([consts['mid_inv'], h] if conv else [])
    specs = [slab, twspec, mspec] + ([mspec, slab] if conv else [])
    return pl.pallas_call(
        functools.partial(_fft_mid_kernel, conv=conv), grid=(FFT_R,),
        in_specs=specs, out_specs=slab,
        out_shape=jax.ShapeDtypeStruct((2, FFT_R, FFT_R, C), BF16 if conv else F32),
        compiler_params=pltpu.CompilerParams(dimension_semantics=("arbitrary",)),
        name="hyena_fft_mid")(*ins)


def _fft_last_kernel(m_ref, b_ref, z_ref, bias_ref, gate_ref, o_ref):
    y = jnp.dot(m_ref[...], b_ref[...], preferred_element_type=F32)
    o_ref[...] = gate_ref[...] * (y + bias_ref[...] * z_ref[...])


def fft_last(m, b2d, z2d, bias_row, gate2d):
    R, W = z2d.shape
    col = lambda r: pl.BlockSpec((r, FFT_COLS), lambda j: (0, j))
    return pl.pallas_call(
        _fft_last_kernel, grid=(W // FFT_COLS,),
        in_specs=[pl.BlockSpec(m.shape, lambda j: (0, 0)), col(b2d.shape[0]), col(R),
                  pl.BlockSpec((1, FFT_COLS), lambda j: (0, 0)), col(R)],
        out_specs=col(R),
        out_shape=jax.ShapeDtypeStruct((R, W), F32),
        compiler_params=pltpu.CompilerParams(dimension_semantics=("arbitrary",)),
        name="hyena_fft_last")(m, b2d, z2d, bias_row, gate2d)


def filter_spectrum(k, consts):
    C = k.shape[1]
    a = fft_first(consts['first_real'], k.reshape(FFT_R, FFT_R * C))
    return fft_mid(a.reshape(2, FFT_R, FFT_R, C), consts)


def long_conv(z, gate, spec, bias, consts):
    B, L, C = z.shape
    assert B == 2 and 2 * L == FFT_N and FFT_COLS % C == 0
    z2d = z.reshape(FFT_R, FFT_R * C)
    a = fft_first(consts['first'], z2d)
    b = fft_mid(a.reshape(2, FFT_R, FFT_R, C), consts, spec)
    bias_row = jnp.tile(bias.reshape(1, C), (1, FFT_COLS // C))
    y = fft_last(consts['last'], b.reshape(2 * FFT_R, FFT_R * C), z2d, bias_row, gate.reshape(FFT_R, FFT_R * C))
    return y.reshape(B, L, C)


def dense_dft_constants(L):
    C, S = _dft_parts(2 * L)
    m_fwd = np.block([[C[:, :L], S[:, :L]], [-S[:, :L], C[:, :L]]])
    m_fwd_real = np.concatenate([C, -S], 0)
    m_inv = np.block([[C[:L], -S[:L]], [S[:L], C[:L]]])
    bf = lambda a: jnp.asarray(a, BF16)
    return bf(m_fwd), bf(m_fwd_real), bf(m_inv)


def _short_conv_kernel(mf_ref, mfr_ref, mi_ref, z_ref, k_ref, bias_ref, gate_ref, o_ref):
    n = k_ref.shape[0]
    z = z_ref[...]
    X = jnp.dot(mf_ref[...], z.astype(BF16), preferred_element_type=F32)
    Hs = jnp.dot(mfr_ref[...], k_ref[...].astype(BF16), preferred_element_type=F32)
    xr, xi, hr, hi = X[:n], X[n:], Hs[:n], Hs[n:]
    y = jnp.concatenate([xr * hr - xi * hi, xr * hi + xi * hr], 0).astype(BF16)
    o_ref[...] = gate_ref[...] * (jnp.dot(mi_ref[...], y, preferred_element_type=F32) + bias_ref[...] * z)


def short_conv(z, gate, k, bias):
    B, L, C = z.shape
    assert B == 2
    mf, mfr, mi = dense_dft_constants(L)
    y = pl.pallas_call(
        _short_conv_kernel,
        out_shape=jax.ShapeDtypeStruct((2 * L, C), F32),
        name="hyena_short_conv")(mf, mfr, mi, z.reshape(2 * L, C), k, bias.reshape(1, C), gate.reshape(2 * L, C))
    return y.reshape(B, L, C)


HALO = 8


def _hy_pre_kernel(x_ref, prev_ref, next_ref, w_ref, b_ref, v_ref, x1_ref, x2_ref, *, nt):
    t = pl.program_id(1)
    x = x_ref[0].astype(F32)
    row = lax.broadcasted_iota(jnp.int32, (ROW_TILE, 1), 0)
    prev_row = jnp.where(t > 0, prev_ref[0, HALO - 1:HALO].astype(F32), 0.0)
    next_row = jnp.where(t < nt - 1, next_ref[0, 0:1].astype(F32), 0.0)
    below = jnp.where(row == 0, prev_row, pltpu.roll(x, 1, 0))
    above = jnp.where(row == ROW_TILE - 1, next_row, pltpu.roll(x, ROW_TILE - 1, 0))
    y = below * w_ref[0:1] + x * w_ref[1:2] + above * w_ref[2:3] + b_ref[...]
    v_ref[0] = y[:, :HY_W]
    x1_ref[0] = y[:, HY_W:2 * HY_W]
    x2_ref[0] = y[:, 2 * HY_W:]


def hyena_pre(P, t0, nt, conv_w, conv_b):
    B, T, _ = P.shape
    cb = COL['hy'] // N_HY
    per = ROW_TILE // HALO
    last_blk = T // HALO - 1
    out = pl.BlockSpec((1, ROW_TILE, HY_W), lambda b, t: (b, t, 0))
    return pl.pallas_call(
        functools.partial(_hy_pre_kernel, nt=nt), grid=(B, nt),
        in_specs=[pl.BlockSpec((1, ROW_TILE, N_HY), lambda b, t: (b, t + t0, cb)),
                  pl.BlockSpec((1, HALO, N_HY), lambda b, t: (b, jnp.maximum((t + t0) * per - 1, 0), cb)),
                  pl.BlockSpec((1, HALO, N_HY), lambda b, t: (b, jnp.minimum((t + t0 + 1) * per, last_blk), cb)),
                  pl.BlockSpec((3, N_HY), lambda b, t: (0, 0)),
                  pl.BlockSpec((1, N_HY), lambda b, t: (0, 0))],
        out_specs=[out, out, out],
        out_shape=[jax.ShapeDtypeStruct((B, nt * ROW_TILE, HY_W), F32)] * 3,
        compiler_params=pltpu.CompilerParams(dimension_semantics=("arbitrary", "arbitrary")),
        name="hyena_pre")(P, P, P, conv_w, conv_b.reshape(1, N_HY))


def hyena_branch(P, with_ctx, n_ctx, conv_w, conv_b, w1, b1, w2, b2, w3, freq, hbias):
    B, T, _ = P.shape
    consts = hyena_fft_constants()
    hy_w = (w1, b1, w2, b2, w3, freq)
    nc = n_ctx // ROW_TILE
    v, x1, x2 = hyena_pre(P, nc, T // ROW_TILE - nc, conv_w, conv_b)
    k = hyena_kernels(T - n_ctx, *hy_w)
    z = long_conv(v, x1, filter_spectrum(k[0], consts), hbias[0], consts)
    o = long_conv(z, x2, filter_spectrum(k[1], consts), hbias[1], consts)
    if with_ctx:
        vc, xc1, xc2 = hyena_pre(P, 0, nc, conv_w, conv_b)
        kc = hyena_kernels(n_ctx, *hy_w)
        oc = short_conv(short_conv(vc, xc1, kc[0], hbias[0]), xc2, kc[1], hbias[1])
    else:
        oc = jnp.zeros((B, n_ctx, HY_W), F32)
    return o, oc


ROUTER_PAD = 128


def _merge_kernel(of_ref, ob_ref, r_ref, om_ref, ohl_ref, ohc_ref, g_ref, x_ref, mod_ref,
                  wg_ref, wm_ref, wh_ref, wo_ref, gn_ref, l1g_ref, l1b_ref, rw_ref, rb_ref,
                  x1_ref, h2_ref, lg_ref, *, n_ctx_tiles):
    o = of_ref[0] + ob_ref[0]
    parts = [_rms(o[:, h * GLA_DV:(h + 1) * GLA_DV], gn_ref[...]) for h in range(GLA_HEADS)]
    og = (jnp.concatenate(parts, axis=1) * jax.nn.silu(r_ref[0].astype(F32))).astype(BF16)
    oh = jnp.where(pl.program_id(1) < n_ctx_tiles, ohc_ref[0], ohl_ref[0]).astype(BF16)
    D = D_MODEL
    g = g_ref[0]
    y = (jax.nn.sigmoid(g[:, :D].astype(F32)) * jnp.dot(og, wg_ref[...], preferred_element_type=F32)
         + jax.nn.sigmoid(g[:, D:2 * D].astype(F32)) * jnp.dot(om_ref[0], wm_ref[...], preferred_element_type=F32)
         + jax.nn.sigmoid(g[:, 2 * D:].astype(F32)) * jnp.dot(oh, wh_ref[...], preferred_element_type=F32))
    ym = jnp.dot(y.astype(BF16), wo_ref[...], preferred_element_type=F32)
    gate1, sh2, sc2 = mod_ref[0, 0, 0:1], mod_ref[0, 0, 1:2], mod_ref[0, 0, 2:3]
    x1 = _layer_norm(DEEPNORM_ALPHA * x_ref[0] + gate1 * ym) * l1g_ref[...] + l1b_ref[...]
    x1_ref[0] = x1
    h2f = _layer_norm(x1) * (1.0 + sc2) + sh2
    h2_ref[0] = pack_bf16_pairs(h2f)
    h2 = h2f.astype(BF16)
    lg_ref[0] = jnp.dot(h2, rw_ref[...], preferred_element_type=F32) + rb_ref[...]


def merge_block(o_f, o_b, P, o_mla, o_hy_lat, o_hy_ctx, xa, mod, w_br_gla, w_br_mla, w_br_hy, w_out, gla_norm,
                ln1_g, ln1_b, router_w, router_b, n_ctx):
    B, T, D = xa.shape
    NC = n_ctx // ROW_TILE
    bf = lambda a: a.astype(BF16)
    row = lambda a: a.reshape(1, -1)
    rw = jnp.pad(router_w, ((0, 0), (0, ROUTER_PAD - N_EXPERTS))).astype(BF16)
    rb = jnp.pad(router_b, (0, ROUTER_PAD - N_EXPERTS)).reshape(1, -1)
    full = lambda a: pl.BlockSpec(a.shape, lambda b, t: (0,) * a.ndim)
    tile = lambda w: pl.BlockSpec((1, ROW_TILE, w), lambda b, t: (b, t, 0))
    lat = pl.BlockSpec((1, ROW_TILE, BRANCH_W), lambda b, t: (b, jnp.maximum(t - NC, 0), 0))
    ctx = pl.BlockSpec((1, ROW_TILE, BRANCH_W), lambda b, t: (b, jnp.minimum(t, NC - 1), 0))
    ws = [bf(w_br_gla), bf(w_br_mla), bf(w_br_hy), bf(w_out), row(gla_norm), row(ln1_g), row(ln1_b), rw, rb]
    return pl.pallas_call(
        functools.partial(_merge_kernel, n_ctx_tiles=NC), grid=(B, T // ROW_TILE),
        in_specs=[tile(BRANCH_W), tile(BRANCH_W), pcol('r'), tile(BRANCH_W), lat, ctx, pcol('gate'), tile(D),
                  pl.BlockSpec((1, 1, 3, D), _tile_kind(NC))] + [full(w) for w in ws],
        out_specs=[tile(D), tile(HALF_D), tile(ROUTER_PAD)],
        out_shape=[jax.ShapeDtypeStruct((B, T, D), F32), jax.ShapeDtypeStruct((B, T, HALF_D), U32),
                   jax.ShapeDtypeStruct((B, T, ROUTER_PAD), F32)],
        compiler_params=pltpu.CompilerParams(dimension_semantics=("arbitrary", "arbitrary"),
                                             vmem_limit_bytes=48 * 1024 * 1024),
        name="merge_block")(o_f, o_b, P, o_mla, o_hy_lat, o_hy_ctx, P, xa, mod, *ws)


def kernel(x, c, ctx, c_ctx, ada_w, ada_b, w_in, gla_wa2_f, gla_ba_f, gla_wa2_b, gla_ba_b, gla_norm,
           mla_q_norm, mla_w_uq, mla_kv_norm, mla_w_ukv, hy_conv_w, hy_conv_b, hy_w1, hy_b1, hy_w2, hy_b2,
           hy_w3, hy_freq, hy_bias, w_br_gla, w_br_mla, w_br_hy, w_out, ln1_g, ln1_b, ln2_g, ln2_b,
           router_w, router_b, moe_w1, moe_b1, moe_w2, moe_b2):
    B, L, D = x.shape
    CL = ctx.shape[1]
    T = CL + L
    tables = rope_tables(L, CL)
    xa = jnp.concatenate([ctx, x], axis=1)
    cond = jnp.concatenate([jax.nn.silu(c), jax.nn.silu(c_ctx)[None], jnp.zeros((8 - B - 1, D), F32)], 0)
    for l in range(DEPTH):
        last = l == DEPTH - 1
        ada = pmm(cond, ada_w[l])[:B + 1] + ada_b[l]
        ada = jnp.stack([ada[:B], jnp.broadcast_to(ada[B:], (B, 6 * D))], 1).reshape(B, 2, 6, D)
        P = ln_mod_proj(xa, ada[:, :, 0:2], permute_w_in(w_in[l]), CL)
        o_f, o_b = gla_scan(P, gla_wa2_f[l], gla_ba_f[l], gla_wa2_b[l], gla_ba_b[l], CL)
        q, k, v = mla_prep(P, tables, mla_weights(mla_w_uq[l], mla_w_ukv[l]), mla_q_norm[l], mla_kv_norm[l])
        o_mla = mla_attention(q, k, v, CL)
        o_hy, o_hy_ctx = hyena_branch(P, not last, CL, hy_conv_w[l], hy_conv_b[l], hy_w1[l], hy_b1[l], hy_w2[l],
                                      hy_b2[l], hy_w3[l], hy_freq[l], hy_bias[l])
        x1, h2, logits = merge_block(o_f, o_b, P, o_mla, o_hy, o_hy_ctx, xa, ada[:, :, 2:5],
                                     w_br_gla[l], w_br_mla[l], w_br_hy[l], w_out[l], gla_norm[l],
                                     ln1_g[l], ln1_b[l], router_w[l], router_b[l], CL)
        t0 = CL // ROW_TILE if last else 0
        nt = T // ROW_TILE - t0
        xa = moe_ffn_residual(h2, logits[:, t0 * ROW_TILE:, :N_EXPERTS], x1, ada[:, :, 5:6],
                              moe_w1[l], moe_b1[l], moe_w2[l], moe_b2[l], ln2_g[l], ln2_b[l],
                              t0, nt, CL // ROW_TILE)
    return xa
```

```python
import functools
import math
import jax
import jax.numpy as jnp
from jax import lax
import numpy as np
from jax.experimental import pallas as pl
from jax.experimental.pallas import tpu as pltpu

D_MODEL = 1024
DEPTH = 2
GRID_W = 64
BRANCH_W = D_MODEL // 2
N_BRANCH = 3
GLA_HEADS = 4
GLA_DV = BRANCH_W // GLA_HEADS
GLA_DK = GLA_DV // 2
GLA_RANK = 16
GLA_TAU = 16.0
GLA_CHUNK = 16
MLA_HEADS = 8
MLA_DV = BRANCH_W // MLA_HEADS
MLA_NOPE = MLA_DV
MLA_ROPE = MLA_NOPE // 2
MLA_Q_RANK = 3 * D_MODEL // 8
MLA_KV_RANK = D_MODEL // 4
ROPE_BASE = 10000.0
Q_BLOCK = 128
HY_W = BRANCH_W
HY_ORDER = 2
HY_DIRS = 2
HY_BANDS = 16
HY_EMB = 2 * HY_BANDS + 1
HY_FFN = 64
HY_DECAY_TARGET = 1e-2
HY_FAST_PCT = 0.3
HY_SLOW_PCT = 1.5
N_EXPERTS = 32
TOP_K = 4
D_FF = D_MODEL
SWIGLU_LIMIT = 7.0
SWIGLU_ALPHA = 1.702
MOE_BLOCK = 128
LN_EPS = 1e-5
RMS_EPS = 1e-6
DEEPNORM_ALPHA = (2 * DEPTH) ** 0.25
IN_SPLITS = (
    GLA_HEADS * GLA_DK, GLA_HEADS * GLA_DV, GLA_RANK, GLA_RANK, MLA_KV_RANK, MLA_ROPE,
    GLA_HEADS * GLA_DK, GLA_HEADS * GLA_DV, MLA_Q_RANK, 3 * HY_W, N_BRANCH * D_MODEL,
)
N_KEY_GROUPS = 6
KEY_COLS = sum(IN_SPLITS[:N_KEY_GROUPS])
IN_OFFSETS = tuple(int(o) for o in np.cumsum(IN_SPLITS)[:-1])
IN_TOTAL = sum(IN_SPLITS)
F32 = jnp.float32
BF16 = jnp.bfloat16
U32 = jnp.uint32


def _mm_kernel(a_ref, w_ref, o_ref):
    o_ref[...] = jnp.dot(a_ref[...], w_ref[...], preferred_element_type=F32)


def _pick(n, cands):
    for c in cands:
        if n % c == 0:
            return c
    return n


def pmm(a, w):
    lead = a.shape[:-1]
    K = a.shape[-1]
    N = w.shape[-1]
    a2 = a.reshape(-1, K).astype(BF16)
    M = a2.shape[0]
    N0 = N
    if N % 128:
        N = -(-N // 256) * 256
        w = jnp.pad(w, ((0, 0), (0, N - N0)))
    tm = _pick(M, (512, 256, 128, 64, 32, 16, 8))
    tn = _pick(N, (512, 256, 128))
    out = pl.pallas_call(
        _mm_kernel,
        grid=(M // tm, N // tn),
        in_specs=[pl.BlockSpec((tm, K), lambda i, j: (i, 0)),
                  pl.BlockSpec((K, tn), lambda i, j: (0, j))],
        out_specs=pl.BlockSpec((tm, tn), lambda i, j: (i, j)),
        out_shape=jax.ShapeDtypeStruct((M, N), F32),
    )(a2, w.astype(BF16))
    return out[:, :N0].reshape(lead + (N0,))


ROW_TILE = 256
HEAD_PAD = 128
N_GATE = N_BRANCH * D_MODEL
N_HY = 3 * HY_W
COL = dict(gate=0, hy=N_GATE, qa=N_GATE + N_HY)
COL['misc'] = COL['qa'] + MLA_Q_RANK
COL['v'] = COL['misc'] + HEAD_PAD
COL['r'] = COL['v'] + GLA_HEADS * GLA_DV
COL['k'] = COL['r'] + GLA_HEADS * GLA_DV
COL['q'] = COL['k'] + GLA_HEADS * GLA_DK
COL['kva'] = COL['q'] + GLA_HEADS * GLA_DK
P_COLS = COL['kva'] + MLA_KV_RANK
WIDTH = dict(gate=N_GATE, hy=N_HY, qa=MLA_Q_RANK, misc=HEAD_PAD, v=GLA_HEADS * GLA_DV, r=GLA_HEADS * GLA_DV,
             k=GLA_HEADS * GLA_DK, q=GLA_HEADS * GLA_DK, kva=MLA_KV_RANK)
MISC_AF, MISC_AB, MISC_KR = 0, GLA_RANK, MLA_NOPE
P_CHUNK = 768
assert all(COL[g] % WIDTH[g] == 0 for g in COL) and P_COLS % P_CHUNK == 0


def permute_w_in(w_in):
    o = (0,) + IN_OFFSETS + (IN_TOTAL,)
    k, v, af, ab, kva, kr, q, r, qa, hy, gate = [w_in[:, o[i]:o[i + 1]] for i in range(len(IN_SPLITS))]
    z = lambda n: jnp.zeros((w_in.shape[0], n), w_in.dtype)
    misc = jnp.concatenate([af, ab, z(MISC_KR - 2 * GLA_RANK), kr, z(HEAD_PAD - MISC_KR - MLA_ROPE)], 1)
    return jnp.concatenate([gate, hy, qa, misc, v, r, k, q, kva], 1).astype(BF16)


def _layer_norm(x):
    xc = x - jnp.mean(x, axis=-1, keepdims=True)
    return xc * lax.rsqrt(jnp.mean(xc * xc, axis=-1, keepdims=True) + LN_EPS)


def _rms(x, gain):
    return x * lax.rsqrt(jnp.mean(x * x, axis=-1, keepdims=True) + RMS_EPS) * gain


def _proj_kernel(x_ref, mod_ref, w_ref, o_ref):
    h = _layer_norm(x_ref[0]) * (1.0 + mod_ref[0, 0, 1:2]) + mod_ref[0, 0, 0:1]
    hb = h.astype(BF16)
    for c in range(0, P_COLS, P_CHUNK):
        o_ref[0, :, c:c + P_CHUNK] = jnp.dot(hb, w_ref[:, c:c + P_CHUNK], preferred_element_type=F32).astype(BF16)


def _tile_kind(n_ctx_tiles):
    return lambda b, t: (b, jnp.where(t < n_ctx_tiles, 1, 0), 0, 0)


def ln_mod_proj(xa, mod, w_p, n_ctx):
    B, T, D = xa.shape
    return pl.pallas_call(
        _proj_kernel, grid=(B, T // ROW_TILE),
        in_specs=[pl.BlockSpec((1, ROW_TILE, D), lambda b, t: (b, t, 0)),
                  pl.BlockSpec((1, 1, 2, D), _tile_kind(n_ctx // ROW_TILE)),
                  pl.BlockSpec((D, P_COLS), lambda b, t: (0, 0), pipeline_mode=pl.Buffered(1))],
        out_specs=pl.BlockSpec((1, ROW_TILE, P_COLS), lambda b, t: (b, t, 0)),
        out_shape=jax.ShapeDtypeStruct((B, T, P_COLS), BF16),
        compiler_params=pltpu.CompilerParams(dimension_semantics=("arbitrary", "arbitrary"),
                                             vmem_limit_bytes=48 * 1024 * 1024),
        name="ln_mod_proj")(xa, mod, w_p)


def pcol(group, row_map=lambda b, t: t):
    w = WIDTH[group]
    return pl.BlockSpec((1, ROW_TILE, w), lambda b, t, _i=COL[group] // w: (b, row_map(b, t), _i))


ONES_LANE_EVEN = MLA_DV
ONES_LANE_ODD = 0
LOG2E = 1.4426950408889634


def rope_tables(L, n_ctx):
    rows = L // GRID_W
    row = jnp.repeat(jnp.arange(rows, dtype=F32), GRID_W)
    col = jnp.tile(jnp.arange(GRID_W, dtype=F32), rows)
    n_freq = MLA_ROPE // 4
    inv = ROPE_BASE ** (-jnp.arange(n_freq, dtype=F32) / n_freq)
    ang = jnp.concatenate([row[:, None] * inv, col[:, None] * inv], -1)
    ang = jnp.concatenate([jnp.zeros((n_ctx, ang.shape[1]), F32), ang], 0)
    cos, sin = jnp.cos(ang), jnp.sin(ang)
    T = ang.shape[0]
    half = MLA_ROPE // 2
    z = lambda n: jnp.zeros((T, n), F32)
    ct = jnp.concatenate([jnp.ones((T, MLA_NOPE), F32), cos, cos, z(HEAD_PAD - MLA_NOPE - MLA_ROPE)], 1)
    sa = jnp.concatenate([z(MLA_NOPE), -sin, z(HEAD_PAD - MLA_NOPE - half)], 1)
    sb = jnp.concatenate([z(MLA_NOPE + half), sin, z(HEAD_PAD - MLA_NOPE - MLA_ROPE)], 1)
    return ct, sa, sb


def mla_weights(w_uq, w_ukv):
    H = MLA_HEADS
    wq = w_uq.reshape(MLA_Q_RANK, H, MLA_NOPE + MLA_ROPE)
    wq = jnp.pad(wq, ((0, 0), (0, 0), (0, HEAD_PAD - MLA_NOPE - MLA_ROPE))).reshape(MLA_Q_RANK, H * HEAD_PAD)
    wkv = w_ukv.reshape(MLA_KV_RANK, H, MLA_NOPE + MLA_DV)
    wk = jnp.pad(wkv[:, :, :MLA_NOPE], ((0, 0), (0, 0), (0, HEAD_PAD - MLA_NOPE))).reshape(MLA_KV_RANK, H * HEAD_PAD)
    wv = wkv[:, :, MLA_NOPE:].reshape(MLA_KV_RANK, H // 2, 2, MLA_DV)
    z = jnp.zeros_like(wv[:, :, 0])
    wv = jnp.stack([jnp.concatenate([wv[:, :, 0], z], -1), jnp.concatenate([z, wv[:, :, 1]], -1)], 2)
    ones = np.zeros((H // 2, 2, HEAD_PAD), np.float32)
    ones[:, 0, ONES_LANE_EVEN] = 1.0
    ones[:, 1, ONES_LANE_ODD] = 1.0
    return (wq.astype(BF16), wk.astype(BF16), wv.reshape(MLA_KV_RANK, H * HEAD_PAD).astype(BF16),
            jnp.asarray(ones.reshape(1, H * HEAD_PAD)))


def _mla_prep_kernel(qa_ref, kva_ref, misc_ref, ct_ref, sa_ref, sb_ref, wq_ref, wk_ref, wv_ref, ones_ref,
                     qn_ref, kn_ref, q_ref, k_ref, v_ref):
    ct, sa, sb = ct_ref[...], sa_ref[...], sb_ref[...]
    half = MLA_ROPE // 2
    rope = lambda x: x * ct + pltpu.roll(x, HEAD_PAD - half, 1) * sa + pltpu.roll(x, half, 1) * sb
    qn = _rms(qa_ref[0].astype(F32), qn_ref[...]).astype(BF16)
    q = jnp.dot(qn, wq_ref[...], preferred_element_type=F32)
    kn = _rms(kva_ref[0].astype(F32), kn_ref[...]).astype(BF16)
    k = jnp.dot(kn, wk_ref[...], preferred_element_type=F32)
    v_ref[0] = (jnp.dot(kn, wv_ref[...], preferred_element_type=F32) + ones_ref[...]).astype(BF16)
    lane = lax.broadcasted_iota(jnp.int32, (1, HEAD_PAD), 1)
    is_kr = (lane >= MISC_KR) & (lane < MISC_KR + MLA_ROPE)
    kr = rope(jnp.where(is_kr, misc_ref[0].astype(F32), 0.0))
    qscale = (MLA_NOPE + MLA_ROPE) ** -0.5 * LOG2E
    for h in range(MLA_HEADS):
        hs = slice(h * HEAD_PAD, (h + 1) * HEAD_PAD)
        q_ref[0, :, hs] = (rope(q[:, hs]) * qscale).astype(BF16)
        k_ref[0, :, hs] = (k[:, hs] + kr).astype(BF16)


def mla_prep(P, tables, weights, q_norm, kv_norm):
    B, T, _ = P.shape
    ct, sa, sb = tables
    wq, wk, wv, ones = weights
    full = lambda a: pl.BlockSpec(a.shape, lambda b, t: (0,) * a.ndim)
    tab = pl.BlockSpec((ROW_TILE, HEAD_PAD), lambda b, t: (t, 0))
    qn, kn = q_norm.reshape(1, -1), kv_norm.reshape(1, -1)
    W = MLA_HEADS * HEAD_PAD
    out = pl.BlockSpec((1, ROW_TILE, W), lambda b, t: (b, t, 0))
    return pl.pallas_call(
        _mla_prep_kernel, grid=(B, T // ROW_TILE),
        in_specs=[pcol('qa'), pcol('kva'), pcol('misc'), tab, tab, tab, full(wq), full(wk), full(wv), full(ones),
                  full(qn), full(kn)],
        out_specs=[out, out, out],
        out_shape=[jax.ShapeDtypeStruct((B, T, W), BF16)] * 3,
        compiler_params=pltpu.CompilerParams(dimension_semantics=("arbitrary", "arbitrary")),
        name="mla_prep")(P, P, P, ct, sa, sb, wq, wk, wv, ones, qn, kn)


ATT_TQ = 256
ATT_TK = 768


def _attn_kernel(q_ref, k_ref, v_ref, o_ref, s_buf, p_buf, m_buf, a_buf, mrun, acc, *, n_ctx, tk):
    tq = q_ref.shape[1]
    T = k_ref.shape[1]
    in_ctx = pl.program_id(2) * tq < n_ctx
    lane = lax.broadcasted_iota(jnp.int32, (tq, HEAD_PAD), 1)
    hs = lambda h: slice(h * HEAD_PAD, (h + 1) * HEAD_PAD)

    def stage_a(c, slot, size):
        start = c * size
        for h in range(2):
            k = k_ref[0, pl.ds(start, size), hs(h)]
            s = lax.dot_general(q_ref[0, :, hs(h)], k, (((1,), (1,)), ((), ())), preferred_element_type=F32)
            s_buf[slot, h, :, :size] = s
            m_prev = mrun[h]
            m_new = jnp.maximum(m_prev, jnp.max(s, axis=-1, keepdims=True))
            mrun[h] = m_new
            m_buf[slot, h] = m_new
            a_buf[slot, h] = jnp.exp2(m_prev - m_new)

    def stage_b(slot, size):
        for h in range(2):
            p_buf[slot, h, :, :size] = jnp.exp2(s_buf[slot, h, :, :size] - m_buf[slot, h]).astype(BF16)

    def stage_c(c, slot, size):
        start = c * size
        for h in range(2):
            v = v_ref[0, pl.ds(start, size), hs(h)]
            acc[h] = a_buf[slot, h] * acc[h] + jnp.dot(p_buf[slot, h, :, :size], v, preferred_element_type=F32)

    def run(n, size):
        for h in range(2):
            mrun[h] = jnp.full((tq, 1), -jnp.inf, F32)
            acc[h] = jnp.zeros((tq, HEAD_PAD), F32)
        if n == 1:
            stage_a(0, 0, size)
            stage_b(0, size)
            stage_c(0, 0, size)
        else:
            stage_a(0, 0, size)
            stage_b(0, size)
            stage_a(1, 1, size)

            for c in range(n - 2):
                stage_c(c, c % 2, size)
                stage_b(1 - c % 2, size)
                stage_a(c + 2, c % 2, size)
            stage_c(n - 2, (n - 2) % 2, size)
            stage_b((n - 1) % 2, size)
            stage_c(n - 1, (n - 1) % 2, size)
        a0, a1 = acc[0], acc[1]
        l0 = a0[:, ONES_LANE_EVEN:ONES_LANE_EVEN + 1]
        l1 = a1[:, ONES_LANE_ODD:ONES_LANE_ODD + 1]
        o_ref[0] = jnp.where(lane < MLA_DV, a0 / l0, a1 / l1).astype(o_ref.dtype)

    @pl.when(in_ctx)
    def _():
        run(1, n_ctx)

    @pl.when(jnp.logical_not(in_ctx))
    def _():
        run(T // tk, tk)


def mla_attention(q, k, v, n_ctx):
    B, T, _ = q.shape
    assert T % ATT_TQ == 0 and T % ATT_TK == 0 and n_ctx % ATT_TQ == 0 and n_ctx <= ATT_TK
    kern = functools.partial(_attn_kernel, n_ctx=n_ctx, tk=ATT_TK)
    return pl.pallas_call(
        kern,
        grid=(B, MLA_HEADS // 2, T // ATT_TQ),
        in_specs=[pl.BlockSpec((1, ATT_TQ, 2 * HEAD_PAD), lambda b, h, i: (b, i, h)),
                  pl.BlockSpec((1, T, 2 * HEAD_PAD), lambda b, h, i: (b, 0, h)),
                  pl.BlockSpec((1, T, 2 * HEAD_PAD), lambda b, h, i: (b, 0, h))],
        out_specs=pl.BlockSpec((1, ATT_TQ, 2 * MLA_DV), lambda b, h, i: (b, i, h)),
        out_shape=jax.ShapeDtypeStruct((B, T, MLA_HEADS * MLA_DV), BF16),
        scratch_shapes=[pltpu.VMEM((2, 2, ATT_TQ, ATT_TK), F32), pltpu.VMEM((2, 2, ATT_TQ, ATT_TK), BF16),
                        pltpu.VMEM((2, 2, ATT_TQ, 1), F32), pltpu.VMEM((2, 2, ATT_TQ, 1), F32),
                        pltpu.VMEM((2, ATT_TQ, 1), F32), pltpu.VMEM((2, ATT_TQ, HEAD_PAD), F32)],
        compiler_params=pltpu.CompilerParams(
            dimension_semantics=("arbitrary", "arbitrary", "arbitrary"),
            vmem_limit_bytes=56 * 1024 * 1024),
        name="mla_attention",
    )(q, k, v)


MOE_TM = 256
HALF_D = D_MODEL // 2
HI_MASK = 0xFFFF0000
N_ROW_DMA = ROW_TILE * TOP_K


def pack_bf16_pairs(x):
    u = pltpu.bitcast(x.astype(BF16).astype(F32), U32)
    return (u[:, :HALF_D] >> 16) | (u[:, HALF_D:] & jnp.uint32(HI_MASK))


def unpack_bf16_pairs(w):
    lo = pltpu.bitcast(w << 16, F32)
    hi = pltpu.bitcast(w & jnp.uint32(HI_MASK), F32)
    return jnp.concatenate([lo, hi], axis=1).astype(BF16)


def moe_route(logits):
    N = logits.shape[0]
    top_val, top_idx = lax.top_k(logits, TOP_K)
    gates = jax.nn.softmax(top_val, axis=-1)
    NK = N * TOP_K
    flat_e = top_idx.reshape(NK)
    onehot = (flat_e[:, None] == jnp.arange(N_EXPERTS, dtype=flat_e.dtype)[None, :]).astype(F32)
    nb = NK // MOE_TM
    within = jnp.einsum('ij,bje->bie', jnp.tril(jnp.ones((MOE_TM, MOE_TM), F32)),
                        onehot.reshape(nb, MOE_TM, N_EXPERTS))
    btot = within[:, -1, :].astype(jnp.int32)
    boff = jnp.cumsum(btot, axis=0) - btot
    csum = (within.astype(jnp.int32) + boff[:, None, :]).reshape(NK, N_EXPERTS)
    rank = jnp.take_along_axis(csum, flat_e[:, None], axis=1)[:, 0] - 1
    counts = csum[-1]
    padded = (counts + MOE_TM - 1) // MOE_TM * MOE_TM
    pad_end = jnp.cumsum(padded)
    dest = ((pad_end - padded)[flat_e] + rank).astype(jnp.int32)
    n_blocks = NK // MOE_TM + N_EXPERTS
    blk_start = jnp.arange(n_blocks, dtype=jnp.int32) * MOE_TM
    block_e = jnp.minimum(jnp.sum(blk_start[:, None] >= pad_end[None, :], axis=1), N_EXPERTS - 1).astype(jnp.int32)
    first = jnp.concatenate([jnp.ones((1,), jnp.int32), (block_e[1:] != block_e[:-1]).astype(jnp.int32)])
    n_used = (pad_end[-1:] // MOE_TM).astype(jnp.int32)
    return dest, gates, block_e, first, n_used


def _dispatch_kernel(dest_ref, h_ref, xb_in_ref, xb_ref, sem):
    del xb_in_ref

    def start(r, c):
        for k in range(TOP_K):
            pltpu.make_async_copy(h_ref.at[0, pl.ds(r, 1)], xb_ref.at[pl.ds(dest_ref[r * TOP_K + k], 1)],
                                  sem).start()
        return c

    lax.fori_loop(0, ROW_TILE, start, 0, unroll=4)
    for k in range(TOP_K):
        pltpu.make_async_copy(h_ref.at[0], xb_ref.at[pl.ds(0, ROW_TILE)], sem).wait()


def moe_dispatch(h2p, dest, n_rows, t0, nt):
    B = h2p.shape[0]
    xb0 = jnp.zeros((n_rows, HALF_D), U32)
    return pl.pallas_call(
        _dispatch_kernel, grid=(B, nt),
        in_specs=[pl.BlockSpec((N_ROW_DMA,), lambda b, t: (b * nt + t,), memory_space=pltpu.SMEM),
                  pl.BlockSpec((1, ROW_TILE, HALF_D), lambda b, t: (b, t + t0, 0)),
                  pl.BlockSpec(memory_space=pl.ANY)],
        out_specs=pl.BlockSpec(memory_space=pl.ANY),
        out_shape=jax.ShapeDtypeStruct((n_rows, HALF_D), U32),
        scratch_shapes=[pltpu.SemaphoreType.DMA],
        input_output_aliases={2: 0},
        compiler_params=pltpu.CompilerParams(dimension_semantics=("arbitrary", "arbitrary")),
        name="moe_dispatch")(dest, h2p, xb0)


def _moe_kernel(be_ref, first_ref, nb_ref, x_ref, w1_ref, b1_ref, w2_ref, b2_ref, o_ref, w1_bf, w2_bf):
    i = pl.program_id(0)

    @pl.when(jnp.logical_and(i < nb_ref[0], first_ref[i] == 1))
    def _():
        w1_bf[...] = w1_ref[0].astype(BF16)
        w2_bf[...] = w2_ref[0].astype(BF16)

    @pl.when(i < nb_ref[0])
    def _():
        x = unpack_bf16_pairs(x_ref[...])
        hcat = jnp.dot(x, w1_bf[...], preferred_element_type=F32) + b1_ref[0]
        glu = jnp.minimum(hcat[:, :D_FF], SWIGLU_LIMIT)
        lin = jnp.clip(hcat[:, D_FF:], -SWIGLU_LIMIT, SWIGLU_LIMIT)
        act = glu * jax.nn.sigmoid(SWIGLU_ALPHA * glu) * (lin + 1)
        o_ref[...] = jnp.dot(act.astype(BF16), w2_bf[...], preferred_element_type=F32) + b2_ref[0]

    @pl.when(i >= nb_ref[0])
    def _():
        o_ref[...] = jnp.zeros_like(o_ref)


def moe_experts(xb, block_e, first, n_used, w1, b1, w2, b2):
    R = xb.shape[0]
    D = D_MODEL
    n_blocks = R // MOE_TM
    grid_spec = pltpu.PrefetchScalarGridSpec(
        num_scalar_prefetch=3,
        grid=(n_blocks,),
        in_specs=[pl.BlockSpec((MOE_TM, HALF_D), lambda i, be, fi, nb: (i, 0)),
                  pl.BlockSpec((1, D, 2 * D_FF), lambda i, be, fi, nb: (be[i], 0, 0)),
                  pl.BlockSpec((1, 1, 2 * D_FF), lambda i, be, fi, nb: (be[i], 0, 0)),
                  pl.BlockSpec((1, D_FF, D), lambda i, be, fi, nb: (be[i], 0, 0)),
                  pl.BlockSpec((1, 1, D), lambda i, be, fi, nb: (be[i], 0, 0))],
        out_specs=pl.BlockSpec((MOE_TM, D), lambda i, be, fi, nb: (i, 0)),
        scratch_shapes=[pltpu.VMEM((D, 2 * D_FF), BF16), pltpu.VMEM((D_FF, D), BF16)],
    )
    return pl.pallas_call(
        _moe_kernel,
        grid_spec=grid_spec,
        out_shape=jax.ShapeDtypeStruct((R, D), F32),
        compiler_params=pltpu.CompilerParams(
            dimension_semantics=("arbitrary",), vmem_limit_bytes=56 * 1024 * 1024),
        name="moe_experts",
    )(block_e, first, n_used, xb, w1, b1[:, None, :], w2, b2[:, None, :])


def _combine_kernel(dest_ref, x_ref, gate_ref, mod_ref, lg_ref, lb_ref, yb_ref, o_ref, buf, sem):
    def start(r, c):
        for k in range(TOP_K):
            pltpu.make_async_copy(yb_ref.at[pl.ds(dest_ref[r * TOP_K + k], 1)], buf.at[k, pl.ds(r, 1)], sem).start()
        return c

    lax.fori_loop(0, ROW_TILE, start, 0, unroll=4)
    for k in range(TOP_K):
        pltpu.make_async_copy(yb_ref.at[pl.ds(0, ROW_TILE)], buf.at[k], sem).wait()
    g = gate_ref[...]
    f = g[:, 0:1] * buf[0]
    for k in range(1, TOP_K):
        f = f + g[:, k:k + 1] * buf[k]
    o_ref[0] = _layer_norm(DEEPNORM_ALPHA * x_ref[0] + mod_ref[0, 0, 0:1] * f) * lg_ref[...] + lb_ref[...]


def moe_combine(yb, dest, gates, x1, mod, ln_g, ln_b, t0, nt, n_ctx_tiles):
    B, T, D = x1.shape
    row = lambda a: a.reshape(1, -1)
    full = lambda a: pl.BlockSpec(a.shape, lambda b, t: (0,) * a.ndim)
    return pl.pallas_call(
        _combine_kernel, grid=(B, nt),
        in_specs=[pl.BlockSpec((N_ROW_DMA,), lambda b, t: (b * nt + t,), memory_space=pltpu.SMEM),
                  pl.BlockSpec((1, ROW_TILE, D), lambda b, t: (b, t + t0, 0)),
                  pl.BlockSpec((ROW_TILE, TOP_K), lambda b, t: (b * nt + t, 0)),
                  pl.BlockSpec((1, 1, 1, D), lambda b, t: (b, jnp.where(t + t0 < n_ctx_tiles, 1, 0), 0, 0)),
                  full(row(ln_g)), full(row(ln_b)),
                  pl.BlockSpec(memory_space=pl.ANY)],
        out_specs=pl.BlockSpec((1, ROW_TILE, D), lambda b, t: (b, t, 0)),
        out_shape=jax.ShapeDtypeStruct((B, nt * ROW_TILE, D), F32),
        scratch_shapes=[pltpu.VMEM((TOP_K, ROW_TILE, D), F32), pltpu.SemaphoreType.DMA],
        compiler_params=pltpu.CompilerParams(dimension_semantics=("arbitrary", "arbitrary")),
        name="moe_combine")(dest, x1, gates, mod, row(ln_g), row(ln_b), yb)


def moe_ffn_residual(h2p, logits, x1, gate2, w1, b1, w2, b2, ln_g, ln_b, t0, nt, n_ctx_tiles):
    B = h2p.shape[0]
    N = B * nt * ROW_TILE
    dest, gates, block_e, first, n_used = moe_route(logits.reshape(N, N_EXPERTS))
    n_rows = (N * TOP_K // MOE_TM + N_EXPERTS) * MOE_TM
    xb = moe_dispatch(h2p, dest, n_rows, t0, nt)
    yb = moe_experts(xb, block_e, first, n_used, w1, b1, w2, b2)
    return moe_combine(yb, dest, gates, x1, gate2, ln_g, ln_b, t0, nt, n_ctx_tiles)


GLA_TILE = 256
GLA_BASE = 16
GLA_LEVELS = (16, 32, 64, 128)
N_SEL = 2 * (len(GLA_LEVELS) + 1)


def gla_constants():
    n = GLA_TILE
    i = np.arange(n)[:, None]
    t = np.arange(n)[None, :]
    sizes = GLA_LEVELS + (n,)
    P = [((t <= i) & (t // s == i // s)) for s in sizes]
    S = [((t > i) & (t // s == i // s)) for s in sizes]
    sel_f = np.concatenate(P + S, 0)
    sel_b = np.concatenate([p.T for p in P] + [s.T for s in S], 0)
    lv = [((i // (2 * s) == t // (2 * s)) & ((i // s) % 2 == 1) & ((t // s) % 2 == 0)) for s in GLA_LEVELS]
    dg = (i // GLA_BASE == t // GLA_BASE) & (t <= i)
    msk_f = np.stack(lv + [dg], 0)
    msk_b = np.stack([m.T for m in lv] + [dg.T], 0)
    return (jnp.asarray(np.stack([sel_f, sel_b], 0), BF16), jnp.asarray(np.stack([msk_f, msk_b], 0), F32))


def _gla_kernel(qf_ref, kf_ref, vf_ref, af_ref, qb_ref, kb_ref, vb_ref, ab_ref,
                wf_ref, bf_ref, wb_ref, bb_ref, sel_ref, msk_ref, of_ref, ob_ref, state):
    TL = GLA_TILE
    NL = len(GLA_LEVELS)

    @pl.when(pl.program_id(1) == 0)
    def _():
        state[...] = jnp.zeros_like(state)

    lane = lax.broadcasted_iota(jnp.int32, (1, 2 * GLA_DK), 1)
    nt = lambda a, b: lax.dot_general(a, b, (((1,), (1,)), ((), ())), preferred_element_type=F32)
    dirs = ((qf_ref, kf_ref, vf_ref, af_ref, wf_ref, bf_ref, of_ref, TL - 1),
            (qb_ref, kb_ref, vb_ref, ab_ref, wb_ref, bb_ref, ob_ref, 0))
    for d, (q_ref, k_ref, v_ref, a_ref, w_ref, b_ref, o_ref, last) in enumerate(dirs):
        x = jnp.dot(a_ref[0], w_ref[...], preferred_element_type=F32) + b_ref[...]
        g = jax.nn.log_sigmoid(x) * (1.0 / GLA_TAU)
        g1 = g.astype(BF16)
        g2 = (g - g1.astype(F32)).astype(BF16)
        sel = sel_ref[d]
        ps = jnp.dot(sel, g1, preferred_element_type=F32) + jnp.dot(sel, g2, preferred_element_type=F32)
        seg = lambda j: ps[j * TL:(j + 1) * TL]
        q = q_ref[0].astype(F32) * (GLA_DK ** -0.5)
        k = k_ref[0].astype(F32)
        e_pre = [jnp.exp(seg(j)) for j in range(NL + 1)]
        e_suf = [jnp.exp(seg(NL + 1 + j)) for j in range(NL + 1)]
        qh = [(q * e).astype(BF16) for e in e_pre]
        kh = [(k * e).astype(BF16) for e in e_suf]
        kd = (k * jnp.exp(-seg(0))).astype(BF16)
        dec = e_pre[NL][last:last + 1]
        for h in range(GLA_HEADS):
            pr = slice((h // 2) * 2 * GLA_DK, (h // 2 + 1) * 2 * GLA_DK)
            mine = (lane // GLA_DK) == (h % 2)
            qm = [jnp.where(mine, t[:, pr], 0) for t in qh[:NL]]
            sc = msk_ref[d, NL] * nt(qm[0], kd[:, pr])
            for j in range(NL):
                sc = sc + msk_ref[d, j] * nt(qm[j], kh[j][:, pr])
            v = v_ref[0, :, h * GLA_DV:(h + 1) * GLA_DV].astype(BF16)
            st = state[d, h]
            o = jnp.dot(sc.astype(BF16), v, preferred_element_type=F32) + nt(qh[NL][:, pr], st.astype(BF16))
            o_ref[0, :, h * GLA_DV:(h + 1) * GLA_DV] = o
            upd = lax.dot_general(v, kh[NL][:, pr], (((0,), (0,)), ((), ())), preferred_element_type=F32)
            state[d, h] = jnp.where(mine, dec[:, pr] * st + upd, 0)


def gla_scan(P, wa2_f, ba_f, wa2_b, ba_b, n_ctx):
    B, T, _ = P.shape
    assert GLA_TILE == ROW_TILE and T % ROW_TILE == 0 and n_ctx % ROW_TILE == 0
    NT, NC = T // ROW_TILE, n_ctx // ROW_TILE
    sel, msk = gla_constants()
    fwd = lambda b, n: n
    bwd = lambda b, n: jnp.where(n < NC, NC - 1 - n, NT - 1 - (n - NC))
    full = lambda a: pl.BlockSpec(a.shape, lambda b, n: (0,) * a.ndim)
    HK, HV = GLA_HEADS * GLA_DK, GLA_HEADS * GLA_DV
    pad = lambda w, off: jnp.zeros((HEAD_PAD, HK), F32).at[off:off + GLA_RANK].set(w).astype(BF16)
    wf, wb = pad(wa2_f, MISC_AF), pad(wa2_b, MISC_AB)
    ba_f2, ba_b2 = ba_f.reshape(1, HK), ba_b.reshape(1, HK)
    out = lambda rm: pl.BlockSpec((1, ROW_TILE, HV), lambda b, n: (b, rm(b, n), 0))
    return pl.pallas_call(
        _gla_kernel,
        grid=(B, NT),
        in_specs=[pcol('q', fwd), pcol('k', fwd), pcol('v', fwd), pcol('misc', fwd),
                  pcol('q', bwd), pcol('k', bwd), pcol('v', bwd), pcol('misc', bwd),
                  full(wf), full(ba_f2), full(wb), full(ba_b2), full(sel), full(msk)],
        out_specs=[out(fwd), out(bwd)],
        out_shape=[jax.ShapeDtypeStruct((B, T, HV), F32)] * 2,
        scratch_shapes=[pltpu.VMEM((2, GLA_HEADS, GLA_DV, 2 * GLA_DK), F32)],
        compiler_params=pltpu.CompilerParams(dimension_semantics=("arbitrary", "arbitrary"),
                                             vmem_limit_bytes=48 * 1024 * 1024),
        name="gla_scan",
    )(P, P, P, P, P, P, P, P, wf, ba_f2, wb, ba_b2, sel, msk)


def hyena_kernels(L, w1, b1, w2, b2, w3, freq):
    t = jnp.linspace(0.0, 1.0, L, dtype=F32)[:, None]
    w = 2 * math.pi * jnp.arange(L, dtype=F32)[:, None] / L
    f = jnp.linspace(1e-4, HY_BANDS - 1, HY_BANDS, dtype=F32)
    z = jnp.concatenate([t, jnp.cos(f * w), -jnp.sin(f * w)], -1)
    ext = lambda a: jnp.concatenate([a, a[:1], jnp.flip(a[1:], 0)], 0)
    z, t = ext(z), ext(t)
    fr = freq.astype(F32)
    h = jnp.sin(fr * (z @ w1.astype(F32) + b1.astype(F32)))
    h = jnp.sin(fr * (h @ w2.astype(F32) + b2.astype(F32)))
    w3r = w3.astype(F32).reshape(HY_FFN, HY_ORDER, HY_DIRS, HY_W)
    deltas = jnp.abs(jnp.linspace(math.log(HY_DECAY_TARGET) / HY_SLOW_PCT,
                                  math.log(HY_DECAY_TARGET) / HY_FAST_PCT, HY_W, dtype=F32))
    win = jnp.exp(-t * deltas)
    not_tap0 = (jnp.arange(L) > 0)[:, None]
    out = []
    for o in range(HY_ORDER):
        hf = (h[:L] @ w3r[:, o, 0]) * win[:L]
        hb = (h[L:] @ w3r[:, o, 1]) * win[L:]
        norm = jnp.sum(jnp.abs(hf), 0) + jnp.sum(jnp.abs(hb), 0)
        out.append(jnp.concatenate([hf, jnp.where(not_tap0, hb, 0.0)], 0) / (norm * (2 * L)))
    return out


FFT_R = 128
FFT_N = FFT_R * FFT_R
FFT_COLS = 2048


def _dft_parts(n):
    ang = 2.0 * np.pi * np.outer(np.arange(n), np.arange(n)) / n
    return np.cos(ang), np.sin(ang)


def hyena_fft_constants():
    C, S = _dft_parts(FFT_R)
    h = FFT_R // 2
    m_first = np.block([[C[:, :h], S[:, :h]], [-S[:, :h], C[:, :h]]])
    m_first_real = np.concatenate([C, -S], 0)
    m_mid = np.block([[C, S], [-S, C]])
    m_mid_inv = np.block([[C, -S], [S, C]])
    m_last = np.block([[C[:h], -S[:h]], [S[:h], C[:h]]])
    ang = 2.0 * np.pi * np.outer(np.arange(FFT_R), np.arange(FFT_R)) / FFT_N
    tw = np.stack([np.cos(ang), np.sin(ang)], -1)
    bf = lambda a: jnp.asarray(a, BF16)
    return dict(first=bf(m_first), first_real=bf(m_first_real), mid=bf(m_mid), mid_inv=bf(m_mid_inv),
                last=bf(m_last), tw=jnp.asarray(tw, F32))


def _fft_outer_kernel(m_ref, x_ref, o_ref):
    o_ref[...] = jnp.dot(m_ref[...], x_ref[...].astype(BF16), preferred_element_type=F32).astype(o_ref.dtype)


def fft_first(m, x2d):
    R, W = x2d.shape
    return pl.pallas_call(
        _fft_outer_kernel, grid=(W // FFT_COLS,),
        in_specs=[pl.BlockSpec(m.shape, lambda j: (0, 0)), pl.BlockSpec((R, FFT_COLS), lambda j: (0, j))],
        out_specs=pl.BlockSpec((m.shape[0], FFT_COLS), lambda j: (0, j)),
        out_shape=jax.ShapeDtypeStruct((m.shape[0], W), BF16),
        compiler_params=pltpu.CompilerParams(dimension_semantics=("arbitrary",)),
        name="hyena_fft_first")(m, x2d)


def _fft_mid_kernel(*refs, conv):
    if conv:
        a_ref, tw_ref, mf_ref, mi_ref, h_ref, o_ref = refs
    else:
        a_ref, tw_ref, mf_ref, o_ref = refs
    R = FFT_R
    tc = tw_ref[0, :, 0:1]
    ts = tw_ref[0, :, 1:2]
    ar = a_ref[0, 0].astype(F32)
    ai = a_ref[1, 0].astype(F32)
    x = jnp.concatenate([ar * tc + ai * ts, ai * tc - ar * ts], 0).astype(BF16)
    X = jnp.dot(mf_ref[...], x, preferred_element_type=F32)
    if not conv:
        o_ref[0, 0] = X[:R]
        o_ref[1, 0] = X[R:]
        return
    xr, xi = X[:R], X[R:]
    hr, hi = h_ref[0, 0], h_ref[1, 0]
    y = jnp.concatenate([xr * hr - xi * hi, xr * hi + xi * hr], 0).astype(BF16)
    Bm = jnp.dot(mi_ref[...], y, preferred_element_type=F32)
    br, bi = Bm[:R], Bm[R:]
    o_ref[0, 0] = (br * tc - bi * ts).astype(o_ref.dtype)
    o_ref[1, 0] = (bi * tc + br * ts).astype(o_ref.dtype)


def fft_mid(a, consts, h=None):
    C = a.shape[-1]
    slab = pl.BlockSpec((2, 1, FFT_R, C), lambda k: (0, k, 0, 0))
    twspec = pl.BlockSpec((1, FFT_R, 2), lambda k: (k, 0, 0))
    mspec = pl.BlockSpec((2 * FFT_R, 2 * FFT_R), lambda k: (0, 0))
    conv = h is not None
    ins = [a, consts['tw'], consts['mid']] + ([consts['mid_inv'], h] if conv else [])
    specs = [slab, twspec, mspec] + ([mspec, slab] if conv else [])
    return pl.pallas_call(
        functools.partial(_fft_mid_kernel, conv=conv), grid=(FFT_R,),
        in_specs=specs, out_specs=slab,
        out_shape=jax.ShapeDtypeStruct((2, FFT_R, FFT_R, C), BF16 if conv else F32),
        compiler_params=pltpu.CompilerParams(dimension_semantics=("arbitrary",)),
        name="hyena_fft_mid")(*ins)


def _fft_last_kernel(m_ref, b_ref, z_ref, bias_ref, gate_ref, o_ref):
    y = jnp.dot(m_ref[...], b_ref[...], preferred_element_type=F32)
    o_ref[...] = gate_ref[...] * (y + bias_ref[...] * z_ref[...])


def fft_last(m, b2d, z2d, bias_row, gate2d):
    R, W = z2d.shape
    col = lambda r: pl.BlockSpec((r, FFT_COLS), lambda j: (0, j))
    return pl.pallas_call(
        _fft_last_kernel, grid=(W // FFT_COLS,),
        in_specs=[pl.BlockSpec(m.shape, lambda j: (0, 0)), col(b2d.shape[0]), col(R),
                  pl.BlockSpec((1, FFT_COLS), lambda j: (0, 0)), col(R)],
        out_specs=col(R),
        out_shape=jax.ShapeDtypeStruct((R, W), F32),
        compiler_params=pltpu.CompilerParams(dimension_semantics=("arbitrary",)),
        name="hyena_fft_last")(m, b2d, z2d, bias_row, gate2d)


def filter_spectrum(k, consts):
    C = k.shape[1]
    a = fft_first(consts['first_real'], k.reshape(FFT_R, FFT_R * C))
    return fft_mid(a.reshape(2, FFT_R, FFT_R, C), consts)


def long_conv(z, gate, spec, bias, consts):
    B, L, C = z.shape
    assert B == 2 and 2 * L == FFT_N and FFT_COLS % C == 0
    z2d = z.reshape(FFT_R, FFT_R * C)
    a = fft_first(consts['first'], z2d)
    b = fft_mid(a.reshape(2, FFT_R, FFT_R, C), consts, spec)
    bias_row = jnp.tile(bias.reshape(1, C), (1, FFT_COLS // C))
    y = fft_last(consts['last'], b.reshape(2 * FFT_R, FFT_R * C), z2d, bias_row, gate.reshape(FFT_R, FFT_R * C))
    return y.reshape(B, L, C)


def dense_dft_constants(L):
    C, S = _dft_parts(2 * L)
    m_fwd = np.block([[C[:, :L], S[:, :L]], [-S[:, :L], C[:, :L]]])
    m_fwd_real = np.concatenate([C, -S], 0)
    m_inv = np.block([[C[:L], -S[:L]], [S[:L], C[:L]]])
    bf = lambda a: jnp.asarray(a, BF16)
    return bf(m_fwd), bf(m_fwd_real), bf(m_inv)


def _short_conv_kernel(mf_ref, mfr_ref, mi_ref, z_ref, k_ref, bias_ref, gate_ref, o_ref):
    n = k_ref.shape[0]
    z = z_ref[...]
    X = jnp.dot(mf_ref[...], z.astype(BF16), preferred_element_type=F32)
    Hs = jnp.dot(mfr_ref[...], k_ref[...].astype(BF16), preferred_element_type=F32)
    xr, xi, hr, hi = X[:n], X[n:], Hs[:n], Hs[n:]
    y = jnp.concatenate([xr * hr - xi * hi, xr * hi + xi * hr], 0).astype(BF16)
    o_ref[...] = gate_ref[...] * (jnp.dot(mi_ref[...], y, preferred_element_type=F32) + bias_ref[...] * z)


def short_conv(z, gate, k, bias):
    B, L, C = z.shape
    assert B == 2
    mf, mfr, mi = dense_dft_constants(L)
    y = pl.pallas_call(
        _short_conv_kernel,
        out_shape=jax.ShapeDtypeStruct((2 * L, C), F32),
        name="hyena_short_conv")(mf, mfr, mi, z.reshape(2 * L, C), k, bias.reshape(1, C), gate.reshape(2 * L, C))
    return y.reshape(B, L, C)


HALO = 8


def _hy_pre_kernel(x_ref, prev_ref, next_ref, w_ref, b_ref, v_ref, x1_ref, x2_ref, *, nt):
    t = pl.program_id(1)
    x = x_ref[0].astype(F32)
    row = lax.broadcasted_iota(jnp.int32, (ROW_TILE, 1), 0)
    prev_row = jnp.where(t > 0, prev_ref[0, HALO - 1:HALO].astype(F32), 0.0)
    next_row = jnp.where(t < nt - 1, next_ref[0, 0:1].astype(F32), 0.0)
    below = jnp.where(row == 0, prev_row, pltpu.roll(x, 1, 0))
    above = jnp.where(row == ROW_TILE - 1, next_row, pltpu.roll(x, ROW_TILE - 1, 0))
    y = below * w_ref[0:1] + x * w_ref[1:2] + above * w_ref[2:3] + b_ref[...]
    v_ref[0] = y[:, :HY_W]
    x1_ref[0] = y[:, HY_W:2 * HY_W]
    x2_ref[0] = y[:, 2 * HY_W:]


def hyena_pre(P, t0, nt, conv_w, conv_b):
    B, T, _ = P.shape
    cb = COL['hy'] // N_HY
    per = ROW_TILE // HALO
    last_blk = T // HALO - 1
    out = pl.BlockSpec((1, ROW_TILE, HY_W), lambda b, t: (b, t, 0))
    return pl.pallas_call(
        functools.partial(_hy_pre_kernel, nt=nt), grid=(B, nt),
        in_specs=[pl.BlockSpec((1, ROW_TILE, N_HY), lambda b, t: (b, t + t0, cb)),
                  pl.BlockSpec((1, HALO, N_HY), lambda b, t: (b, jnp.maximum((t + t0) * per - 1, 0), cb)),
                  pl.BlockSpec((1, HALO, N_HY), lambda b, t: (b, jnp.minimum((t + t0 + 1) * per, last_blk), cb)),
                  pl.BlockSpec((3, N_HY), lambda b, t: (0, 0)),
                  pl.BlockSpec((1, N_HY), lambda b, t: (0, 0))],
        out_specs=[out, out, out],
        out_shape=[jax.ShapeDtypeStruct((B, nt * ROW_TILE, HY_W), F32)] * 3,
        compiler_params=pltpu.CompilerParams(dimension_semantics=("arbitrary", "arbitrary")),
        name="hyena_pre")(P, P, P, conv_w, conv_b.reshape(1, N_HY))


def hyena_branch(P, with_ctx, n_ctx, conv_w, conv_b, w1, b1, w2, b2, w3, freq, hbias):
    B, T, _ = P.shape
    consts = hyena_fft_constants()
    hy_w = (w1, b1, w2, b2, w3, freq)
    nc = n_ctx // ROW_TILE
    v, x1, x2 = hyena_pre(P, nc, T // ROW_TILE - nc, conv_w, conv_b)
    k = hyena_kernels(T - n_ctx, *hy_w)
    z = long_conv(v, x1, filter_spectrum(k[0], consts), hbias[0], consts)
    o = long_conv(z, x2, filter_spectrum(k[1], consts), hbias[1], consts)
    if with_ctx:
        vc, xc1, xc2 = hyena_pre(P, 0, nc, conv_w, conv_b)
        kc = hyena_kernels(n_ctx, *hy_w)
        oc = short_conv(short_conv(vc, xc1, kc[0], hbias[0]), xc2, kc[1], hbias[1])
    else:
        oc = jnp.zeros((B, n_ctx, HY_W), F32)
    return o, oc


ROUTER_PAD = 128


def _merge_kernel(of_ref, ob_ref, r_ref, om_ref, ohl_ref, ohc_ref, g_ref, x_ref, mod_ref,
                  wg_ref, wm_ref, wh_ref, wo_ref, gn_ref, l1g_ref, l1b_ref, rw_ref, rb_ref,
                  x1_ref, h2_ref, lg_ref, *, n_ctx_tiles):
    o = of_ref[0] + ob_ref[0]
    parts = [_rms(o[:, h * GLA_DV:(h + 1) * GLA_DV], gn_ref[...]) for h in range(GLA_HEADS)]
    og = (jnp.concatenate(parts, axis=1) * jax.nn.silu(r_ref[0].astype(F32))).astype(BF16)
    oh = jnp.where(pl.program_id(1) < n_ctx_tiles, ohc_ref[0], ohl_ref[0]).astype(BF16)
    D = D_MODEL
    g = g_ref[0]
    y = (jax.nn.sigmoid(g[:, :D].astype(F32)) * jnp.dot(og, wg_ref[...], preferred_element_type=F32)
         + jax.nn.sigmoid(g[:, D:2 * D].astype(F32)) * jnp.dot(om_ref[0], wm_ref[...], preferred_element_type=F32)
         + jax.nn.sigmoid(g[:, 2 * D:].astype(F32)) * jnp.dot(oh, wh_ref[...], preferred_element_type=F32))
    ym = jnp.dot(y.astype(BF16), wo_ref[...], preferred_element_type=F32)
    gate1, sh2, sc2 = mod_ref[0, 0, 0:1], mod_ref[0, 0, 1:2], mod_ref[0, 0, 2:3]
    x1 = _layer_norm(DEEPNORM_ALPHA * x_ref[0] + gate1 * ym) * l1g_ref[...] + l1b_ref[...]
    x1_ref[0] = x1
    h2f = _layer_norm(x1) * (1.0 + sc2) + sh2
    h2_ref[0] = pack_bf16_pairs(h2f)
    h2 = h2f.astype(BF16)
    lg_ref[0] = jnp.dot(h2, rw_ref[...], preferred_element_type=F32) + rb_ref[...]


def merge_block(o_f, o_b, P, o_mla, o_hy_lat, o_hy_ctx, xa, mod, w_br_gla, w_br_mla, w_br_hy, w_out, gla_norm,
                ln1_g, ln1_b, router_w, router_b, n_ctx):
    B, T, D = xa.shape
    NC = n_ctx // ROW_TILE
    bf = lambda a: a.astype(BF16)
    row = lambda a: a.reshape(1, -1)
    rw = jnp.pad(router_w, ((0, 0), (0, ROUTER_PAD - N_EXPERTS))).astype(BF16)
    rb = jnp.pad(router_b, (0, ROUTER_PAD - N_EXPERTS)).reshape(1, -1)
    full = lambda a: pl.BlockSpec(a.shape, lambda b, t: (0,) * a.ndim)
    tile = lambda w: pl.BlockSpec((1, ROW_TILE, w), lambda b, t: (b, t, 0))
    lat = pl.BlockSpec((1, ROW_TILE, BRANCH_W), lambda b, t: (b, jnp.maximum(t - NC, 0), 0))
    ctx = pl.BlockSpec((1, ROW_TILE, BRANCH_W), lambda b, t: (b, jnp.minimum(t, NC - 1), 0))
    ws = [bf(w_br_gla), bf(w_br_mla), bf(w_br_hy), bf(w_out), row(gla_norm), row(ln1_g), row(ln1_b), rw, rb]
    return pl.pallas_call(
        functools.partial(_merge_kernel, n_ctx_tiles=NC), grid=(B, T // ROW_TILE),
        in_specs=[tile(BRANCH_W), tile(BRANCH_W), pcol('r'), tile(BRANCH_W), lat, ctx, pcol('gate'), tile(D),
                  pl.BlockSpec((1, 1, 3, D), _tile_kind(NC))] + [full(w) for w in ws],
        out_specs=[tile(D), tile(HALF_D), tile(ROUTER_PAD)],
        out_shape=[jax.ShapeDtypeStruct((B, T, D), F32), jax.ShapeDtypeStruct((B, T, HALF_D), U32),
                   jax.ShapeDtypeStruct((B, T, ROUTER_PAD), F32)],
        compiler_params=pltpu.CompilerParams(dimension_semantics=("arbitrary", "arbitrary"),
                                             vmem_limit_bytes=48 * 1024 * 1024),
        name="merge_block")(o_f, o_b, P, o_mla, o_hy_lat, o_hy_ctx, P, xa, mod, *ws)


def kernel(x, c, ctx, c_ctx, ada_w, ada_b, w_in, gla_wa2_f, gla_ba_f, gla_wa2_b, gla_ba_b, gla_norm,
           mla_q_norm, mla_w_uq, mla_kv_norm, mla_w_ukv, hy_conv_w, hy_conv_b, hy_w1, hy_b1, hy_w2, hy_b2,
           hy_w3, hy_freq, hy_bias, w_br_gla, w_br_mla, w_br_hy, w_out, ln1_g, ln1_b, ln2_g, ln2_b,
           router_w, router_b, moe_w1, moe_b1, moe_w2, moe_b2):
    B, L, D = x.shape
    CL = ctx.shape[1]
    T = CL + L
    tables = rope_tables(L, CL)
    xa = jnp.concatenate([ctx, x], axis=1)
    cond = jnp.concatenate([jax.nn.silu(c), jax.nn.silu(c_ctx)[None], jnp.zeros((8 - B - 1, D), F32)], 0)
    for l in range(DEPTH):
        last = l == DEPTH - 1
        ada = pmm(cond, ada_w[l])[:B + 1] + ada_b[l]
        ada = jnp.stack([ada[:B], jnp.broadcast_to(ada[B:], (B, 6 * D))], 1).reshape(B, 2, 6, D)
        P = ln_mod_proj(xa, ada[:, :, 0:2], permute_w_in(w_in[l]), CL)
        o_f, o_b = gla_scan(P, gla_wa2_f[l], gla_ba_f[l], gla_wa2_b[l], gla_ba_b[l], CL)
        q, k, v = mla_prep(P, tables, mla_weights(mla_w_uq[l], mla_w_ukv[l]), mla_q_norm[l], mla_kv_norm[l])
        o_mla = mla_attention(q, k, v, CL)
        o_hy, o_hy_ctx = hyena_branch(P, not last, CL, hy_conv_w[l], hy_conv_b[l], hy_w1[l], hy_b1[l], hy_w2[l],
                                      hy_b2[l], hy_w3[l], hy_freq[l], hy_bias[l])
        x1, h2, logits = merge_block(o_f, o_b, P, o_mla, o_hy, o_hy_ctx, xa, ada[:, :, 2:5],
                                     w_br_gla[l], w_br_mla[l], w_br_hy[l], w_out[l], gla_norm[l],
                                     ln1_g[l], ln1_b[l], router_w[l], router_b[l], CL)
        t0 = CL // ROW_TILE if last else 0
        nt = T // ROW_TILE - t0
        xa = moe_ffn_residual(h2, logits[:, t0 * ROW_TILE:, :N_EXPERTS], x1, ada[:, :, 5:6],
                              moe_w1[l], moe_b1[l], moe_w2[l], moe_b2[l], ln2_g[l], ln2_b[l],
                              t0, nt, CL // ROW_TILE)
    return xa
```

```python
import functools
import math
import jax
import jax.numpy as jnp
from jax import lax
import numpy as np
from jax.experimental import pallas as pl
from jax.experimental.pallas import tpu as pltpu

D_MODEL = 1024
DEPTH = 2
GRID_W = 64
BRANCH_W = D_MODEL // 2
N_BRANCH = 3
GLA_HEADS = 4
GLA_DV = BRANCH_W // GLA_HEADS
GLA_DK = GLA_DV // 2
GLA_RANK = 16
GLA_TAU = 16.0
GLA_CHUNK = 16
MLA_HEADS = 8
MLA_DV = BRANCH_W // MLA_HEADS
MLA_NOPE = MLA_DV
MLA_ROPE = MLA_NOPE // 2
MLA_Q_RANK = 3 * D_MODEL // 8
MLA_KV_RANK = D_MODEL // 4
ROPE_BASE = 10000.0
Q_BLOCK = 128
HY_W = BRANCH_W
HY_ORDER = 2
HY_DIRS = 2
HY_BANDS = 16
HY_EMB = 2 * HY_BANDS + 1
HY_FFN = 64
HY_DECAY_TARGET = 1e-2
HY_FAST_PCT = 0.3
HY_SLOW_PCT = 1.5
N_EXPERTS = 32
TOP_K = 4
D_FF = D_MODEL
SWIGLU_LIMIT = 7.0
SWIGLU_ALPHA = 1.702
MOE_BLOCK = 128
LN_EPS = 1e-5
RMS_EPS = 1e-6
DEEPNORM_ALPHA = (2 * DEPTH) ** 0.25
IN_SPLITS = (
    GLA_HEADS * GLA_DK, GLA_HEADS * GLA_DV, GLA_RANK, GLA_RANK, MLA_KV_RANK, MLA_ROPE,
    GLA_HEADS * GLA_DK, GLA_HEADS * GLA_DV, MLA_Q_RANK, 3 * HY_W, N_BRANCH * D_MODEL,
)
N_KEY_GROUPS = 6
KEY_COLS = sum(IN_SPLITS[:N_KEY_GROUPS])
IN_OFFSETS = tuple(int(o) for o in np.cumsum(IN_SPLITS)[:-1])
IN_TOTAL = sum(IN_SPLITS)
F32 = jnp.float32
BF16 = jnp.bfloat16
U32 = jnp.uint32


def _mm_kernel(a_ref, w_ref, o_ref):
    o_ref[...] = jnp.dot(a_ref[...], w_ref[...], preferred_element_type=F32)


def _pick(n, cands):
    for c in cands:
        if n % c == 0:
            return c
    return n


def pmm(a, w):
    lead = a.shape[:-1]
    K = a.shape[-1]
    N = w.shape[-1]
    a2 = a.reshape(-1, K).astype(BF16)
    M = a2.shape[0]
    N0 = N
    if N % 128:
        N = -(-N // 256) * 256
        w = jnp.pad(w, ((0, 0), (0, N - N0)))
    tm = _pick(M, (512, 256, 128, 64, 32, 16, 8))
    tn = _pick(N, (512, 256, 128))
    out = pl.pallas_call(
        _mm_kernel,
        grid=(M // tm, N // tn),
        in_specs=[pl.BlockSpec((tm, K), lambda i, j: (i, 0)),
                  pl.BlockSpec((K, tn), lambda i, j: (0, j))],
        out_specs=pl.BlockSpec((tm, tn), lambda i, j: (i, j)),
        out_shape=jax.ShapeDtypeStruct((M, N), F32),
    )(a2, w.astype(BF16))
    return out[:, :N0].reshape(lead + (N0,))


ROW_TILE = 256
HEAD_PAD = 128
N_GATE = N_BRANCH * D_MODEL
N_HY = 3 * HY_W
COL = dict(gate=0, hy=N_GATE, qa=N_GATE + N_HY)
COL['misc'] = COL['qa'] + MLA_Q_RANK
COL['v'] = COL['misc'] + HEAD_PAD
COL['r'] = COL['v'] + GLA_HEADS * GLA_DV
COL['k'] = COL['r'] + GLA_HEADS * GLA_DV
COL['q'] = COL['k'] + GLA_HEADS * GLA_DK
COL['kva'] = COL['q'] + GLA_HEADS * GLA_DK
P_COLS = COL['kva'] + MLA_KV_RANK
WIDTH = dict(gate=N_GATE, hy=N_HY, qa=MLA_Q_RANK, misc=HEAD_PAD, v=GLA_HEADS * GLA_DV, r=GLA_HEADS * GLA_DV,
             k=GLA_HEADS * GLA_DK, q=GLA_HEADS * GLA_DK, kva=MLA_KV_RANK)
MISC_AF, MISC_AB, MISC_KR = 0, GLA_RANK, MLA_NOPE
P_CHUNK = 768
assert all(COL[g] % WIDTH[g] == 0 for g in COL) and P_COLS % P_CHUNK == 0


def permute_w_in(w_in):
    o = (0,) + IN_OFFSETS + (IN_TOTAL,)
    k, v, af, ab, kva, kr, q, r, qa, hy, gate = [w_in[:, o[i]:o[i + 1]] for i in range(len(IN_SPLITS))]
    z = lambda n: jnp.zeros((w_in.shape[0], n), w_in.dtype)
    misc = jnp.concatenate([af, ab, z(MISC_KR - 2 * GLA_RANK), kr, z(HEAD_PAD - MISC_KR - MLA_ROPE)], 1)
    return jnp.concatenate([gate, hy, qa, misc, v, r, k, q, kva], 1).astype(BF16)


def _layer_norm(x):
    xc = x - jnp.mean(x, axis=-1, keepdims=True)
    return xc * lax.rsqrt(jnp.mean(xc * xc, axis=-1, keepdims=True) + LN_EPS)


def _rms(x, gain):
    return x * lax.rsqrt(jnp.mean(x * x, axis=-1, keepdims=True) + RMS_EPS) * gain


def _proj_kernel(x_ref, mod_ref, w_ref, o_ref):
    h = _layer_norm(x_ref[0]) * (1.0 + mod_ref[0, 0, 1:2]) + mod_ref[0, 0, 0:1]
    hb = h.astype(BF16)
    for c in range(0, P_COLS, P_CHUNK):
        o_ref[0, :, c:c + P_CHUNK] = jnp.dot(hb, w_ref[:, c:c + P_CHUNK], preferred_element_type=F32).astype(BF16)


def _tile_kind(n_ctx_tiles):
    return lambda b, t: (b, jnp.where(t < n_ctx_tiles, 1, 0), 0, 0)


def ln_mod_proj(xa, mod, w_p, n_ctx):
    B, T, D = xa.shape
    return pl.pallas_call(
        _proj_kernel, grid=(B, T // ROW_TILE),
        in_specs=[pl.BlockSpec((1, ROW_TILE, D), lambda b, t: (b, t, 0)),
                  pl.BlockSpec((1, 1, 2, D), _tile_kind(n_ctx // ROW_TILE)),
                  pl.BlockSpec((D, P_COLS), lambda b, t: (0, 0), pipeline_mode=pl.Buffered(1))],
        out_specs=pl.BlockSpec((1, ROW_TILE, P_COLS), lambda b, t: (b, t, 0)),
        out_shape=jax.ShapeDtypeStruct((B, T, P_COLS), BF16),
        compiler_params=pltpu.CompilerParams(dimension_semantics=("arbitrary", "arbitrary"),
                                             vmem_limit_bytes=48 * 1024 * 1024),
        name="ln_mod_proj")(xa, mod, w_p)


def pcol(group, row_map=lambda b, t: t):
    w = WIDTH[group]
    return pl.BlockSpec((1, ROW_TILE, w), lambda b, t, _i=COL[group] // w: (b, row_map(b, t), _i))


ONES_LANE_EVEN = MLA_DV
ONES_LANE_ODD = 0
LOG2E = 1.4426950408889634


def rope_tables(L, n_ctx):
    rows = L // GRID_W
    row = jnp.repeat(jnp.arange(rows, dtype=F32), GRID_W)
    col = jnp.tile(jnp.arange(GRID_W, dtype=F32), rows)
    n_freq = MLA_ROPE // 4
    inv = ROPE_BASE ** (-jnp.arange(n_freq, dtype=F32) / n_freq)
    ang = jnp.concatenate([row[:, None] * inv, col[:, None] * inv], -1)
    ang = jnp.concatenate([jnp.zeros((n_ctx, ang.shape[1]), F32), ang], 0)
    cos, sin = jnp.cos(ang), jnp.sin(ang)
    T = ang.shape[0]
    half = MLA_ROPE // 2
    z = lambda n: jnp.zeros((T, n), F32)
    ct = jnp.concatenate([jnp.ones((T, MLA_NOPE), F32), cos, cos, z(HEAD_PAD - MLA_NOPE - MLA_ROPE)], 1)
    sa = jnp.concatenate([z(MLA_NOPE), -sin, z(HEAD_PAD - MLA_NOPE - half)], 1)
    sb = jnp.concatenate([z(MLA_NOPE + half), sin, z(HEAD_PAD - MLA_NOPE - MLA_ROPE)], 1)
    return ct, sa, sb


def mla_weights(w_uq, w_ukv):
    H = MLA_HEADS
    wq = w_uq.reshape(MLA_Q_RANK, H, MLA_NOPE + MLA_ROPE)
    wq = jnp.pad(wq, ((0, 0), (0, 0), (0, HEAD_PAD - MLA_NOPE - MLA_ROPE))).reshape(MLA_Q_RANK, H * HEAD_PAD)
    wkv = w_ukv.reshape(MLA_KV_RANK, H, MLA_NOPE + MLA_DV)
    wk = jnp.pad(wkv[:, :, :MLA_NOPE], ((0, 0), (0, 0), (0, HEAD_PAD - MLA_NOPE))).reshape(MLA_KV_RANK, H * HEAD_PAD)
    wv = wkv[:, :, MLA_NOPE:].reshape(MLA_KV_RANK, H // 2, 2, MLA_DV)
    z = jnp.zeros_like(wv[:, :, 0])
    wv = jnp.stack([jnp.concatenate([wv[:, :, 0], z], -1), jnp.concatenate([z, wv[:, :, 1]], -1)], 2)
    ones = np.zeros((H // 2, 2, HEAD_PAD), np.float32)
    ones[:, 0, ONES_LANE_EVEN] = 1.0
    ones[:, 1, ONES_LANE_ODD] = 1.0
    return (wq.astype(BF16), wk.astype(BF16), wv.reshape(MLA_KV_RANK, H * HEAD_PAD).astype(BF16),
            jnp.asarray(ones.reshape(1, H * HEAD_PAD)))


def _mla_prep_kernel(qa_ref, kva_ref, misc_ref, ct_ref, sa_ref, sb_ref, wq_ref, wk_ref, wv_ref, ones_ref,
                     qn_ref, kn_ref, q_ref, k_ref, v_ref):
    ct, sa, sb = ct_ref[...], sa_ref[...], sb_ref[...]
    half = MLA_ROPE // 2
    rope = lambda x: x * ct + pltpu.roll(x, HEAD_PAD - half, 1) * sa + pltpu.roll(x, half, 1) * sb
    qn = _rms(qa_ref[0].astype(F32), qn_ref[...]).astype(BF16)
    q = jnp.dot(qn, wq_ref[...], preferred_element_type=F32)
    kn = _rms(kva_ref[0].astype(F32), kn_ref[...]).astype(BF16)
    k = jnp.dot(kn, wk_ref[...], preferred_element_type=F32)
    v_ref[0] = (jnp.dot(kn, wv_ref[...], preferred_element_type=F32) + ones_ref[...]).astype(BF16)
    lane = lax.broadcasted_iota(jnp.int32, (1, HEAD_PAD), 1)
    is_kr = (lane >= MISC_KR) & (lane < MISC_KR + MLA_ROPE)
    kr = rope(jnp.where(is_kr, misc_ref[0].astype(F32), 0.0))
    qscale = (MLA_NOPE + MLA_ROPE) ** -0.5 * LOG2E
    for h in range(MLA_HEADS):
        hs = slice(h * HEAD_PAD, (h + 1) * HEAD_PAD)
        q_ref[0, :, hs] = (rope(q[:, hs]) * qscale).astype(BF16)
        k_ref[0, :, hs] = (k[:, hs] + kr).astype(BF16)


def mla_prep(P, tables, weights, q_norm, kv_norm):
    B, T, _ = P.shape
    ct, sa, sb = tables
    wq, wk, wv, ones = weights
    full = lambda a: pl.BlockSpec(a.shape, lambda b, t: (0,) * a.ndim)
    tab = pl.BlockSpec((ROW_TILE, HEAD_PAD), lambda b, t: (t, 0))
    qn, kn = q_norm.reshape(1, -1), kv_norm.reshape(1, -1)
    W = MLA_HEADS * HEAD_PAD
    out = pl.BlockSpec((1, ROW_TILE, W), lambda b, t: (b, t, 0))
    return pl.pallas_call(
        _mla_prep_kernel, grid=(B, T // ROW_TILE),
        in_specs=[pcol('qa'), pcol('kva'), pcol('misc'), tab, tab, tab, full(wq), full(wk), full(wv), full(ones),
                  full(qn), full(kn)],
        out_specs=[out, out, out],
        out_shape=[jax.ShapeDtypeStruct((B, T, W), BF16)] * 3,
        compiler_params=pltpu.CompilerParams(dimension_semantics=("arbitrary", "arbitrary")),
        name="mla_prep")(P, P, P, ct, sa, sb, wq, wk, wv, ones, qn, kn)


ATT_TQ = 256
ATT_TK = 768


def _attn_kernel(q_ref, k_ref, v_ref, o_ref, s_buf, p_buf, m_buf, a_buf, mrun, acc, *, n_ctx, tk):
    tq = q_ref.shape[1]
    T = k_ref.shape[1]
    in_ctx = pl.program_id(2) * tq < n_ctx
    lane = lax.broadcasted_iota(jnp.int32, (tq, HEAD_PAD), 1)
    hs = lambda h: slice(h * HEAD_PAD, (h + 1) * HEAD_PAD)

    def stage_a(c, slot, size):
        start = c * size
        for h in range(2):
            k = k_ref[0, pl.ds(start, size), hs(h)]
            s = lax.dot_general(q_ref[0, :, hs(h)], k, (((1,), (1,)), ((), ())), preferred_element_type=F32)
            s_buf[slot, h, :, :size] = s
            m_prev = mrun[h]
            m_new = jnp.maximum(m_prev, jnp.max(s, axis=-1, keepdims=True))
            mrun[h] = m_new
            m_buf[slot, h] = m_new
            a_buf[slot, h] = jnp.exp2(m_prev - m_new)

    def stage_b(slot, size):
        for h in range(2):
            p_buf[slot, h, :, :size] = jnp.exp2(s_buf[slot, h, :, :size] - m_buf[slot, h]).astype(BF16)

    def stage_c(c, slot, size):
        start = c * size
        for h in range(2):
            v = v_ref[0, pl.ds(start, size), hs(h)]
            acc[h] = a_buf[slot, h] * acc[h] + jnp.dot(p_buf[slot, h, :, :size], v, preferred_element_type=F32)

    def run(n, size):
        for h in range(2):
            mrun[h] = jnp.full((tq, 1), -jnp.inf, F32)
            acc[h] = jnp.zeros((tq, HEAD_PAD), F32)
        if n == 1:
            stage_a(0, 0, size)
            stage_b(0, size)
            stage_c(0, 0, size)
        else:
            stage_a(0, 0, size)
            stage_b(0, size)
            stage_a(1, 1, size)

            for c in range(n - 2):
                stage_c(c, c % 2, size)
                stage_b(1 - c % 2, size)
                stage_a(c + 2, c % 2, size)
            stage_c(n - 2, (n - 2) % 2, size)
            stage_b((n - 1) % 2, size)
            stage_c(n - 1, (n - 1) % 2, size)
        a0, a1 = acc[0], acc[1]
        l0 = a0[:, ONES_LANE_EVEN:ONES_LANE_EVEN + 1]
        l1 = a1[:, ONES_LANE_ODD:ONES_LANE_ODD + 1]
        o_ref[0] = jnp.where(lane < MLA_DV, a0 / l0, a1 / l1).astype(o_ref.dtype)

    @pl.when(in_ctx)
    def _():
        run(1, n_ctx)

    @pl.when(jnp.logical_not(in_ctx))
    def _():
        run(T // tk, tk)


def mla_attention(q, k, v, n_ctx):
    B, T, _ = q.shape
    assert T % ATT_TQ == 0 and T % ATT_TK == 0 and n_ctx % ATT_TQ == 0 and n_ctx <= ATT_TK
    kern = functools.partial(_attn_kernel, n_ctx=n_ctx, tk=ATT_TK)
    return pl.pallas_call(
        kern,
        grid=(B, MLA_HEADS // 2, T // ATT_TQ),
        in_specs=[pl.BlockSpec((1, ATT_TQ, 2 * HEAD_PAD), lambda b, h, i: (b, i, h)),
                  pl.BlockSpec((1, T, 2 * HEAD_PAD), lambda b, h, i: (b, 0, h)),
                  pl.BlockSpec((1, T, 2 * HEAD_PAD), lambda b, h, i: (b, 0, h))],
        out_specs=pl.BlockSpec((1, ATT_TQ, 2 * MLA_DV), lambda b, h, i: (b, i, h)),
        out_shape=jax.ShapeDtypeStruct((B, T, MLA_HEADS * MLA_DV), BF16),
        scratch_shapes=[pltpu.VMEM((2, 2, ATT_TQ, ATT_TK), F32), pltpu.VMEM((2, 2, ATT_TQ, ATT_TK), BF16),
                        pltpu.VMEM((2, 2, ATT_TQ, 1), F32), pltpu.VMEM((2, 2, ATT_TQ, 1), F32),
                        pltpu.VMEM((2, ATT_TQ, 1), F32), pltpu.VMEM((2, ATT_TQ, HEAD_PAD), F32)],
        compiler_params=pltpu.CompilerParams(
            dimension_semantics=("arbitrary", "arbitrary", "arbitrary"),
            vmem_limit_bytes=56 * 1024 * 1024),
        name="mla_attention",
    )(q, k, v)


MOE_TM = 256
HALF_D = D_MODEL // 2
HI_MASK = 0xFFFF0000
N_ROW_DMA = ROW_TILE * TOP_K


def pack_bf16_pairs(x):
    u = pltpu.bitcast(x.astype(BF16).astype(F32), U32)
    return (u[:, :HALF_D] >> 16) | (u[:, HALF_D:] & jnp.uint32(HI_MASK))


def unpack_bf16_pairs(w):
    lo = pltpu.bitcast(w << 16, F32)
    hi = pltpu.bitcast(w & jnp.uint32(HI_MASK), F32)
    return jnp.concatenate([lo, hi], axis=1).astype(BF16)


def moe_route(logits):
    N = logits.shape[0]
    top_val, top_idx = lax.top_k(logits, TOP_K)
    gates = jax.nn.softmax(top_val, axis=-1)
    NK = N * TOP_K
    flat_e = top_idx.reshape(NK)
    onehot = (flat_e[:, None] == jnp.arange(N_EXPERTS, dtype=flat_e.dtype)[None, :]).astype(F32)
    nb = NK // MOE_TM
    within = jnp.einsum('ij,bje->bie', jnp.tril(jnp.ones((MOE_TM, MOE_TM), F32)),
                        onehot.reshape(nb, MOE_TM, N_EXPERTS))
    btot = within[:, -1, :].astype(jnp.int32)
    boff = jnp.cumsum(btot, axis=0) - btot
    csum = (within.astype(jnp.int32) + boff[:, None, :]).reshape(NK, N_EXPERTS)
    rank = jnp.take_along_axis(csum, flat_e[:, None], axis=1)[:, 0] - 1
    counts = csum[-1]
    padded = (counts + MOE_TM - 1) // MOE_TM * MOE_TM
    pad_end = jnp.cumsum(padded)
    dest = ((pad_end - padded)[flat_e] + rank).astype(jnp.int32)
    n_blocks = NK // MOE_TM + N_EXPERTS
    blk_start = jnp.arange(n_blocks, dtype=jnp.int32) * MOE_TM
    block_e = jnp.minimum(jnp.sum(blk_start[:, None] >= pad_end[None, :], axis=1), N_EXPERTS - 1).astype(jnp.int32)
    first = jnp.concatenate([jnp.ones((1,), jnp.int32), (block_e[1:] != block_e[:-1]).astype(jnp.int32)])
    n_used = (pad_end[-1:] // MOE_TM).astype(jnp.int32)
    return dest, gates, block_e, first, n_used


def _dispatch_kernel(dest_ref, h_ref, xb_in_ref, xb_ref, sem):
    del xb_in_ref

    def start(r, c):
        for k in range(TOP_K):
            pltpu.make_async_copy(h_ref.at[0, pl.ds(r, 1)], xb_ref.at[pl.ds(dest_ref[r * TOP_K + k], 1)],
                                  sem).start()
        return c

    lax.fori_loop(0, ROW_TILE, start, 0, unroll=4)
    for k in range(TOP_K):
        pltpu.make_async_copy(h_ref.at[0], xb_ref.at[pl.ds(0, ROW_TILE)], sem).wait()


def moe_dispatch(h2p, dest, n_rows, t0, nt):
    B = h2p.shape[0]
    xb0 = jnp.zeros((n_rows, HALF_D), U32)
    return pl.pallas_call(
        _dispatch_kernel, grid=(B, nt),
        in_specs=[pl.BlockSpec((N_ROW_DMA,), lambda b, t: (b * nt + t,), memory_space=pltpu.SMEM),
                  pl.BlockSpec((1, ROW_TILE, HALF_D), lambda b, t: (b, t + t0, 0)),
                  pl.BlockSpec(memory_space=pl.ANY)],
        out_specs=pl.BlockSpec(memory_space=pl.ANY),
        out_shape=jax.ShapeDtypeStruct((n_rows, HALF_D), U32),
        scratch_shapes=[pltpu.SemaphoreType.DMA],
        input_output_aliases={2: 0},
        compiler_params=pltpu.CompilerParams(dimension_semantics=("arbitrary", "arbitrary")),
        name="moe_dispatch")(dest, h2p, xb0)


def _moe_kernel(be_ref, first_ref, nb_ref, x_ref, w1_ref, b1_ref, w2_ref, b2_ref, o_ref, w1_bf, w2_bf):
    i = pl.program_id(0)

    @pl.when(jnp.logical_and(i < nb_ref[0], first_ref[i] == 1))
    def _():
        w1_bf[...] = w1_ref[0].astype(BF16)
        w2_bf[...] = w2_ref[0].astype(BF16)

    @pl.when(i < nb_ref[0])
    def _():
        x = unpack_bf16_pairs(x_ref[...])
        hcat = jnp.dot(x, w1_bf[...], preferred_element_type=F32) + b1_ref[0]
        glu = jnp.minimum(hcat[:, :D_FF], SWIGLU_LIMIT)
        lin = jnp.clip(hcat[:, D_FF:], -SWIGLU_LIMIT, SWIGLU_LIMIT)
        act = glu * jax.nn.sigmoid(SWIGLU_ALPHA * glu) * (lin + 1)
        o_ref[...] = jnp.dot(act.astype(BF16), w2_bf[...], preferred_element_type=F32) + b2_ref[0]

    @pl.when(i >= nb_ref[0])
    def _():
        o_ref[...] = jnp.zeros_like(o_ref)


def moe_experts(xb, block_e, first, n_used, w1, b1, w2, b2, layer):
    R = xb.shape[0]
    D = D_MODEL
    n_blocks = R // MOE_TM
    grid_spec = pltpu.PrefetchScalarGridSpec(
        num_scalar_prefetch=3,
        grid=(n_blocks,),
        in_specs=[pl.BlockSpec((MOE_TM, HALF_D), lambda i, be, fi, nb: (i, 0)),
                  pl.BlockSpec((None, 1, D, 2 * D_FF), lambda i, be, fi, nb: (layer, be[i], 0, 0)),
                  pl.BlockSpec((None, 1, 1, 2 * D_FF), lambda i, be, fi, nb: (layer, be[i], 0, 0)),
                  pl.BlockSpec((None, 1, D_FF, D), lambda i, be, fi, nb: (layer, be[i], 0, 0)),
                  pl.BlockSpec((None, 1, 1, D), lambda i, be, fi, nb: (layer, be[i], 0, 0))],
        out_specs=pl.BlockSpec((MOE_TM, D), lambda i, be, fi, nb: (i, 0)),
        scratch_shapes=[pltpu.VMEM((D, 2 * D_FF), BF16), pltpu.VMEM((D_FF, D), BF16)],
    )
    return pl.pallas_call(
        _moe_kernel,
        grid_spec=grid_spec,
        out_shape=jax.ShapeDtypeStruct((R, D), F32),
        compiler_params=pltpu.CompilerParams(
            dimension_semantics=("arbitrary",), vmem_limit_bytes=56 * 1024 * 1024),
        name="moe_experts",
    )(block_e, first, n_used, xb, w1, b1[:, :, None, :], w2, b2[:, :, None, :])


def _combine_kernel(dest_ref, x_ref, gate_ref, mod_ref, lg_ref, lb_ref, yb_ref, o_ref, buf, sem):
    def start(r, c):
        for k in range(TOP_K):
            pltpu.make_async_copy(yb_ref.at[pl.ds(dest_ref[r * TOP_K + k], 1)], buf.at[k, pl.ds(r, 1)], sem).start()
        return c

    lax.fori_loop(0, ROW_TILE, start, 0, unroll=4)
    for k in range(TOP_K):
        pltpu.make_async_copy(yb_ref.at[pl.ds(0, ROW_TILE)], buf.at[k], sem).wait()
    g = gate_ref[...]
    f = g[:, 0:1] * buf[0]
    for k in range(1, TOP_K):
        f = f + g[:, k:k + 1] * buf[k]
    o_ref[0] = _layer_norm(DEEPNORM_ALPHA * x_ref[0] + mod_ref[0, 0, 0:1] * f) * lg_ref[...] + lb_ref[...]


def moe_combine(yb, dest, gates, x1, mod, ln_g, ln_b, t0, nt, n_ctx_tiles):
    B, T, D = x1.shape
    row = lambda a: a.reshape(1, -1)
    full = lambda a: pl.BlockSpec(a.shape, lambda b, t: (0,) * a.ndim)
    return pl.pallas_call(
        _combine_kernel, grid=(B, nt),
        in_specs=[pl.BlockSpec((N_ROW_DMA,), lambda b, t: (b * nt + t,), memory_space=pltpu.SMEM),
                  pl.BlockSpec((1, ROW_TILE, D), lambda b, t: (b, t + t0, 0)),
                  pl.BlockSpec((ROW_TILE, TOP_K), lambda b, t: (b * nt + t, 0)),
                  pl.BlockSpec((1, 1, 1, D), lambda b, t: (b, jnp.where(t + t0 < n_ctx_tiles, 1, 0), 0, 0)),
                  full(row(ln_g)), full(row(ln_b)),
                  pl.BlockSpec(memory_space=pl.ANY)],
        out_specs=pl.BlockSpec((1, ROW_TILE, D), lambda b, t: (b, t, 0)),
        out_shape=jax.ShapeDtypeStruct((B, nt * ROW_TILE, D), F32),
        scratch_shapes=[pltpu.VMEM((TOP_K, ROW_TILE, D), F32), pltpu.SemaphoreType.DMA],
        compiler_params=pltpu.CompilerParams(dimension_semantics=("arbitrary", "arbitrary")),
        name="moe_combine")(dest, x1, gates, mod, row(ln_g), row(ln_b), yb)


def moe_ffn_residual(h2p, logits, x1, gate2, w1, b1, w2, b2, layer, ln_g, ln_b, t0, nt, n_ctx_tiles):
    B = h2p.shape[0]
    N = B * nt * ROW_TILE
    dest, gates, block_e, first, n_used = moe_route(logits.reshape(N, N_EXPERTS))
    n_rows = (N * TOP_K // MOE_TM + N_EXPERTS) * MOE_TM
    xb = moe_dispatch(h2p, dest, n_rows, t0, nt)
    yb = moe_experts(xb, block_e, first, n_used, w1, b1, w2, b2, layer)
    return moe_combine(yb, dest, gates, x1, gate2, ln_g, ln_b, t0, nt, n_ctx_tiles)


GLA_TILE = 256
GLA_BASE = 16
GLA_LEVELS = (16, 32, 64, 128)
N_SEL = 2 * (len(GLA_LEVELS) + 1)


def gla_constants():
    n = GLA_TILE
    i = np.arange(n)[:, None]
    t = np.arange(n)[None, :]
    sizes = GLA_LEVELS + (n,)
    P = [((t <= i) & (t // s == i // s)) for s in sizes]
    S = [((t > i) & (t // s == i // s)) for s in sizes]
    sel_f = np.concatenate(P + S, 0)
    sel_b = np.concatenate([p.T for p in P] + [s.T for s in S], 0)
    lv = [((i // (2 * s) == t // (2 * s)) & ((i // s) % 2 == 1) & ((t // s) % 2 == 0)) for s in GLA_LEVELS]
    dg = (i // GLA_BASE == t // GLA_BASE) & (t <= i)
    msk_f = np.stack(lv + [dg], 0)
    msk_b = np.stack([m.T for m in lv] + [dg.T], 0)
    return (jnp.asarray(np.stack([sel_f, sel_b], 0), BF16), jnp.asarray(np.stack([msk_f, msk_b], 0), F32))


def _gla_kernel(qf_ref, kf_ref, vf_ref, af_ref, qb_ref, kb_ref, vb_ref, ab_ref,
                wf_ref, bf_ref, wb_ref, bb_ref, sel_ref, msk_ref, of_ref, ob_ref, state):
    TL = GLA_TILE
    NL = len(GLA_LEVELS)

    @pl.when(pl.program_id(1) == 0)
    def _():
        state[...] = jnp.zeros_like(state)

    lane = lax.broadcasted_iota(jnp.int32, (1, 2 * GLA_DK), 1)
    nt = lambda a, b: lax.dot_general(a, b, (((1,), (1,)), ((), ())), preferred_element_type=F32)
    dirs = ((qf_ref, kf_ref, vf_ref, af_ref, wf_ref, bf_ref, of_ref, TL - 1),
            (qb_ref, kb_ref, vb_ref, ab_ref, wb_ref, bb_ref, ob_ref, 0))
    for d, (q_ref, k_ref, v_ref, a_ref, w_ref, b_ref, o_ref, last) in enumerate(dirs):
        x = jnp.dot(a_ref[0], w_ref[...], preferred_element_type=F32) + b_ref[...]
        g = jax.nn.log_sigmoid(x) * (1.0 / GLA_TAU)
        g1 = g.astype(BF16)
        g2 = (g - g1.astype(F32)).astype(BF16)
        sel = sel_ref[d]
        ps = jnp.dot(sel, g1, preferred_element_type=F32) + jnp.dot(sel, g2, preferred_element_type=F32)
        seg = lambda j: ps[j * TL:(j + 1) * TL]
        q = q_ref[0].astype(F32) * (GLA_DK ** -0.5)
        k = k_ref[0].astype(F32)
        e_pre = [jnp.exp(seg(j)) for j in range(NL + 1)]
        e_suf = [jnp.exp(seg(NL + 1 + j)) for j in range(NL + 1)]
        qh = [(q * e).astype(BF16) for e in e_pre]
        kh = [(k * e).astype(BF16) for e in e_suf]
        kd = (k * jnp.exp(-seg(0))).astype(BF16)
        dec = e_pre[NL][last:last + 1]
        for h in range(GLA_HEADS):
            pr = slice((h // 2) * 2 * GLA_DK, (h // 2 + 1) * 2 * GLA_DK)
            mine = (lane // GLA_DK) == (h % 2)
            qm = [jnp.where(mine, t[:, pr], 0) for t in qh[:NL]]
            sc = msk_ref[d, NL] * nt(qm[0], kd[:, pr])
            for j in range(NL):
                sc = sc + msk_ref[d, j] * nt(qm[j], kh[j][:, pr])
            v = v_ref[0, :, h * GLA_DV:(h + 1) * GLA_DV].astype(BF16)
            st = state[d, h]
            o = jnp.dot(sc.astype(BF16), v, preferred_element_type=F32) + nt(qh[NL][:, pr], st.astype(BF16))
            o_ref[0, :, h * GLA_DV:(h + 1) * GLA_DV] = o
            upd = lax.dot_general(v, kh[NL][:, pr], (((0,), (0,)), ((), ())), preferred_element_type=F32)
            state[d, h] = jnp.where(mine, dec[:, pr] * st + upd, 0)


def gla_scan(P, wa2_f, ba_f, wa2_b, ba_b, n_ctx):
    B, T, _ = P.shape
    assert GLA_TILE == ROW_TILE and T % ROW_TILE == 0 and n_ctx % ROW_TILE == 0
    NT, NC = T // ROW_TILE, n_ctx // ROW_TILE
    sel, msk = gla_constants()
    fwd = lambda b, n: n
    bwd = lambda b, n: jnp.where(n < NC, NC - 1 - n, NT - 1 - (n - NC))
    full = lambda a: pl.BlockSpec(a.shape, lambda b, n: (0,) * a.ndim)
    HK, HV = GLA_HEADS * GLA_DK, GLA_HEADS * GLA_DV
    pad = lambda w, off: jnp.zeros((HEAD_PAD, HK), F32).at[off:off + GLA_RANK].set(w).astype(BF16)
    wf, wb = pad(wa2_f, MISC_AF), pad(wa2_b, MISC_AB)
    ba_f2, ba_b2 = ba_f.reshape(1, HK), ba_b.reshape(1, HK)
    out = lambda rm: pl.BlockSpec((1, ROW_TILE, HV), lambda b, n: (b, rm(b, n), 0))
    return pl.pallas_call(
        _gla_kernel,
        grid=(B, NT),
        in_specs=[pcol('q', fwd), pcol('k', fwd), pcol('v', fwd), pcol('misc', fwd),
                  pcol('q', bwd), pcol('k', bwd), pcol('v', bwd), pcol('misc', bwd),
                  full(wf), full(ba_f2), full(wb), full(ba_b2), full(sel), full(msk)],
        out_specs=[out(fwd), out(bwd)],
        out_shape=[jax.ShapeDtypeStruct((B, T, HV), F32)] * 2,
        scratch_shapes=[pltpu.VMEM((2, GLA_HEADS, GLA_DV, 2 * GLA_DK), F32)],
        compiler_params=pltpu.CompilerParams(dimension_semantics=("arbitrary", "arbitrary"),
                                             vmem_limit_bytes=48 * 1024 * 1024),
        name="gla_scan",
    )(P, P, P, P, P, P, P, P, wf, ba_f2, wb, ba_b2, sel, msk)


def hyena_kernels(L, w1, b1, w2, b2, w3, freq):
    t = jnp.linspace(0.0, 1.0, L, dtype=F32)[:, None]
    w = 2 * math.pi * jnp.arange(L, dtype=F32)[:, None] / L
    f = jnp.linspace(1e-4, HY_BANDS - 1, HY_BANDS, dtype=F32)
    z = jnp.concatenate([t, jnp.cos(f * w), -jnp.sin(f * w)], -1)
    ext = lambda a: jnp.concatenate([a, a[:1], jnp.flip(a[1:], 0)], 0)
    z, t = ext(z), ext(t)
    fr = freq.astype(F32)
    h = jnp.sin(fr * (z @ w1.astype(F32) + b1.astype(F32)))
    h = jnp.sin(fr * (h @ w2.astype(F32) + b2.astype(F32)))
    w3r = w3.astype(F32).reshape(HY_FFN, HY_ORDER, HY_DIRS, HY_W)
    deltas = jnp.abs(jnp.linspace(math.log(HY_DECAY_TARGET) / HY_SLOW_PCT,
                                  math.log(HY_DECAY_TARGET) / HY_FAST_PCT, HY_W, dtype=F32))
    win = jnp.exp(-t * deltas)
    not_tap0 = (jnp.arange(L) > 0)[:, None]
    out = []
    for o in range(HY_ORDER):
        hf = (h[:L] @ w3r[:, o, 0]) * win[:L]
        hb = (h[L:] @ w3r[:, o, 1]) * win[L:]
        norm = jnp.sum(jnp.abs(hf), 0) + jnp.sum(jnp.abs(hb), 0)
        out.append(jnp.concatenate([hf, jnp.where(not_tap0, hb, 0.0)], 0) / (norm * (2 * L)))
    return out


FFT_R = 128
FFT_N = FFT_R * FFT_R
FFT_COLS = 2048


def _dft_parts(n):
    ang = 2.0 * np.pi * np.outer(np.arange(n), np.arange(n)) / n
    return np.cos(ang), np.sin(ang)


def hyena_fft_constants():
    C, S = _dft_parts(FFT_R)
    h = FFT_R // 2
    m_first = np.block([[C[:, :h], S[:, :h]], [-S[:, :h], C[:, :h]]])
    m_first_real = np.concatenate([C, -S], 0)
    m_mid = np.block([[C, S], [-S, C]])
    m_mid_inv = np.block([[C, -S], [S, C]])
    m_last = np.block([[C[:h], -S[:h]], [S[:h], C[:h]]])
    ang = 2.0 * np.pi * np.outer(np.arange(FFT_R), np.arange(FFT_R)) / FFT_N
    tw = np.stack([np.cos(ang), np.sin(ang)], -1)
    bf = lambda a: jnp.asarray(a, BF16)
    return dict(first=bf(m_first), first_real=bf(m_first_real), mid=bf(m_mid), mid_inv=bf(m_mid_inv),
                last=bf(m_last), tw=jnp.asarray(tw, F32))


def _fft_outer_kernel(m_ref, x_ref, o_ref):
    o_ref[...] = jnp.dot(m_ref[...], x_ref[...].astype(BF16), preferred_element_type=F32).astype(o_ref.dtype)


def fft_first(m, x2d):
    R, W = x2d.shape
    return pl.pallas_call(
        _fft_outer_kernel, grid=(W // FFT_COLS,),
        in_specs=[pl.BlockSpec(m.shape, lambda j: (0, 0)), pl.BlockSpec((R, FFT_COLS), lambda j: (0, j))],
        out_specs=pl.BlockSpec((m.shape[0], FFT_COLS), lambda j: (0, j)),
        out_shape=jax.ShapeDtypeStruct((m.shape[0], W), BF16),
        compiler_params=pltpu.CompilerParams(dimension_semantics=("arbitrary",)),
        name="hyena_fft_first")(m, x2d)


def _fft_mid_kernel(*refs, conv):
    if conv:
        a_ref, tw_ref, mf_ref, mi_ref, h_ref, o_ref = refs
    else:
        a_ref, tw_ref, mf_ref, o_ref = refs
    R = FFT_R
    tc = tw_ref[0, :, 0:1]
    ts = tw_ref[0, :, 1:2]
    ar = a_ref[0, 0].astype(F32)
    ai = a_ref[1, 0].astype(F32)
    x = jnp.concatenate([ar * tc + ai * ts, ai * tc - ar * ts], 0).astype(BF16)
    X = jnp.dot(mf_ref[...], x, preferred_element_type=F32)
    if not conv:
        o_ref[0, 0] = X[:R]
        o_ref[1, 0] = X[R:]
        return
    xr, xi = X[:R], X[R:]
    hr, hi = h_ref[0, 0], h_ref[1, 0]
    y = jnp.concatenate([xr * hr - xi * hi, xr * hi + xi * hr], 0).astype(BF16)
    Bm = jnp.dot(mi_ref[...], y, preferred_element_type=F32)
    br, bi = Bm[:R], Bm[R:]
    o_ref[0, 0] = (br * tc - bi * ts).astype(o_ref.dtype)
    o_ref[1, 0] = (bi * tc + br * ts).astype(o_ref.dtype)


def fft_mid(a, consts, h=None):
    C = a.shape[-1]
    slab = pl.BlockSpec((2, 1, FFT_R, C), lambda k: (0, k, 0, 0))
    twspec = pl.BlockSpec((1, FFT_R, 2), lambda k: (k, 0, 0))
    mspec = pl.BlockSpec((2 * FFT_R, 2 * FFT_R), lambda k: (0, 0))
    conv = h is not None
    ins = [a, consts['tw'], consts['mid']] + ([consts['mid_inv'], h] if conv else [])
    specs = [slab, twspec, mspec] + ([mspec, slab] if conv else [])
    return pl.pallas_call(
        functools.partial(_fft_mid_kernel, conv=conv), grid=(FFT_R,),
        in_specs=specs, out_specs=slab,
        out_shape=jax.ShapeDtypeStruct((2, FFT_R, FFT_R, C), BF16 if conv else F32),
        compiler_params=pltpu.CompilerParams(dimension_semantics=("arbitrary",)),
        name="hyena_fft_mid")(*ins)


def _fft_last_kernel(m_ref, b_ref, z_ref, bias_ref, gate_ref, o_ref):
    y = jnp.dot(m_ref[...], b_ref[...], preferred_element_type=F32)
    o_ref[...] = gate_ref[...] * (y + bias_ref[...] * z_ref[...])


def fft_last(m, b2d, z2d, bias_row, gate2d):
    R, W = z2d.shape
    col = lambda r: pl.BlockSpec((r, FFT_COLS), lambda j: (0, j))
    return pl.pallas_call(
        _fft_last_kernel, grid=(W // FFT_COLS,),
        in_specs=[pl.BlockSpec(m.shape, lambda j: (0, 0)), col(b2d.shape[0]), col(R),
                  pl.BlockSpec((1, FFT_COLS), lambda j: (0, 0)), col(R)],
        out_specs=col(R),
        out_shape=jax.ShapeDtypeStruct((R, W), F32),
        compiler_params=pltpu.CompilerParams(dimension_semantics=("arbitrary",)),
        name="hyena_fft_last")(m, b2d, z2d, bias_row, gate2d)


def filter_spectrum(k, consts):
    C = k.shape[1]
    a = fft_first(consts['first_real'], k.reshape(FFT_R, FFT_R * C))
    return fft_mid(a.reshape(2, FFT_R, FFT_R, C), consts)


def long_conv(z, gate, spec, bias, consts):
    B, L, C = z.shape
    assert B == 2 and 2 * L == FFT_N and FFT_COLS % C == 0
    z2d = z.reshape(FFT_R, FFT_R * C)
    a = fft_first(consts['first'], z2d)
    b = fft_mid(a.reshape(2, FFT_R, FFT_R, C), consts, spec)
    bias_row = jnp.tile(bias.reshape(1, C), (1, FFT_COLS // C))
    y = fft_last(consts['last'], b.reshape(2 * FFT_R, FFT_R * C), z2d, bias_row, gate.reshape(FFT_R, FFT_R * C))
    return y.reshape(B, L, C)


def dense_dft_constants(L):
    C, S = _dft_parts(2 * L)
    m_fwd = np.block([[C[:, :L], S[:, :L]], [-S[:, :L], C[:, :L]]])
    m_fwd_real = np.concatenate([C, -S], 0)
    m_inv = np.block([[C[:L], -S[:L]], [S[:L], C[:L]]])
    bf = lambda a: jnp.asarray(a, BF16)
    return bf(m_fwd), bf(m_fwd_real), bf(m_inv)


def _short_conv_kernel(mf_ref, mfr_ref, mi_ref, z_ref, k_ref, bias_ref, gate_ref, o_ref):
    n = k_ref.shape[0]
    z = z_ref[...]
    X = jnp.dot(mf_ref[...], z.astype(BF16), preferred_element_type=F32)
    Hs = jnp.dot(mfr_ref[...], k_ref[...].astype(BF16), preferred_element_type=F32)
    xr, xi, hr, hi = X[:n], X[n:], Hs[:n], Hs[n:]
    y = jnp.concatenate([xr * hr - xi * hi, xr * hi + xi * hr], 0).astype(BF16)
    o_ref[...] = gate_ref[...] * (jnp.dot(mi_ref[...], y, preferred_element_type=F32) + bias_ref[...] * z)


def short_conv(z, gate, k, bias):
    B, L, C = z.shape
    assert B == 2
    mf, mfr, mi = dense_dft_constants(L)
    y = pl.pallas_call(
        _short_conv_kernel,
        out_shape=jax.ShapeDtypeStruct((2 * L, C), F32),
        name="hyena_short_conv")(mf, mfr, mi, z.reshape(2 * L, C), k, bias.reshape(1, C), gate.reshape(2 * L, C))
    return y.reshape(B, L, C)


HALO = 8


def _hy_pre_kernel(x_ref, prev_ref, next_ref, w_ref, b_ref, v_ref, x1_ref, x2_ref, *, nt):
    t = pl.program_id(1)
    x = x_ref[0].astype(F32)
    row = lax.broadcasted_iota(jnp.int32, (ROW_TILE, 1), 0)
    prev_row = jnp.where(t > 0, prev_ref[0, HALO - 1:HALO].astype(F32), 0.0)
    next_row = jnp.where(t < nt - 1, next_ref[0, 0:1].astype(F32), 0.0)
    below = jnp.where(row == 0, prev_row, pltpu.roll(x, 1, 0))
    above = jnp.where(row == ROW_TILE - 1, next_row, pltpu.roll(x, ROW_TILE - 1, 0))
    y = below * w_ref[0:1] + x * w_ref[1:2] + above * w_ref[2:3] + b_ref[...]
    v_ref[0] = y[:, :HY_W]
    x1_ref[0] = y[:, HY_W:2 * HY_W]
    x2_ref[0] = y[:, 2 * HY_W:]


def hyena_pre(P, t0, nt, conv_w, conv_b):
    B, T, _ = P.shape
    cb = COL['hy'] // N_HY
    per = ROW_TILE // HALO
    last_blk = T // HALO - 1
    out = pl.BlockSpec((1, ROW_TILE, HY_W), lambda b, t: (b, t, 0))
    return pl.pallas_call(
        functools.partial(_hy_pre_kernel, nt=nt), grid=(B, nt),
        in_specs=[pl.BlockSpec((1, ROW_TILE, N_HY), lambda b, t: (b, t + t0, cb)),
                  pl.BlockSpec((1, HALO, N_HY), lambda b, t: (b, jnp.maximum((t + t0) * per - 1, 0), cb)),
                  pl.BlockSpec((1, HALO, N_HY), lambda b, t: (b, jnp.minimum((t + t0 + 1) * per, last_blk), cb)),
                  pl.BlockSpec((3, N_HY), lambda b, t: (0, 0)),
                  pl.BlockSpec((1, N_HY), lambda b, t: (0, 0))],
        out_specs=[out, out, out],
        out_shape=[jax.ShapeDtypeStruct((B, nt * ROW_TILE, HY_W), F32)] * 3,
        compiler_params=pltpu.CompilerParams(dimension_semantics=("arbitrary", "arbitrary")),
        name="hyena_pre")(P, P, P, conv_w, conv_b.reshape(1, N_HY))


def hyena_branch(P, with_ctx, n_ctx, conv_w, conv_b, w1, b1, w2, b2, w3, freq, hbias):
    B, T, _ = P.shape
    consts = hyena_fft_constants()
    hy_w = (w1, b1, w2, b2, w3, freq)
    nc = n_ctx // ROW_TILE
    v, x1, x2 = hyena_pre(P, nc, T // ROW_TILE - nc, conv_w, conv_b)
    k = hyena_kernels(T - n_ctx, *hy_w)
    z = long_conv(v, x1, filter_spectrum(k[0], consts), hbias[0], consts)
    o = long_conv(z, x2, filter_spectrum(k[1], consts), hbias[1], consts)
    if with_ctx:
        vc, xc1, xc2 = hyena_pre(P, 0, nc, conv_w, conv_b)
        kc = hyena_kernels(n_ctx, *hy_w)
        oc = short_conv(short_conv(vc, xc1, kc[0], hbias[0]), xc2, kc[1], hbias[1])
    else:
        oc = jnp.zeros((B, n_ctx, HY_W), F32)
    return o, oc


ROUTER_PAD = 128


def _merge_kernel(of_ref, ob_ref, r_ref, om_ref, ohl_ref, ohc_ref, g_ref, x_ref, mod_ref,
                  wg_ref, wm_ref, wh_ref, wo_ref, gn_ref, l1g_ref, l1b_ref, rw_ref, rb_ref,
                  x1_ref, h2_ref, lg_ref, *, n_ctx_tiles):
    o = of_ref[0] + ob_ref[0]
    parts = [_rms(o[:, h * GLA_DV:(h + 1) * GLA_DV], gn_ref[...]) for h in range(GLA_HEADS)]
    og = (jnp.concatenate(parts, axis=1) * jax.nn.silu(r_ref[0].astype(F32))).astype(BF16)
    oh = jnp.where(pl.program_id(1) < n_ctx_tiles, ohc_ref[0], ohl_ref[0]).astype(BF16)
    D = D_MODEL
    g = g_ref[0]
    y = (jax.nn.sigmoid(g[:, :D].astype(F32)) * jnp.dot(og, wg_ref[...], preferred_element_type=F32)
         + jax.nn.sigmoid(g[:, D:2 * D].astype(F32)) * jnp.dot(om_ref[0], wm_ref[...], preferred_element_type=F32)
         + jax.nn.sigmoid(g[:, 2 * D:].astype(F32)) * jnp.dot(oh, wh_ref[...], preferred_element_type=F32))
    ym = jnp.dot(y.astype(BF16), wo_ref[...], preferred_element_type=F32)
    gate1, sh2, sc2 = mod_ref[0, 0, 0:1], mod_ref[0, 0, 1:2], mod_ref[0, 0, 2:3]
    x1 = _layer_norm(DEEPNORM_ALPHA * x_ref[0] + gate1 * ym) * l1g_ref[...] + l1b_ref[...]
    x1_ref[0] = x1
    h2f = _layer_norm(x1) * (1.0 + sc2) + sh2
    h2_ref[0] = pack_bf16_pairs(h2f)
    h2 = h2f.astype(BF16)
    lg_ref[0] = jnp.dot(h2, rw_ref[...], preferred_element_type=F32) + rb_ref[...]


def merge_block(o_f, o_b, P, o_mla, o_hy_lat, o_hy_ctx, xa, mod, w_br_gla, w_br_mla, w_br_hy, w_out, gla_norm,
                ln1_g, ln1_b, router_w, router_b, n_ctx):
    B, T, D = xa.shape
    NC = n_ctx // ROW_TILE
    bf = lambda a: a.astype(BF16)
    row = lambda a: a.reshape(1, -1)
    rw = jnp.pad(router_w, ((0, 0), (0, ROUTER_PAD - N_EXPERTS))).astype(BF16)
    rb = jnp.pad(router_b, (0, ROUTER_PAD - N_EXPERTS)).reshape(1, -1)
    full = lambda a: pl.BlockSpec(a.shape, lambda b, t: (0,) * a.ndim)
    tile = lambda w: pl.BlockSpec((1, ROW_TILE, w), lambda b, t: (b, t, 0))
    lat = pl.BlockSpec((1, ROW_TILE, BRANCH_W), lambda b, t: (b, jnp.maximum(t - NC, 0), 0))
    ctx = pl.BlockSpec((1, ROW_TILE, BRANCH_W), lambda b, t: (b, jnp.minimum(t, NC - 1), 0))
    ws = [bf(w_br_gla), bf(w_br_mla), bf(w_br_hy), bf(w_out), row(gla_norm), row(ln1_g), row(ln1_b), rw, rb]
    return pl.pallas_call(
        functools.partial(_merge_kernel, n_ctx_tiles=NC), grid=(B, T // ROW_TILE),
        in_specs=[tile(BRANCH_W), tile(BRANCH_W), pcol('r'), tile(BRANCH_W), lat, ctx, pcol('gate'), tile(D),
                  pl.BlockSpec((1, 1, 3, D), _tile_kind(NC))] + [full(w) for w in ws],
        out_specs=[tile(D), tile(HALF_D), tile(ROUTER_PAD)],
        out_shape=[jax.ShapeDtypeStruct((B, T, D), F32), jax.ShapeDtypeStruct((B, T, HALF_D), U32),
                   jax.ShapeDtypeStruct((B, T, ROUTER_PAD), F32)],
        compiler_params=pltpu.CompilerParams(dimension_semantics=("arbitrary", "arbitrary"),
                                             vmem_limit_bytes=48 * 1024 * 1024),
        name="merge_block")(o_f, o_b, P, o_mla, o_hy_lat, o_hy_ctx, P, xa, mod, *ws)


def kernel(x, c, ctx, c_ctx, ada_w, ada_b, w_in, gla_wa2_f, gla_ba_f, gla_wa2_b, gla_ba_b, gla_norm,
           mla_q_norm, mla_w_uq, mla_kv_norm, mla_w_ukv, hy_conv_w, hy_conv_b, hy_w1, hy_b1, hy_w2, hy_b2,
           hy_w3, hy_freq, hy_bias, w_br_gla, w_br_mla, w_br_hy, w_out, ln1_g, ln1_b, ln2_g, ln2_b,
           router_w, router_b, moe_w1, moe_b1, moe_w2, moe_b2):
    B, L, D = x.shape
    CL = ctx.shape[1]
    T = CL + L
    tables = rope_tables(L, CL)
    xa = jnp.concatenate([ctx, x], axis=1)
    cond = jnp.concatenate([jax.nn.silu(c), jax.nn.silu(c_ctx)[None], jnp.zeros((8 - B - 1, D), F32)], 0)
    for l in range(DEPTH):
        last = l == DEPTH - 1
        ada = pmm(cond, ada_w[l])[:B + 1] + ada_b[l]
        ada = jnp.stack([ada[:B], jnp.broadcast_to(ada[B:], (B, 6 * D))], 1).reshape(B, 2, 6, D)
        P = ln_mod_proj(xa, ada[:, :, 0:2], permute_w_in(w_in[l]), CL)
        o_f, o_b = gla_scan(P, gla_wa2_f[l], gla_ba_f[l], gla_wa2_b[l], gla_ba_b[l], CL)
        q, k, v = mla_prep(P, tables, mla_weights(mla_w_uq[l], mla_w_ukv[l]), mla_q_norm[l], mla_kv_norm[l])
        o_mla = mla_attention(q, k, v, CL)
        o_hy, o_hy_ctx = hyena_branch(P, not last, CL, hy_conv_w[l], hy_conv_b[l], hy_w1[l], hy_b1[l], hy_w2[l],
                                      hy_b2[l], hy_w3[l], hy_freq[l], hy_bias[l])
        x1, h2, logits = merge_block(o_f, o_b, P, o_mla, o_hy, o_hy_ctx, xa, ada[:, :, 2:5],
                                     w_br_gla[l], w_br_mla[l], w_br_hy[l], w_out[l], gla_norm[l],
                                     ln1_g[l], ln1_b[l], router_w[l], router_b[l], CL)
        t0 = CL // ROW_TILE if last else 0
        nt = T // ROW_TILE - t0
        xa = moe_ffn_residual(h2, logits[:, t0 * ROW_TILE:, :N_EXPERTS], x1, ada[:, :, 5:6],
                              moe_w1, moe_b1, moe_w2, moe_b2, l, ln2_g[l], ln2_b[l],
                              t0, nt, CL // ROW_TILE)
    return xa
```

```python
import functools
import math
import jax
import jax.numpy as jnp
from jax import lax
import numpy as np
from jax.experimental import pallas as pl
from jax.experimental.pallas import tpu as pltpu

D_MODEL = 1024
DEPTH = 2
GRID_W = 64
BRANCH_W = D_MODEL // 2
N_BRANCH = 3
GLA_HEADS = 4
GLA_DV = BRANCH_W // GLA_HEADS
GLA_DK = GLA_DV // 2
GLA_RANK = 16
GLA_TAU = 16.0
GLA_CHUNK = 16
MLA_HEADS = 8
MLA_DV = BRANCH_W // MLA_HEADS
MLA_NOPE = MLA_DV
MLA_ROPE = MLA_NOPE // 2
MLA_Q_RANK = 3 * D_MODEL // 8
MLA_KV_RANK = D_MODEL // 4
ROPE_BASE = 10000.0
Q_BLOCK = 128
HY_W = BRANCH_W
HY_ORDER = 2
HY_DIRS = 2
HY_BANDS = 16
HY_EMB = 2 * HY_BANDS + 1
HY_FFN = 64
HY_DECAY_TARGET = 1e-2
HY_FAST_PCT = 0.3
HY_SLOW_PCT = 1.5
N_EXPERTS = 32
TOP_K = 4
D_FF = D_MODEL
SWIGLU_LIMIT = 7.0
SWIGLU_ALPHA = 1.702
MOE_BLOCK = 128
LN_EPS = 1e-5
RMS_EPS = 1e-6
DEEPNORM_ALPHA = (2 * DEPTH) ** 0.25
IN_SPLITS = (
    GLA_HEADS * GLA_DK, GLA_HEADS * GLA_DV, GLA_RANK, GLA_RANK, MLA_KV_RANK, MLA_ROPE,
    GLA_HEADS * GLA_DK, GLA_HEADS * GLA_DV, MLA_Q_RANK, 3 * HY_W, N_BRANCH * D_MODEL,
)
N_KEY_GROUPS = 6
KEY_COLS = sum(IN_SPLITS[:N_KEY_GROUPS])
IN_OFFSETS = tuple(int(o) for o in np.cumsum(IN_SPLITS)[:-1])
IN_TOTAL = sum(IN_SPLITS)
F32 = jnp.float32
BF16 = jnp.bfloat16
U32 = jnp.uint32


def _mm_kernel(a_ref, w_ref, o_ref):
    o_ref[...] = jnp.dot(a_ref[...], w_ref[...], preferred_element_type=F32)


def _pick(n, cands):
    for c in cands:
        if n % c == 0:
            return c
    return n


def pmm(a, w):
    lead = a.shape[:-1]
    K = a.shape[-1]
    N = w.shape[-1]
    a2 = a.reshape(-1, K).astype(BF16)
    M = a2.shape[0]
    N0 = N
    if N % 128:
        N = -(-N // 256) * 256
        w = jnp.pad(w, ((0, 0), (0, N - N0)))
    tm = _pick(M, (512, 256, 128, 64, 32, 16, 8))
    tn = _pick(N, (512, 256, 128))
    out = pl.pallas_call(
        _mm_kernel,
        grid=(M // tm, N // tn),
        in_specs=[pl.BlockSpec((tm, K), lambda i, j: (i, 0)),
                  pl.BlockSpec((K, tn), lambda i, j: (0, j))],
        out_specs=pl.BlockSpec((tm, tn), lambda i, j: (i, j)),
        out_shape=jax.ShapeDtypeStruct((M, N), F32),
    )(a2, w.astype(BF16))
    return out[:, :N0].reshape(lead + (N0,))


ROW_TILE = 256
HEAD_PAD = 128
N_GATE = N_BRANCH * D_MODEL
N_HY = 3 * HY_W
COL = dict(gate=0, hy=N_GATE, qa=N_GATE + N_HY)
COL['misc'] = COL['qa'] + MLA_Q_RANK
COL['v'] = COL['misc'] + HEAD_PAD
COL['r'] = COL['v'] + GLA_HEADS * GLA_DV
COL['k'] = COL['r'] + GLA_HEADS * GLA_DV
COL['q'] = COL['k'] + GLA_HEADS * GLA_DK
COL['kva'] = COL['q'] + GLA_HEADS * GLA_DK
P_COLS = COL['kva'] + MLA_KV_RANK
WIDTH = dict(gate=N_GATE, hy=N_HY, qa=MLA_Q_RANK, misc=HEAD_PAD, v=GLA_HEADS * GLA_DV, r=GLA_HEADS * GLA_DV,
             k=GLA_HEADS * GLA_DK, q=GLA_HEADS * GLA_DK, kva=MLA_KV_RANK)
MISC_AF, MISC_AB, MISC_KR = 0, GLA_RANK, MLA_NOPE
P_CHUNK = 768
assert all(COL[g] % WIDTH[g] == 0 for g in COL) and P_COLS % P_CHUNK == 0


def permute_w_in(w_in):
    o = (0,) + IN_OFFSETS + (IN_TOTAL,)
    k, v, af, ab, kva, kr, q, r, qa, hy, gate = [w_in[:, o[i]:o[i + 1]] for i in range(len(IN_SPLITS))]
    z = lambda n: jnp.zeros((w_in.shape[0], n), w_in.dtype)
    misc = jnp.concatenate([af, ab, z(MISC_KR - 2 * GLA_RANK), kr, z(HEAD_PAD - MISC_KR - MLA_ROPE)], 1)
    return jnp.concatenate([gate, hy, qa, misc, v, r, k, q, kva], 1).astype(BF16)


def _layer_norm(x):
    xc = x - jnp.mean(x, axis=-1, keepdims=True)
    return xc * lax.rsqrt(jnp.mean(xc * xc, axis=-1, keepdims=True) + LN_EPS)


def _rms(x, gain):
    return x * lax.rsqrt(jnp.mean(x * x, axis=-1, keepdims=True) + RMS_EPS) * gain


def _proj_kernel(x_ref, mod_ref, w_ref, o_ref):
    h = _layer_norm(x_ref[0]) * (1.0 + mod_ref[0, 0, 1:2]) + mod_ref[0, 0, 0:1]
    hb = h.astype(BF16)
    for c in range(0, P_COLS, P_CHUNK):
        o_ref[0, :, c:c + P_CHUNK] = jnp.dot(hb, w_ref[:, c:c + P_CHUNK], preferred_element_type=F32).astype(BF16)


def _tile_kind(n_ctx_tiles):
    return lambda b, t: (b, jnp.where(t < n_ctx_tiles, 1, 0), 0, 0)


def ln_mod_proj(xa, mod, w_p, n_ctx):
    B, T, D = xa.shape
    return pl.pallas_call(
        _proj_kernel, grid=(B, T // ROW_TILE),
        in_specs=[pl.BlockSpec((1, ROW_TILE, D), lambda b, t: (b, t, 0)),
                  pl.BlockSpec((1, 1, 2, D), _tile_kind(n_ctx // ROW_TILE)),
                  pl.BlockSpec((D, P_COLS), lambda b, t: (0, 0), pipeline_mode=pl.Buffered(1))],
        out_specs=pl.BlockSpec((1, ROW_TILE, P_COLS), lambda b, t: (b, t, 0)),
        out_shape=jax.ShapeDtypeStruct((B, T, P_COLS), BF16),
        compiler_params=pltpu.CompilerParams(dimension_semantics=("arbitrary", "arbitrary"),
                                             vmem_limit_bytes=48 * 1024 * 1024),
        name="ln_mod_proj")(xa, mod, w_p)


def pcol(group, row_map=lambda b, t: t):
    w = WIDTH[group]
    return pl.BlockSpec((1, ROW_TILE, w), lambda b, t, _i=COL[group] // w: (b, row_map(b, t), _i))


ONES_LANE_EVEN = MLA_DV
ONES_LANE_ODD = 0
LOG2E = 1.4426950408889634


def rope_tables(L, n_ctx):
    rows = L // GRID_W
    row = jnp.repeat(jnp.arange(rows, dtype=F32), GRID_W)
    col = jnp.tile(jnp.arange(GRID_W, dtype=F32), rows)
    n_freq = MLA_ROPE // 4
    inv = ROPE_BASE ** (-jnp.arange(n_freq, dtype=F32) / n_freq)
    ang = jnp.concatenate([row[:, None] * inv, col[:, None] * inv], -1)
    ang = jnp.concatenate([jnp.zeros((n_ctx, ang.shape[1]), F32), ang], 0)
    cos, sin = jnp.cos(ang), jnp.sin(ang)
    T = ang.shape[0]
    half = MLA_ROPE // 2
    z = lambda n: jnp.zeros((T, n), F32)
    ct = jnp.concatenate([jnp.ones((T, MLA_NOPE), F32), cos, cos, z(HEAD_PAD - MLA_NOPE - MLA_ROPE)], 1)
    sa = jnp.concatenate([z(MLA_NOPE), -sin, z(HEAD_PAD - MLA_NOPE - half)], 1)
    sb = jnp.concatenate([z(MLA_NOPE + half), sin, z(HEAD_PAD - MLA_NOPE - MLA_ROPE)], 1)
    return ct, sa, sb


def mla_weights(w_uq, w_ukv):
    H = MLA_HEADS
    wq = w_uq.reshape(MLA_Q_RANK, H, MLA_NOPE + MLA_ROPE)
    wq = jnp.pad(wq, ((0, 0), (0, 0), (0, HEAD_PAD - MLA_NOPE - MLA_ROPE))).reshape(MLA_Q_RANK, H * HEAD_PAD)
    wkv = w_ukv.reshape(MLA_KV_RANK, H, MLA_NOPE + MLA_DV)
    wk = jnp.pad(wkv[:, :, :MLA_NOPE], ((0, 0), (0, 0), (0, HEAD_PAD - MLA_NOPE))).reshape(MLA_KV_RANK, H * HEAD_PAD)
    wv = wkv[:, :, MLA_NOPE:].reshape(MLA_KV_RANK, H // 2, 2, MLA_DV)
    z = jnp.zeros_like(wv[:, :, 0])
    wv = jnp.stack([jnp.concatenate([wv[:, :, 0], z], -1), jnp.concatenate([z, wv[:, :, 1]], -1)], 2)
    ones = np.zeros((H // 2, 2, HEAD_PAD), np.float32)
    ones[:, 0, ONES_LANE_EVEN] = 1.0
    ones[:, 1, ONES_LANE_ODD] = 1.0
    return (wq.astype(BF16), wk.astype(BF16), wv.reshape(MLA_KV_RANK, H * HEAD_PAD).astype(BF16),
            jnp.asarray(ones.reshape(1, H * HEAD_PAD)))


def _mla_prep_kernel(qa_ref, kva_ref, misc_ref, ct_ref, sa_ref, sb_ref, wq_ref, wk_ref, wv_ref, ones_ref,
                     qn_ref, kn_ref, q_ref, k_ref, v_ref):
    ct, sa, sb = ct_ref[...], sa_ref[...], sb_ref[...]
    half = MLA_ROPE // 2
    rope = lambda x: x * ct + pltpu.roll(x, HEAD_PAD - half, 1) * sa + pltpu.roll(x, half, 1) * sb
    qn = _rms(qa_ref[0].astype(F32), qn_ref[...]).astype(BF16)
    q = jnp.dot(qn, wq_ref[...], preferred_element_type=F32)
    kn = _rms(kva_ref[0].astype(F32), kn_ref[...]).astype(BF16)
    k = jnp.dot(kn, wk_ref[...], preferred_element_type=F32)
    v_ref[0] = (jnp.dot(kn, wv_ref[...], preferred_element_type=F32) + ones_ref[...]).astype(BF16)
    lane = lax.broadcasted_iota(jnp.int32, (1, HEAD_PAD), 1)
    is_kr = (lane >= MISC_KR) & (lane < MISC_KR + MLA_ROPE)
    kr = rope(jnp.where(is_kr, misc_ref[0].astype(F32), 0.0))
    qscale = (MLA_NOPE + MLA_ROPE) ** -0.5 * LOG2E
    for h in range(MLA_HEADS):
        hs = slice(h * HEAD_PAD, (h + 1) * HEAD_PAD)
        q_ref[0, :, hs] = (rope(q[:, hs]) * qscale).astype(BF16)
        k_ref[0, :, hs] = (k[:, hs] + kr).astype(BF16)


def mla_prep(P, tables, weights, q_norm, kv_norm):
    B, T, _ = P.shape
    ct, sa, sb = tables
    wq, wk, wv, ones = weights
    full = lambda a: pl.BlockSpec(a.shape, lambda b, t: (0,) * a.ndim)
    tab = pl.BlockSpec((ROW_TILE, HEAD_PAD), lambda b, t: (t, 0))
    qn, kn = q_norm.reshape(1, -1), kv_norm.reshape(1, -1)
    W = MLA_HEADS * HEAD_PAD
    out = pl.BlockSpec((1, ROW_TILE, W), lambda b, t: (b, t, 0))
    return pl.pallas_call(
        _mla_prep_kernel, grid=(B, T // ROW_TILE),
        in_specs=[pcol('qa'), pcol('kva'), pcol('misc'), tab, tab, tab, full(wq), full(wk), full(wv), full(ones),
                  full(qn), full(kn)],
        out_specs=[out, out, out],
        out_shape=[jax.ShapeDtypeStruct((B, T, W), BF16)] * 3,
        compiler_params=pltpu.CompilerParams(dimension_semantics=("arbitrary", "arbitrary")),
        name="mla_prep")(P, P, P, ct, sa, sb, wq, wk, wv, ones, qn, kn)


ATT_TQ = 256
ATT_TK = 768


def _attn_kernel(q_ref, k_ref, v_ref, o_ref, s_buf, p_buf, m_buf, a_buf, mrun, acc, *, n_ctx, tk):
    tq = q_ref.shape[1]
    T = k_ref.shape[1]
    in_ctx = pl.program_id(2) * tq < n_ctx
    lane = lax.broadcasted_iota(jnp.int32, (tq, HEAD_PAD), 1)
    hs = lambda h: slice(h * HEAD_PAD, (h + 1) * HEAD_PAD)

    def stage_a(c, slot, size):
        start = c * size
        for h in range(2):
            k = k_ref[0, pl.ds(start, size), hs(h)]
            s = lax.dot_general(q_ref[0, :, hs(h)], k, (((1,), (1,)), ((), ())), preferred_element_type=F32)
            s_buf[slot, h, :, :size] = s
            m_prev = mrun[h]
            m_new = jnp.maximum(m_prev, jnp.max(s, axis=-1, keepdims=True))
            mrun[h] = m_new
            m_buf[slot, h] = m_new
            a_buf[slot, h] = jnp.exp2(m_prev - m_new)

    def stage_b(slot, size):
        for h in range(2):
            p_buf[slot, h, :, :size] = jnp.exp2(s_buf[slot, h, :, :size] - m_buf[slot, h]).astype(BF16)

    def stage_c(c, slot, size):
        start = c * size
        for h in range(2):
            v = v_ref[0, pl.ds(start, size), hs(h)]
            acc[h] = a_buf[slot, h] * acc[h] + jnp.dot(p_buf[slot, h, :, :size], v, preferred_element_type=F32)

    def run(n, size):
        for h in range(2):
            mrun[h] = jnp.full((tq, 1), -jnp.inf, F32)
            acc[h] = jnp.zeros((tq, HEAD_PAD), F32)
        if n == 1:
            stage_a(0, 0, size)
            stage_b(0, size)
            stage_c(0, 0, size)
        else:
            stage_a(0, 0, size)
            stage_b(0, size)
            stage_a(1, 1, size)

            for c in range(n - 2):
                stage_c(c, c % 2, size)
                stage_b(1 - c % 2, size)
                stage_a(c + 2, c % 2, size)
            stage_c(n - 2, (n - 2) % 2, size)
            stage_b((n - 1) % 2, size)
            stage_c(n - 1, (n - 1) % 2, size)
        a0, a1 = acc[0], acc[1]
        l0 = a0[:, ONES_LANE_EVEN:ONES_LANE_EVEN + 1]
        l1 = a1[:, ONES_LANE_ODD:ONES_LANE_ODD + 1]
        o_ref[0] = jnp.where(lane < MLA_DV, a0 / l0, a1 / l1).astype(o_ref.dtype)

    @pl.when(in_ctx)
    def _():
        run(1, n_ctx)

    @pl.when(jnp.logical_not(in_ctx))
    def _():
        run(T // tk, tk)


def mla_attention(q, k, v, n_ctx):
    B, T, _ = q.shape
    assert T % ATT_TQ == 0 and T % ATT_TK == 0 and n_ctx % ATT_TQ == 0 and n_ctx <= ATT_TK
    kern = functools.partial(_attn_kernel, n_ctx=n_ctx, tk=ATT_TK)
    return pl.pallas_call(
        kern,
        grid=(B, MLA_HEADS // 2, T // ATT_TQ),
        in_specs=[pl.BlockSpec((1, ATT_TQ, 2 * HEAD_PAD), lambda b, h, i: (b, i, h)),
                  pl.BlockSpec((1, T, 2 * HEAD_PAD), lambda b, h, i: (b, 0, h)),
                  pl.BlockSpec((1, T, 2 * HEAD_PAD), lambda b, h, i: (b, 0, h))],
        out_specs=pl.BlockSpec((1, ATT_TQ, 2 * MLA_DV), lambda b, h, i: (b, i, h)),
        out_shape=jax.ShapeDtypeStruct((B, T, MLA_HEADS * MLA_DV), BF16),
        scratch_shapes=[pltpu.VMEM((2, 2, ATT_TQ, ATT_TK), F32), pltpu.VMEM((2, 2, ATT_TQ, ATT_TK), BF16),
                        pltpu.VMEM((2, 2, ATT_TQ, 1), F32), pltpu.VMEM((2, 2, ATT_TQ, 1), F32),
                        pltpu.VMEM((2, ATT_TQ, 1), F32), pltpu.VMEM((2, ATT_TQ, HEAD_PAD), F32)],
        compiler_params=pltpu.CompilerParams(
            dimension_semantics=("arbitrary", "arbitrary", "arbitrary"),
            vmem_limit_bytes=56 * 1024 * 1024),
        name="mla_attention",
    )(q, k, v)


MOE_TM = 256
HALF_D = D_MODEL // 2
HI_MASK = 0xFFFF0000
N_ROW_DMA = ROW_TILE * TOP_K


def pack_bf16_pairs(x):
    u = pltpu.bitcast(x.astype(BF16).astype(F32), U32)
    return (u[:, :HALF_D] >> 16) | (u[:, HALF_D:] & jnp.uint32(HI_MASK))


def unpack_bf16_pairs(w):
    lo = pltpu.bitcast(w << 16, F32)
    hi = pltpu.bitcast(w & jnp.uint32(HI_MASK), F32)
    return jnp.concatenate([lo, hi], axis=1).astype(BF16)


def moe_route(logits):
    N = logits.shape[0]
    top_val, top_idx = lax.top_k(logits, TOP_K)
    gates = jax.nn.softmax(top_val, axis=-1)
    NK = N * TOP_K
    flat_e = top_idx.reshape(NK)
    onehot = (flat_e[:, None] == jnp.arange(N_EXPERTS, dtype=flat_e.dtype)[None, :]).astype(F32)
    nb = NK // MOE_TM
    within = jnp.einsum('ij,bje->bie', jnp.tril(jnp.ones((MOE_TM, MOE_TM), F32)),
                        onehot.reshape(nb, MOE_TM, N_EXPERTS))
    btot = within[:, -1, :].astype(jnp.int32)
    boff = jnp.cumsum(btot, axis=0) - btot
    csum = (within.astype(jnp.int32) + boff[:, None, :]).reshape(NK, N_EXPERTS)
    rank = jnp.take_along_axis(csum, flat_e[:, None], axis=1)[:, 0] - 1
    counts = csum[-1]
    padded = (counts + MOE_TM - 1) // MOE_TM * MOE_TM
    pad_end = jnp.cumsum(padded)
    dest = ((pad_end - padded)[flat_e] + rank).astype(jnp.int32)
    n_blocks = NK // MOE_TM + N_EXPERTS
    blk_start = jnp.arange(n_blocks, dtype=jnp.int32) * MOE_TM
    block_e = jnp.minimum(jnp.sum(blk_start[:, None] >= pad_end[None, :], axis=1), N_EXPERTS - 1).astype(jnp.int32)
    first = jnp.concatenate([jnp.ones((1,), jnp.int32), (block_e[1:] != block_e[:-1]).astype(jnp.int32)])
    n_used = (pad_end[-1:] // MOE_TM).astype(jnp.int32)
    return dest, gates, block_e, first, n_used


def _dispatch_kernel(dest_ref, h_ref, xb_in_ref, xb_ref, sem):
    del xb_in_ref

    def start(r, c):
        for k in range(TOP_K):
            pltpu.make_async_copy(h_ref.at[0, pl.ds(r, 1)], xb_ref.at[pl.ds(dest_ref[r * TOP_K + k], 1)],
                                  sem).start()
        return c

    lax.fori_loop(0, ROW_TILE, start, 0, unroll=4)
    for k in range(TOP_K):
        pltpu.make_async_copy(h_ref.at[0], xb_ref.at[pl.ds(0, ROW_TILE)], sem).wait()


def moe_dispatch(h2p, dest, n_rows, t0, nt):
    B = h2p.shape[0]
    xb0 = jnp.zeros((n_rows, HALF_D), U32)
    return pl.pallas_call(
        _dispatch_kernel, grid=(B, nt),
        in_specs=[pl.BlockSpec((N_ROW_DMA,), lambda b, t: (b * nt + t,), memory_space=pltpu.SMEM),
                  pl.BlockSpec((1, ROW_TILE, HALF_D), lambda b, t: (b, t + t0, 0)),
                  pl.BlockSpec(memory_space=pl.ANY)],
        out_specs=pl.BlockSpec(memory_space=pl.ANY),
        out_shape=jax.ShapeDtypeStruct((n_rows, HALF_D), U32),
        scratch_shapes=[pltpu.SemaphoreType.DMA],
        input_output_aliases={2: 0},
        compiler_params=pltpu.CompilerParams(dimension_semantics=("arbitrary", "arbitrary")),
        name="moe_dispatch")(dest, h2p, xb0)


def _moe_kernel(be_ref, first_ref, nb_ref, x_ref, w1_ref, b1_ref, w2_ref, b2_ref, o_ref, w1_bf, w2_bf):
    i = pl.program_id(0)

    @pl.when(jnp.logical_and(i < nb_ref[0], first_ref[i] == 1))
    def _():
        w1_bf[...] = w1_ref[0].astype(BF16)
        w2_bf[...] = w2_ref[0].astype(BF16)

    @pl.when(i < nb_ref[0])
    def _():
        x = unpack_bf16_pairs(x_ref[...])
        hcat = jnp.dot(x, w1_bf[...], preferred_element_type=F32) + b1_ref[0]
        glu = jnp.minimum(hcat[:, :D_FF], SWIGLU_LIMIT)
        lin = jnp.clip(hcat[:, D_FF:], -SWIGLU_LIMIT, SWIGLU_LIMIT)
        act = glu * jax.nn.sigmoid(SWIGLU_ALPHA * glu) * (lin + 1)
        o_ref[...] = jnp.dot(act.astype(BF16), w2_bf[...], preferred_element_type=F32) + b2_ref[0]

    @pl.when(i >= nb_ref[0])
    def _():
        o_ref[...] = jnp.zeros_like(o_ref)


def moe_experts(xb, block_e, first, n_used, w1, b1, w2, b2, layer):
    R = xb.shape[0]
    D = D_MODEL
    n_blocks = R // MOE_TM
    grid_spec = pltpu.PrefetchScalarGridSpec(
        num_scalar_prefetch=3,
        grid=(n_blocks,),
        in_specs=[pl.BlockSpec((MOE_TM, HALF_D), lambda i, be, fi, nb: (i, 0)),
                  pl.BlockSpec((None, 1, D, 2 * D_FF), lambda i, be, fi, nb: (layer, be[i], 0, 0)),
                  pl.BlockSpec((None, 1, 1, 2 * D_FF), lambda i, be, fi, nb: (layer, be[i], 0, 0)),
                  pl.BlockSpec((None, 1, D_FF, D), lambda i, be, fi, nb: (layer, be[i], 0, 0)),
                  pl.BlockSpec((None, 1, 1, D), lambda i, be, fi, nb: (layer, be[i], 0, 0))],
        out_specs=pl.BlockSpec((MOE_TM, D), lambda i, be, fi, nb: (i, 0)),
        scratch_shapes=[pltpu.VMEM((D, 2 * D_FF), BF16), pltpu.VMEM((D_FF, D), BF16)],
    )
    return pl.pallas_call(
        _moe_kernel,
        grid_spec=grid_spec,
        out_shape=jax.ShapeDtypeStruct((R, D), F32),
        compiler_params=pltpu.CompilerParams(
            dimension_semantics=("arbitrary",), vmem_limit_bytes=56 * 1024 * 1024),
        name="moe_experts",
    )(block_e, first, n_used, xb, w1, b1[:, :, None, :], w2, b2[:, :, None, :])


def _combine_kernel(dest_ref, x_ref, gate_ref, mod_ref, lg_ref, lb_ref, yb_ref, o_ref, buf, sem):
    def start(r, c):
        for k in range(TOP_K):
            pltpu.make_async_copy(yb_ref.at[pl.ds(dest_ref[r * TOP_K + k], 1)], buf.at[k, pl.ds(r, 1)], sem).start()
        return c

    lax.fori_loop(0, ROW_TILE, start, 0, unroll=4)
    for k in range(TOP_K):
        pltpu.make_async_copy(yb_ref.at[pl.ds(0, ROW_TILE)], buf.at[k], sem).wait()
    g = gate_ref[...]
    f = g[:, 0:1] * buf[0]
    for k in range(1, TOP_K):
        f = f + g[:, k:k + 1] * buf[k]
    o_ref[0] = _layer_norm(DEEPNORM_ALPHA * x_ref[0] + mod_ref[0, 0, 0:1] * f) * lg_ref[...] + lb_ref[...]


def moe_combine(yb, dest, gates, x1, mod, ln_g, ln_b, t0, nt, n_ctx_tiles):
    B, T, D = x1.shape
    row = lambda a: a.reshape(1, -1)
    full = lambda a: pl.BlockSpec(a.shape, lambda b, t: (0,) * a.ndim)
    return pl.pallas_call(
        _combine_kernel, grid=(B, nt),
        in_specs=[pl.BlockSpec((N_ROW_DMA,), lambda b, t: (b * nt + t,), memory_space=pltpu.SMEM),
                  pl.BlockSpec((1, ROW_TILE, D), lambda b, t: (b, t + t0, 0)),
                  pl.BlockSpec((ROW_TILE, TOP_K), lambda b, t: (b * nt + t, 0)),
                  pl.BlockSpec((1, 1, 1, D), lambda b, t: (b, jnp.where(t + t0 < n_ctx_tiles, 1, 0), 0, 0)),
                  full(row(ln_g)), full(row(ln_b)),
                  pl.BlockSpec(memory_space=pl.ANY)],
        out_specs=pl.BlockSpec((1, ROW_TILE, D), lambda b, t: (b, t, 0)),
        out_shape=jax.ShapeDtypeStruct((B, nt * ROW_TILE, D), F32),
        scratch_shapes=[pltpu.VMEM((TOP_K, ROW_TILE, D), F32), pltpu.SemaphoreType.DMA],
        compiler_params=pltpu.CompilerParams(dimension_semantics=("arbitrary", "arbitrary")),
        name="moe_combine")(dest, x1, gates, mod, row(ln_g), row(ln_b), yb)


def moe_ffn_residual(h2p, logits, x1, gate2, w1, b1, w2, b2, layer, ln_g, ln_b, t0, nt, n_ctx_tiles):
    B = h2p.shape[0]
    N = B * nt * ROW_TILE
    dest, gates, block_e, first, n_used = moe_route(logits.reshape(N, N_EXPERTS))
    n_rows = (N * TOP_K // MOE_TM + N_EXPERTS) * MOE_TM
    xb = moe_dispatch(h2p, dest, n_rows, t0, nt)
    yb = moe_experts(xb, block_e, first, n_used, w1, b1, w2, b2, layer)
    return moe_combine(yb, dest, gates, x1, gate2, ln_g, ln_b, t0, nt, n_ctx_tiles)


GLA_TILE = 256
GLA_BASE = 16
GLA_LEVELS = (16, 32, 64, 128)
N_SEL = 2 * (len(GLA_LEVELS) + 1)


def gla_constants():
    n = GLA_TILE
    i = np.arange(n)[:, None]
    t = np.arange(n)[None, :]
    sizes = GLA_LEVELS + (n,)
    P = [((t <= i) & (t // s == i // s)) for s in sizes]
    S = [((t > i) & (t // s == i // s)) for s in sizes]
    sel_f = np.concatenate(P + S, 0)
    sel_b = np.concatenate([p.T for p in P] + [s.T for s in S], 0)
    lv = [((i // (2 * s) == t // (2 * s)) & ((i // s) % 2 == 1) & ((t // s) % 2 == 0)) for s in GLA_LEVELS]
    dg = (i // GLA_BASE == t // GLA_BASE) & (t <= i)
    msk_f = np.stack(lv + [dg], 0)
    msk_b = np.stack([m.T for m in lv] + [dg.T], 0)
    return (jnp.asarray(np.stack([sel_f, sel_b], 0), BF16), jnp.asarray(np.stack([msk_f, msk_b], 0), F32))


def _gla_kernel(qf_ref, kf_ref, vf_ref, af_ref, qb_ref, kb_ref, vb_ref, ab_ref,
                wf_ref, bf_ref, wb_ref, bb_ref, sel_ref, msk_ref, of_ref, ob_ref, state):
    TL = GLA_TILE
    NL = len(GLA_LEVELS)

    @pl.when(pl.program_id(1) == 0)
    def _():
        state[...] = jnp.zeros_like(state)

    lane = lax.broadcasted_iota(jnp.int32, (1, 2 * GLA_DK), 1)
    nt = lambda a, b: lax.dot_general(a, b, (((1,), (1,)), ((), ())), preferred_element_type=F32)
    dirs = ((qf_ref, kf_ref, vf_ref, af_ref, wf_ref, bf_ref, of_ref, TL - 1),
            (qb_ref, kb_ref, vb_ref, ab_ref, wb_ref, bb_ref, ob_ref, 0))
    for d, (q_ref, k_ref, v_ref, a_ref, w_ref, b_ref, o_ref, last) in enumerate(dirs):
        x = jnp.dot(a_ref[0], w_ref[...], preferred_element_type=F32) + b_ref[...]
        g = jax.nn.log_sigmoid(x) * (1.0 / GLA_TAU)
        g1 = g.astype(BF16)
        g2 = (g - g1.astype(F32)).astype(BF16)
        sel = sel_ref[d]
        ps = jnp.dot(sel, g1, preferred_element_type=F32) + jnp.dot(sel, g2, preferred_element_type=F32)
        seg = lambda j: ps[j * TL:(j + 1) * TL]
        q = q_ref[0].astype(F32) * (GLA_DK ** -0.5)
        k = k_ref[0].astype(F32)
        e_pre = [jnp.exp(seg(j)) for j in range(NL + 1)]
        e_suf = [jnp.exp(seg(NL + 1 + j)) for j in range(NL + 1)]
        qh = [(q * e).astype(BF16) for e in e_pre]
        kh = [(k * e).astype(BF16) for e in e_suf]
        kd = (k * jnp.exp(-seg(0))).astype(BF16)
        dec = e_pre[NL][last:last + 1]
        for h in range(GLA_HEADS):
            pr = slice((h // 2) * 2 * GLA_DK, (h // 2 + 1) * 2 * GLA_DK)
            mine = (lane // GLA_DK) == (h % 2)
            qm = [jnp.where(mine, t[:, pr], 0) for t in qh[:NL]]
            sc = msk_ref[d, NL] * nt(qm[0], kd[:, pr])
            for j in range(NL):
                sc = sc + msk_ref[d, j] * nt(qm[j], kh[j][:, pr])
            v = v_ref[0, :, h * GLA_DV:(h + 1) * GLA_DV].astype(BF16)
            st = state[d, h]
            o = jnp.dot(sc.astype(BF16), v, preferred_element_type=F32) + nt(qh[NL][:, pr], st.astype(BF16))
            o_ref[0, :, h * GLA_DV:(h + 1) * GLA_DV] = o
            upd = lax.dot_general(v, kh[NL][:, pr], (((0,), (0,)), ((), ())), preferred_element_type=F32)
            state[d, h] = jnp.where(mine, dec[:, pr] * st + upd, 0)


def gla_scan(P, wa2_f, ba_f, wa2_b, ba_b, n_ctx):
    B, T, _ = P.shape
    assert GLA_TILE == ROW_TILE and T % ROW_TILE == 0 and n_ctx % ROW_TILE == 0
    NT, NC = T // ROW_TILE, n_ctx // ROW_TILE
    sel, msk = gla_constants()
    fwd = lambda b, n: n
    bwd = lambda b, n: jnp.where(n < NC, NC - 1 - n, NT - 1 - (n - NC))
    full = lambda a: pl.BlockSpec(a.shape, lambda b, n: (0,) * a.ndim)
    HK, HV = GLA_HEADS * GLA_DK, GLA_HEADS * GLA_DV
    pad = lambda w, off: jnp.zeros((HEAD_PAD, HK), F32).at[off:off + GLA_RANK].set(w).astype(BF16)
    wf, wb = pad(wa2_f, MISC_AF), pad(wa2_b, MISC_AB)
    ba_f2, ba_b2 = ba_f.reshape(1, HK), ba_b.reshape(1, HK)
    out = lambda rm: pl.BlockSpec((1, ROW_TILE, HV), lambda b, n: (b, rm(b, n), 0))
    return pl.pallas_call(
        _gla_kernel,
        grid=(B, NT),
        in_specs=[pcol('q', fwd), pcol('k', fwd), pcol('v', fwd), pcol('misc', fwd),
                  pcol('q', bwd), pcol('k', bwd), pcol('v', bwd), pcol('misc', bwd),
                  full(wf), full(ba_f2), full(wb), full(ba_b2), full(sel), full(msk)],
        out_specs=[out(fwd), out(bwd)],
        out_shape=[jax.ShapeDtypeStruct((B, T, HV), F32)] * 2,
        scratch_shapes=[pltpu.VMEM((2, GLA_HEADS, GLA_DV, 2 * GLA_DK), F32)],
        compiler_params=pltpu.CompilerParams(dimension_semantics=("arbitrary", "arbitrary"),
                                             vmem_limit_bytes=48 * 1024 * 1024),
        name="gla_scan",
    )(P, P, P, P, P, P, P, P, wf, ba_f2, wb, ba_b2, sel, msk)


def hyena_kernels(L, w1, b1, w2, b2, w3, freq):
    t = jnp.linspace(0.0, 1.0, L, dtype=F32)[:, None]
    w = 2 * math.pi * jnp.arange(L, dtype=F32)[:, None] / L
    f = jnp.linspace(1e-4, HY_BANDS - 1, HY_BANDS, dtype=F32)
    z = jnp.concatenate([t, jnp.cos(f * w), -jnp.sin(f * w)], -1)
    ext = lambda a: jnp.concatenate([a, a[:1], jnp.flip(a[1:], 0)], 0)
    z, t = ext(z), ext(t)
    fr = freq.astype(F32)
    h = jnp.sin(fr * (z @ w1.astype(F32) + b1.astype(F32)))
    h = jnp.sin(fr * (h @ w2.astype(F32) + b2.astype(F32)))
    w3r = w3.astype(F32).reshape(HY_FFN, HY_ORDER, HY_DIRS, HY_W)
    deltas = jnp.abs(jnp.linspace(math.log(HY_DECAY_TARGET) / HY_SLOW_PCT,
                                  math.log(HY_DECAY_TARGET) / HY_FAST_PCT, HY_W, dtype=F32))
    win = jnp.exp(-t * deltas)
    not_tap0 = (jnp.arange(L) > 0)[:, None]
    out = []
    for o in range(HY_ORDER):
        hf = (h[:L] @ w3r[:, o, 0]) * win[:L]
        hb = (h[L:] @ w3r[:, o, 1]) * win[L:]
        norm = jnp.sum(jnp.abs(hf), 0) + jnp.sum(jnp.abs(hb), 0)
        out.append(jnp.concatenate([hf, jnp.where(not_tap0, hb, 0.0)], 0) / (norm * (2 * L)))
    return out


FFT_R = 128
FFT_N = FFT_R * FFT_R
FFT_COLS = 2048


def _dft_parts(n):
    ang = 2.0 * np.pi * np.outer(np.arange(n), np.arange(n)) / n
    return np.cos(ang), np.sin(ang)


def hyena_fft_constants():
    C, S = _dft_parts(FFT_R)
    h = FFT_R // 2
    m_first = np.block([[C[:, :h], S[:, :h]], [-S[:, :h], C[:, :h]]])
    m_first_real = np.concatenate([C, -S], 0)
    m_mid = np.block([[C, S], [-S, C]])
    m_mid_inv = np.block([[C, -S], [S, C]])
    m_last = np.block([[C[:h], -S[:h]], [S[:h], C[:h]]])
    ang = 2.0 * np.pi * np.outer(np.arange(FFT_R), np.arange(FFT_R)) / FFT_N
    tw = np.stack([np.cos(ang), np.sin(ang)], -1)
    bf = lambda a: jnp.asarray(a, BF16)
    return dict(first=bf(m_first), first_real=bf(m_first_real), mid=bf(m_mid), mid_inv=bf(m_mid_inv),
                last=bf(m_last), tw=jnp.asarray(tw, F32))


def _fft_outer_kernel(m_ref, x_ref, o_ref):
    o_ref[...] = jnp.dot(m_ref[...], x_ref[...].astype(BF16), preferred_element_type=F32).astype(o_ref.dtype)


def fft_first(m, x2d):
    R, W = x2d.shape
    return pl.pallas_call(
        _fft_outer_kernel, grid=(W // FFT_COLS,),
        in_specs=[pl.BlockSpec(m.shape, lambda j: (0, 0)), pl.BlockSpec((R, FFT_COLS), lambda j: (0, j))],
        out_specs=pl.BlockSpec((m.shape[0], FFT_COLS), lambda j: (0, j)),
        out_shape=jax.ShapeDtypeStruct((m.shape[0], W), BF16),
        compiler_params=pltpu.CompilerParams(dimension_semantics=("arbitrary",)),
        name="hyena_fft_first")(m, x2d)


def _fft_mid_kernel(*refs, conv):
    if conv:
        a_ref, tw_ref, mf_ref, mi_ref, h_ref, o_ref = refs
    else:
        a_ref, tw_ref, mf_ref, o_ref = refs
    R = FFT_R
    tc = tw_ref[0, :, 0:1]
    ts = tw_ref[0, :, 1:2]
    ar = a_ref[0, 0].astype(F32)
    ai = a_ref[1, 0].astype(F32)
    x = jnp.concatenate([ar * tc + ai * ts, ai * tc - ar * ts], 0).astype(BF16)
    X = jnp.dot(mf_ref[...], x, preferred_element_type=F32)
    if not conv:
        o_ref[0, 0] = X[:R].astype(o_ref.dtype)
        o_ref[1, 0] = X[R:].astype(o_ref.dtype)
        return
    xr, xi = X[:R], X[R:]
    hr, hi = h_ref[0, 0].astype(F32), h_ref[1, 0].astype(F32)
    y = jnp.concatenate([xr * hr - xi * hi, xr * hi + xi * hr], 0).astype(BF16)
    Bm = jnp.dot(mi_ref[...], y, preferred_element_type=F32)
    br, bi = Bm[:R], Bm[R:]
    o_ref[0, 0] = (br * tc - bi * ts).astype(o_ref.dtype)
    o_ref[1, 0] = (bi * tc + br * ts).astype(o_ref.dtype)


def fft_mid(a, consts, h=None):
    C = a.shape[-1]
    slab = pl.BlockSpec((2, 1, FFT_R, C), lambda k: (0, k, 0, 0))
    twspec = pl.BlockSpec((1, FFT_R, 2), lambda k: (k, 0, 0))
    mspec = pl.BlockSpec((2 * FFT_R, 2 * FFT_R), lambda k: (0, 0))
    conv = h is not None
    ins = [a, consts['tw'], consts['mid']] + ([consts['mid_inv'], h] if conv else [])
    specs = [slab, twspec, mspec] + ([mspec, slab] if conv else [])
    return pl.pallas_call(
        functools.partial(_fft_mid_kernel, conv=conv), grid=(FFT_R,),
        in_specs=specs, out_specs=slab,
        out_shape=jax.ShapeDtypeStruct((2, FFT_R, FFT_R, C), BF16),
        compiler_params=pltpu.CompilerParams(dimension_semantics=("arbitrary",)),
        name="hyena_fft_mid")(*ins)


def _fft_last_kernel(m_ref, b_ref, z_ref, bias_ref, gate_ref, o_ref):
    y = jnp.dot(m_ref[...], b_ref[...], preferred_element_type=F32)
    o_ref[...] = gate_ref[...] * (y + bias_ref[...] * z_ref[...])


def fft_last(m, b2d, z2d, bias_row, gate2d):
    R, W = z2d.shape
    col = lambda r: pl.BlockSpec((r, FFT_COLS), lambda j: (0, j))
    return pl.pallas_call(
        _fft_last_kernel, grid=(W // FFT_COLS,),
        in_specs=[pl.BlockSpec(m.shape, lambda j: (0, 0)), col(b2d.shape[0]), col(R),
                  pl.BlockSpec((1, FFT_COLS), lambda j: (0, 0)), col(R)],
        out_specs=col(R),
        out_shape=jax.ShapeDtypeStruct((R, W), F32),
        compiler_params=pltpu.CompilerParams(dimension_semantics=("arbitrary",)),
        name="hyena_fft_last")(m, b2d, z2d, bias_row, gate2d)


def filter_spectrum(k, consts):
    C = k.shape[1]
    a = fft_first(consts['first_real'], k.reshape(FFT_R, FFT_R * C))
    return fft_mid(a.reshape(2, FFT_R, FFT_R, C), consts)


def long_conv(z, gate, spec, bias, consts):
    B, L, C = z.shape
    assert B == 2 and 2 * L == FFT_N and FFT_COLS % C == 0
    z2d = z.reshape(FFT_R, FFT_R * C)
    a = fft_first(consts['first'], z2d)
    b = fft_mid(a.reshape(2, FFT_R, FFT_R, C), consts, spec)
    bias_row = jnp.tile(bias.reshape(1, C), (1, FFT_COLS // C))
    y = fft_last(consts['last'], b.reshape(2 * FFT_R, FFT_R * C), z2d, bias_row, gate.reshape(FFT_R, FFT_R * C))
    return y.reshape(B, L, C)


def dense_dft_constants(L):
    C, S = _dft_parts(2 * L)
    m_fwd = np.block([[C[:, :L], S[:, :L]], [-S[:, :L], C[:, :L]]])
    m_fwd_real = np.concatenate([C, -S], 0)
    m_inv = np.block([[C[:L], -S[:L]], [S[:L], C[:L]]])
    bf = lambda a: jnp.asarray(a, BF16)
    return bf(m_fwd), bf(m_fwd_real), bf(m_inv)


def _short_conv_kernel(mf_ref, mfr_ref, mi_ref, z_ref, k_ref, bias_ref, gate_ref, o_ref):
    n = k_ref.shape[0]
    z = z_ref[...]
    X = jnp.dot(mf_ref[...], z.astype(BF16), preferred_element_type=F32)
    Hs = jnp.dot(mfr_ref[...], k_ref[...].astype(BF16), preferred_element_type=F32)
    xr, xi, hr, hi = X[:n], X[n:], Hs[:n], Hs[n:]
    y = jnp.concatenate([xr * hr - xi * hi, xr * hi + xi * hr], 0).astype(BF16)
    o_ref[...] = gate_ref[...] * (jnp.dot(mi_ref[...], y, preferred_element_type=F32) + bias_ref[...] * z)


def short_conv(z, gate, k, bias):
    B, L, C = z.shape
    assert B == 2
    mf, mfr, mi = dense_dft_constants(L)
    y = pl.pallas_call(
        _short_conv_kernel,
        out_shape=jax.ShapeDtypeStruct((2 * L, C), F32),
        name="hyena_short_conv")(mf, mfr, mi, z.reshape(2 * L, C), k, bias.reshape(1, C), gate.reshape(2 * L, C))
    return y.reshape(B, L, C)


HALO = 8


def _hy_pre_kernel(x_ref, prev_ref, next_ref, w_ref, b_ref, v_ref, x1_ref, x2_ref, *, nt):
    t = pl.program_id(1)
    x = x_ref[0].astype(F32)
    row = lax.broadcasted_iota(jnp.int32, (ROW_TILE, 1), 0)
    prev_row = jnp.where(t > 0, prev_ref[0, HALO - 1:HALO].astype(F32), 0.0)
    next_row = jnp.where(t < nt - 1, next_ref[0, 0:1].astype(F32), 0.0)
    below = jnp.where(row == 0, prev_row, pltpu.roll(x, 1, 0))
    above = jnp.where(row == ROW_TILE - 1, next_row, pltpu.roll(x, ROW_TILE - 1, 0))
    y = below * w_ref[0:1] + x * w_ref[1:2] + above * w_ref[2:3] + b_ref[...]
    v_ref[0] = y[:, :HY_W]
    x1_ref[0] = y[:, HY_W:2 * HY_W]
    x2_ref[0] = y[:, 2 * HY_W:]


def hyena_pre(P, t0, nt, conv_w, conv_b):
    B, T, _ = P.shape
    cb = COL['hy'] // N_HY
    per = ROW_TILE // HALO
    last_blk = T // HALO - 1
    out = pl.BlockSpec((1, ROW_TILE, HY_W), lambda b, t: (b, t, 0))
    return pl.pallas_call(
        functools.partial(_hy_pre_kernel, nt=nt), grid=(B, nt),
        in_specs=[pl.BlockSpec((1, ROW_TILE, N_HY), lambda b, t: (b, t + t0, cb)),
                  pl.BlockSpec((1, HALO, N_HY), lambda b, t: (b, jnp.maximum((t + t0) * per - 1, 0), cb)),
                  pl.BlockSpec((1, HALO, N_HY), lambda b, t: (b, jnp.minimum((t + t0 + 1) * per, last_blk), cb)),
                  pl.BlockSpec((3, N_HY), lambda b, t: (0, 0)),
                  pl.BlockSpec((1, N_HY), lambda b, t: (0, 0))],
        out_specs=[out, out, out],
        out_shape=[jax.ShapeDtypeStruct((B, nt * ROW_TILE, HY_W), F32)] * 3,
        compiler_params=pltpu.CompilerParams(dimension_semantics=("arbitrary", "arbitrary")),
        name="hyena_pre")(P, P, P, conv_w, conv_b.reshape(1, N_HY))


def hyena_branch(P, with_ctx, n_ctx, conv_w, conv_b, w1, b1, w2, b2, w3, freq, hbias):
    B, T, _ = P.shape
    consts = hyena_fft_constants()
    hy_w = (w1, b1, w2, b2, w3, freq)
    nc = n_ctx // ROW_TILE
    v, x1, x2 = hyena_pre(P, nc, T // ROW_TILE - nc, conv_w, conv_b)
    k = hyena_kernels(T - n_ctx, *hy_w)
    z = long_conv(v, x1, filter_spectrum(k[0], consts), hbias[0], consts)
    o = long_conv(z, x2, filter_spectrum(k[1], consts), hbias[1], consts)
    if with_ctx:
        vc, xc1, xc2 = hyena_pre(P, 0, nc, conv_w, conv_b)
        kc = hyena_kernels(n_ctx, *hy_w)
        oc = short_conv(short_conv(vc, xc1, kc[0], hbias[0]), xc2, kc[1], hbias[1])
    else:
        oc = jnp.zeros((B, n_ctx, HY_W), F32)
    return o, oc


ROUTER_PAD = 128


def _merge_kernel(of_ref, ob_ref, r_ref, om_ref, ohl_ref, ohc_ref, g_ref, x_ref, mod_ref,
                  wg_ref, wm_ref, wh_ref, wo_ref, gn_ref, l1g_ref, l1b_ref, rw_ref, rb_ref,
                  x1_ref, h2_ref, lg_ref, *, n_ctx_tiles):
    o = of_ref[0] + ob_ref[0]
    parts = [_rms(o[:, h * GLA_DV:(h + 1) * GLA_DV], gn_ref[...]) for h in range(GLA_HEADS)]
    og = (jnp.concatenate(parts, axis=1) * jax.nn.silu(r_ref[0].astype(F32))).astype(BF16)
    oh = jnp.where(pl.program_id(1) < n_ctx_tiles, ohc_ref[0], ohl_ref[0]).astype(BF16)
    D = D_MODEL
    g = g_ref[0]
    y = (jax.nn.sigmoid(g[:, :D].astype(F32)) * jnp.dot(og, wg_ref[...], preferred_element_type=F32)
         + jax.nn.sigmoid(g[:, D:2 * D].astype(F32)) * jnp.dot(om_ref[0], wm_ref[...], preferred_element_type=F32)
         + jax.nn.sigmoid(g[:, 2 * D:].astype(F32)) * jnp.dot(oh, wh_ref[...], preferred_element_type=F32))
    ym = jnp.dot(y.astype(BF16), wo_ref[...], preferred_element_type=F32)
    gate1, sh2, sc2 = mod_ref[0, 0, 0:1], mod_ref[0, 0, 1:2], mod_ref[0, 0, 2:3]
    x1 = _layer_norm(DEEPNORM_ALPHA * x_ref[0] + gate1 * ym) * l1g_ref[...] + l1b_ref[...]
    x1_ref[0] = x1
    h2f = _layer_norm(x1) * (1.0 + sc2) + sh2
    h2_ref[0] = pack_bf16_pairs(h2f)
    h2 = h2f.astype(BF16)
    lg_ref[0] = jnp.dot(h2, rw_ref[...], preferred_element_type=F32) + rb_ref[...]


def merge_block(o_f, o_b, P, o_mla, o_hy_lat, o_hy_ctx, xa, mod, w_br_gla, w_br_mla, w_br_hy, w_out, gla_norm,
                ln1_g, ln1_b, router_w, router_b, n_ctx):
    B, T, D = xa.shape
    NC = n_ctx // ROW_TILE
    bf = lambda a: a.astype(BF16)
    row = lambda a: a.reshape(1, -1)
    rw = jnp.pad(router_w, ((0, 0), (0, ROUTER_PAD - N_EXPERTS))).astype(BF16)
    rb = jnp.pad(router_b, (0, ROUTER_PAD - N_EXPERTS)).reshape(1, -1)
    full = lambda a: pl.BlockSpec(a.shape, lambda b, t: (0,) * a.ndim)
    tile = lambda w: pl.BlockSpec((1, ROW_TILE, w), lambda b, t: (b, t, 0))
    lat = pl.BlockSpec((1, ROW_TILE, BRANCH_W), lambda b, t: (b, jnp.maximum(t - NC, 0), 0))
    ctx = pl.BlockSpec((1, ROW_TILE, BRANCH_W), lambda b, t: (b, jnp.minimum(t, NC - 1), 0))
    ws = [bf(w_br_gla), bf(w_br_mla), bf(w_br_hy), bf(w_out), row(gla_norm), row(ln1_g), row(ln1_b), rw, rb]
    return pl.pallas_call(
        functools.partial(_merge_kernel, n_ctx_tiles=NC), grid=(B, T // ROW_TILE),
        in_specs=[tile(BRANCH_W), tile(BRANCH_W), pcol('r'), tile(BRANCH_W), lat, ctx, pcol('gate'), tile(D),
                  pl.BlockSpec((1, 1, 3, D), _tile_kind(NC))] + [full(w) for w in ws],
        out_specs=[tile(D), tile(HALF_D), tile(ROUTER_PAD)],
        out_shape=[jax.ShapeDtypeStruct((B, T, D), F32), jax.ShapeDtypeStruct((B, T, HALF_D), U32),
                   jax.ShapeDtypeStruct((B, T, ROUTER_PAD), F32)],
        compiler_params=pltpu.CompilerParams(dimension_semantics=("arbitrary", "arbitrary"),
                                             vmem_limit_bytes=48 * 1024 * 1024),
        name="merge_block")(o_f, o_b, P, o_mla, o_hy_lat, o_hy_ctx, P, xa, mod, *ws)


def kernel(x, c, ctx, c_ctx, ada_w, ada_b, w_in, gla_wa2_f, gla_ba_f, gla_wa2_b, gla_ba_b, gla_norm,
           mla_q_norm, mla_w_uq, mla_kv_norm, mla_w_ukv, hy_conv_w, hy_conv_b, hy_w1, hy_b1, hy_w2, hy_b2,
           hy_w3, hy_freq, hy_bias, w_br_gla, w_br_mla, w_br_hy, w_out, ln1_g, ln1_b, ln2_g, ln2_b,
           router_w, router_b, moe_w1, moe_b1, moe_w2, moe_b2):
    B, L, D = x.shape
    CL = ctx.shape[1]
    T = CL + L
    tables = rope_tables(L, CL)
    xa = jnp.concatenate([ctx, x], axis=1)
    cond = jnp.concatenate([jax.nn.silu(c), jax.nn.silu(c_ctx)[None], jnp.zeros((8 - B - 1, D), F32)], 0)
    for l in range(DEPTH):
        last = l == DEPTH - 1
        ada = pmm(cond, ada_w[l])[:B + 1] + ada_b[l]
        ada = jnp.stack([ada[:B], jnp.broadcast_to(ada[B:], (B, 6 * D))], 1).reshape(B, 2, 6, D)
        P = ln_mod_proj(xa, ada[:, :, 0:2], permute_w_in(w_in[l]), CL)
        o_f, o_b = gla_scan(P, gla_wa2_f[l], gla_ba_f[l], gla_wa2_b[l], gla_ba_b[l], CL)
        q, k, v = mla_prep(P, tables, mla_weights(mla_w_uq[l], mla_w_ukv[l]), mla_q_norm[l], mla_kv_norm[l])
        o_mla = mla_attention(q, k, v, CL)
        o_hy, o_hy_ctx = hyena_branch(P, not last, CL, hy_conv_w[l], hy_conv_b[l], hy_w1[l], hy_b1[l], hy_w2[l],
                                      hy_b2[l], hy_w3[l], hy_freq[l], hy_bias[l])
        x1, h2, logits = merge_block(o_f, o_b, P, o_mla, o_hy, o_hy_ctx, xa, ada[:, :, 2:5],
                                     w_br_gla[l], w_br_mla[l], w_br_hy[l], w_out[l], gla_norm[l],
                                     ln1_g[l], ln1_b[l], router_w[l], router_b[l], CL)
        t0 = CL // ROW_TILE if last else 0
        nt = T // ROW_TILE - t0
        xa = moe_ffn_residual(h2, logits[:, t0 * ROW_TILE:, :N_EXPERTS], x1, ada[:, :, 5:6],
                              moe_w1, moe_b1, moe_w2, moe_b2, l, ln2_g[l], ln2_b[l],
                              t0, nt, CL // ROW_TILE)
    return xa
```
